```python
import jax, jax.numpy as jnp
from jax import lax
import numpy as np

D_MODEL = 2048
BATCH = 4
SEQ = 2048
DEPTH = 1
DEC_BATCH = 128
DEC_SEQ = 8
PAST_LEN = 16384
PAGE_SIZE = 128

CHUNK = 128
A_WIDTH = 1024
A_GROUPS = 4
A_GROUP_DIM = A_WIDTH // A_GROUPS
B_WIDTH = 1024
CONV_W = 3
N_MEM = 256
X_HEADS = 4
X_HEAD_DIM = D_MODEL // X_HEADS
D_FF = 4 * D_MODEL
EPS = 1e-6
IN_WIDTHS = (A_WIDTH, A_WIDTH, B_WIDTH, B_WIDTH, B_WIDTH, D_MODEL, D_MODEL)
IN_WIDTH = sum(IN_WIDTHS)
IN_SPLITS = tuple(int(s) for s in np.cumsum(IN_WIDTHS)[:-1])

kernel_name = "gated_chunkmlp_shortconv_memxattn_step"


def rmsnorm(x, g):
    xf = x.astype(jnp.float32)
    r = lax.rsqrt(jnp.mean(xf * xf, axis=-1, keepdims=True) + EPS)
    return (xf * r).astype(x.dtype) * g


def layernorm(x, g, b):
    xf = x.astype(jnp.float32)
    mu = jnp.mean(xf, axis=-1, keepdims=True)
    var = jnp.mean(jnp.square(xf - mu), axis=-1, keepdims=True)
    return ((xf - mu) * lax.rsqrt(var + EPS)).astype(x.dtype) * g + b


def chunk_spatial_gate(u, v, w_s, b_s):
    n, t, _ = v.shape
    L = CHUNK if t % CHUNK == 0 else t
    nc = t // L
    mask = jnp.tril(jnp.ones((L, L), dtype=bool))
    ws = jnp.where(mask, w_s[:, :L, :L], 0).astype(v.dtype)
    vc = v.reshape(n, nc, L, A_GROUPS, A_GROUP_DIM)
    z = jnp.einsum("gts,bcsgd->bctgd", ws, vc) + b_s[:, :L].T[None, None, :, :, None]
    return u * z.reshape(n, t, A_WIDTH)


def causal_dwconv(p, prev, w):
    t = p.shape[1]
    xp = jnp.concatenate([prev, p], axis=1)
    y = sum(w[k] * xp[:, k:k + t] for k in range(CONV_W))
    return y, xp[:, t:]


def mem_kv(mem, g_mem, w_k, w_v):
    n = mem.shape[0]
    mn = rmsnorm(mem, g_mem)
    k = (mn @ w_k).reshape(n, N_MEM, X_HEADS, X_HEAD_DIM)
    v = (mn @ w_v).reshape(n, N_MEM, X_HEADS, X_HEAD_DIM)
    return k, v


def cross_attn(hn, k, v, w_q, w_xo):
    n, t, _ = hn.shape
    q = (hn @ w_q).reshape(n, t, X_HEADS, X_HEAD_DIM)
    s = jnp.einsum("bthd,bmhd->bhtm", q, k).astype(jnp.float32) * (X_HEAD_DIM ** -0.5)
    p = jax.nn.softmax(s, axis=-1).astype(v.dtype)
    o = jnp.einsum("bhtm,bmhd->bthd", p, v).reshape(n, t, D_MODEL)
    return o @ w_xo


def layer(x, conv_prev, k_mem, v_mem, norm_mix_g, w_in, ln_v_g, ln_v_b, w_spatial, b_spatial,
          conv_w, w_branch_a, w_branch_b, w_mix_out, norm_x_g, w_q, w_x_out, norm_mlp_g, w_up, w_down):
    xn = rmsnorm(x, norm_mix_g)
    u, v, bg, cg, xin, ga, gb = jnp.split(xn @ w_in, IN_SPLITS, axis=-1)
    v = layernorm(v, ln_v_g, ln_v_b)
    y_a = chunk_spatial_gate(u, v, w_spatial, b_spatial)
    conv, conv_state = causal_dwconv(cg * xin, conv_prev, conv_w)
    y_b = bg * conv
    merged = jax.nn.sigmoid(ga) * (y_a @ w_branch_a) + jax.nn.sigmoid(gb) * (y_b @ w_branch_b)
    h = x + merged @ w_mix_out
    h = h + cross_attn(rmsnorm(h, norm_x_g), k_mem, v_mem, w_q, w_x_out)
    h = h + jnp.square(jax.nn.relu(rmsnorm(h, norm_mlp_g) @ w_up)) @ w_down
    return h, v, conv_state


def setup_inputs(seed: int = 0) -> dict:
    key = jax.random.key(seed)
    ks = jax.random.split(key, 32)
    f32 = jnp.float32

    def nrm(k, shape, scale=1.0):
        return jax.random.normal(k, shape, f32) * scale

    def gain(k, shape):
        return 1.0 + 0.1 * jax.random.normal(k, shape, f32)

    return {
        "x_prompt": nrm(ks[0], (BATCH, SEQ, D_MODEL)),
        "x_sample": nrm(ks[1], (DEC_BATCH, DEC_SEQ, D_MODEL)),
        "state_conv": nrm(ks[2], (DEPTH, DEC_BATCH, CONV_W - 1, B_WIDTH)),
        "cache_mem_k": nrm(ks[3], (DEPTH, DEC_BATCH, N_MEM, X_HEADS, X_HEAD_DIM)),
        "cache_mem_v": nrm(ks[4], (DEPTH, DEC_BATCH, N_MEM, X_HEADS, X_HEAD_DIM)),
        "mem_prompt": nrm(ks[5], (BATCH, N_MEM, D_MODEL)),
        "norm_mix_g": gain(ks[6], (DEPTH, D_MODEL)),
        "w_in": nrm(ks[7], (DEPTH, D_MODEL, IN_WIDTH), D_MODEL ** -0.5),
        "ln_v_g": gain(ks[8], (DEPTH, A_WIDTH)),
        "ln_v_b": nrm(ks[9], (DEPTH, A_WIDTH), 0.02),
        "w_spatial": nrm(ks[10], (DEPTH, A_GROUPS, CHUNK, CHUNK), CHUNK ** -0.5),
        "b_spatial": gain(ks[11], (DEPTH, A_GROUPS, CHUNK)),
        "conv_w": nrm(ks[12], (DEPTH, CONV_W, B_WIDTH), CONV_W ** -0.5),
        "w_branch_a": nrm(ks[13], (DEPTH, A_WIDTH, D_MODEL), A_WIDTH ** -0.5),
        "w_branch_b": nrm(ks[14], (DEPTH, B_WIDTH, D_MODEL), B_WIDTH ** -0.5),
        "w_mix_out": nrm(ks[15], (DEPTH, D_MODEL, D_MODEL), D_MODEL ** -0.5),
        "norm_x_g": gain(ks[16], (DEPTH, D_MODEL)),
        "norm_mem_g": gain(ks[17], (DEPTH, D_MODEL)),
        "w_q": nrm(ks[18], (DEPTH, D_MODEL, D_MODEL), D_MODEL ** -0.5),
        "w_k": nrm(ks[19], (DEPTH, D_MODEL, D_MODEL), D_MODEL ** -0.5),
        "w_v": nrm(ks[20], (DEPTH, D_MODEL, D_MODEL), D_MODEL ** -0.5),
        "w_x_out": nrm(ks[21], (DEPTH, D_MODEL, D_MODEL), D_MODEL ** -0.5),
        "norm_mlp_g": gain(ks[22], (DEPTH, D_MODEL)),
        "w_up": nrm(ks[23], (DEPTH, D_MODEL, D_FF), D_MODEL ** -0.5),
        "w_down": nrm(ks[24], (DEPTH, D_FF, D_MODEL), D_FF ** -0.5),
        "norm_final_g": gain(ks[25], (D_MODEL,)),
    }


def reference(x_prompt, x_sample, state_conv, cache_mem_k, cache_mem_v, mem_prompt,
              norm_mix_g, w_in, ln_v_g, ln_v_b, w_spatial, b_spatial, conv_w,
              w_branch_a, w_branch_b, w_mix_out, norm_x_g, norm_mem_g, w_q, w_k, w_v,
              w_x_out, norm_mlp_g, w_up, w_down, norm_final_g):
    hp, hs = x_prompt, x_sample
    mem_k_list, mem_v_list, conv_p_list, conv_s_list, chunk_v_list = [], [], [], [], []
    for l in range(DEPTH):
        lp = (norm_mix_g[l], w_in[l], ln_v_g[l], ln_v_b[l], w_spatial[l], b_spatial[l], conv_w[l],
              w_branch_a[l], w_branch_b[l], w_mix_out[l], norm_x_g[l], w_q[l], w_x_out[l],
              norm_mlp_g[l], w_up[l], w_down[l])
        k_p, v_p = mem_kv(mem_prompt, norm_mem_g[l], w_k[l], w_v[l])
        zero_prev = jnp.zeros((hp.shape[0], CONV_W - 1, B_WIDTH), hp.dtype)
        hp, _, conv_p = layer(hp, zero_prev, k_p, v_p, *lp)
        hs, v_s, conv_s = layer(hs, state_conv[l], cache_mem_k[l], cache_mem_v[l], *lp)
        mem_k_list.append(k_p)
        mem_v_list.append(v_p)
        conv_p_list.append(conv_p)
        conv_s_list.append(conv_s)
        chunk_v_list.append(v_s)
    y_prompt = rmsnorm(hp, norm_final_g)
    y_sample = rmsnorm(hs, norm_final_g)
    mem_k_prompt = jnp.stack(mem_k_list, axis=0)
    mem_v_prompt = jnp.stack(mem_v_list, axis=0)
    conv_prompt = jnp.stack(conv_p_list, axis=0)
    conv_sample = jnp.stack(conv_s_list, axis=0)
    chunk_v_sample = jnp.stack(chunk_v_list, axis=0)
    return (y_prompt, y_sample, mem_k_prompt, mem_v_prompt, conv_prompt, conv_sample, chunk_v_sample)
```

```python
import functools

import jax
import jax.numpy as jnp
from jax import lax
from jax.experimental import pallas as pl
from jax.experimental.pallas import tpu as pltpu

EPS = 1e-6
CHUNK = 128
F32 = jnp.float32
BF16 = jnp.bfloat16

V7X_VMEM_BYTES = 64 * 1024 * 1024
VMEM_LIMIT = V7X_VMEM_BYTES * 7 // 8


def _params(*sem):
    return pltpu.CompilerParams(dimension_semantics=sem, vmem_limit_bytes=VMEM_LIMIT)


def _rms(x, g):
    r = lax.rsqrt(jnp.mean(x * x, axis=-1, keepdims=True) + EPS)
    return (x * r) * g


def _resident(shape):
    return pl.BlockSpec(shape, lambda *_: (0,) * len(shape), pipeline_mode=pl.Buffered(1))


def _rms_matmul_kernel(x_ref, g_ref, w_ref, o_ref, xn_ref):
    @pl.when(pl.program_id(1) == 0)
    def _():
        xn_ref[...] = _rms(x_ref[...], g_ref[...]).astype(BF16)

    o_ref[...] = jnp.dot(xn_ref[...], w_ref[...], preferred_element_type=F32).astype(o_ref.dtype)


def rms_matmul(x, g, w, out_dtype, *, tm, tn, name):
    t, d = x.shape
    n = w.shape[1]
    return pl.pallas_call(
        _rms_matmul_kernel,
        grid=(t // tm, n // tn),
        in_specs=[
            pl.BlockSpec((tm, d), lambda i, j: (i, 0)),
            pl.BlockSpec((1, d), lambda i, j: (0, 0)),
            pl.BlockSpec((d, tn), lambda i, j: (0, j)),
        ],
        out_specs=pl.BlockSpec((tm, tn), lambda i, j: (i, j)),
        out_shape=jax.ShapeDtypeStruct((t, n), out_dtype),
        scratch_shapes=[pltpu.VMEM((tm, d), BF16)],
        compiler_params=_params("parallel", "arbitrary"),
        name=name,
    )(x, g.reshape(1, d), w)


def _mix_kernel(*refs, tm, gate_len, widths, groups, prompt, tiles_per_seq):
    if prompt:
        (x_ref, proj_ref, cgp_ref, xinp_ref, lng_ref, lnb_ref, wsp_ref, bsp_ref, cw_ref,
         wa_ref, wb_ref, wmix_ref, h_ref, tail_ref) = refs
    else:
        (x_ref, proj_ref, e1_ref, e2_ref, lng_ref, lnb_ref, wsp_ref, bsp_ref, cw_ref,
         wa_ref, wb_ref, wmix_ref, h_ref, p_ref, v_ref) = refs
    aw, bw, dm = widths
    o_u, o_v, o_bg, o_cg, o_xin, o_ga, o_gb = (0, aw, 2 * aw, 2 * aw + bw, 2 * aw + 2 * bw,
                                               2 * aw + 3 * bw, 2 * aw + 3 * bw + dm)

    def col(o, w):
        return proj_ref[:, o:o + w].astype(F32)

    v = col(o_v, aw)
    mu = jnp.mean(v, axis=-1, keepdims=True)
    vc = v - mu
    var = jnp.mean(vc * vc, axis=-1, keepdims=True)
    vn = (vc * lax.rsqrt(var + EPS)) * lng_ref[...] + lnb_ref[...]
    if not prompt:
        v_ref[...] = vn
    vb = vn.astype(BF16)
    row = lax.broadcasted_iota(jnp.int32, (tm, tm), 0)
    cidx = lax.broadcasted_iota(jnp.int32, (tm, tm), 1)
    keep = ((row ^ cidx) < gate_len) & (cidx <= row)
    gd = aw // groups
    zs = []
    for g in range(groups):
        ws = jnp.where(keep, wsp_ref[g], 0.0).astype(BF16)
        zs.append(jnp.dot(ws, vb[:, g * gd:(g + 1) * gd], preferred_element_type=F32))
    z = jnp.concatenate(zs, axis=1) + bsp_ref[...]
    y_a = (col(o_u, aw) * z).astype(BF16)

    p = col(o_cg, bw) * col(o_xin, bw)
    trow = lax.broadcasted_iota(jnp.int32, (tm, bw), 0)
    if prompt:
        fresh = pl.program_id(0) % tiles_per_seq == 0
        pp = cgp_ref[...].astype(F32) * xinp_ref[...].astype(F32)
        pp = jnp.where(fresh, 0.0, pp)
        last1 = pp[-1:, :]
        last2 = pp[-2:-1, :]
        e1 = jnp.where(trow == 0, last1, 0.0)
        e2 = jnp.where(trow == 0, last2, jnp.where(trow == 1, last1, 0.0))
        tail_ref[...] = p[tm - 8:, :]
    else:
        trow = trow & (gate_len - 1)
        e1 = e1_ref[...]
        e2 = e2_ref[...]
        p_ref[...] = p
    s1 = jnp.where(trow >= 1, pltpu.roll(p, 1, 0), 0.0) + e1
    s2 = jnp.where(trow >= 2, pltpu.roll(p, 2, 0), 0.0) + e2
    conv = cw_ref[0:1, :] * s2 + cw_ref[1:2, :] * s1 + cw_ref[2:3, :] * p
    y_b = (col(o_bg, bw) * conv).astype(BF16)

    merged = (jax.nn.sigmoid(col(o_ga, dm)) * jnp.dot(y_a, wa_ref[...], preferred_element_type=F32)
              + jax.nn.sigmoid(col(o_gb, dm)) * jnp.dot(y_b, wb_ref[...], preferred_element_type=F32))
    h_ref[...] = x_ref[...] + jnp.dot(merged.astype(BF16), wmix_ref[...], preferred_element_type=F32)


def mix(x, proj, prev, ln_g, ln_b, w_sp, b_sp, conv_w, wa, wb, wmix, *, tm, gate_len, seq_len, name):
    t, dm = x.shape
    aw, bw = wa.shape[0], wb.shape[0]
    groups = w_sp.shape[0]
    prompt = prev is None
    nt = t // tm
    row_spec = lambda w: pl.BlockSpec((tm, w), lambda i: (i, 0))
    in_specs = [row_spec(dm), row_spec(proj.shape[1])]
    if prompt:
        prev_rows = 16
        cg_blk = (2 * aw + bw) // bw
        in_specs += [
            pl.BlockSpec((prev_rows, bw), lambda i: (jnp.maximum(i * (tm // prev_rows) - 1, 0), cg_blk)),
            pl.BlockSpec((prev_rows, bw), lambda i: (jnp.maximum(i * (tm // prev_rows) - 1, 0), cg_blk + 1)),
        ]
        extra = (proj, proj)
    else:
        in_specs += [row_spec(bw), row_spec(bw)]
        extra = prev
    in_specs += [_resident((1, aw)), _resident((1, aw)), _resident(w_sp.shape), _resident(b_sp.shape),
                 _resident(conv_w.shape), _resident(wa.shape), _resident(wb.shape), _resident(wmix.shape)]
    out_specs = [row_spec(dm)]
    out_shape = [jax.ShapeDtypeStruct((t, dm), F32)]
    if prompt:
        out_specs.append(pl.BlockSpec((8, bw), lambda i: (i, 0)))
        out_shape.append(jax.ShapeDtypeStruct((nt * 8, bw), F32))
    else:
        out_specs += [row_spec(bw), row_spec(aw)]
        out_shape += [jax.ShapeDtypeStruct((t, bw), F32), jax.ShapeDtypeStruct((t, aw), F32)]
    kern = functools.partial(_mix_kernel, tm=tm, gate_len=gate_len, widths=(aw, bw, dm), groups=groups,
                             prompt=prompt, tiles_per_seq=max(seq_len // tm, 1))
    return pl.pallas_call(
        kern, grid=(nt,), in_specs=in_specs, out_specs=out_specs, out_shape=out_shape,
        compiler_params=_params("parallel"), name=name,
    )(x, proj, *extra, ln_g.reshape(1, aw), ln_b.reshape(1, aw), w_sp, b_sp, conv_w, wa, wb, wmix)


def _attn_kernel(q_ref, k_ref, v_ref, o_ref, *, n_seq, rows, heads, head_dim):
    scale = head_dim ** -0.5
    for b in range(n_seq):
        r0 = b * rows
        for h in range(heads):
            c0 = h * head_dim
            q = q_ref[r0:r0 + rows, c0:c0 + head_dim].astype(BF16)
            k = k_ref[b, :, c0:c0 + head_dim].astype(BF16)
            s = lax.dot_general(q, k, (((1,), (1,)), ((), ())), preferred_element_type=F32) * scale
            e = jnp.exp(s - jnp.max(s, axis=-1, keepdims=True))
            p = (e / jnp.sum(e, axis=-1, keepdims=True)).astype(BF16)
            v = v_ref[b, :, c0:c0 + head_dim].astype(BF16)
            o = jnp.dot(p, v, preferred_element_type=F32)
            o_ref[r0:r0 + rows, c0:c0 + head_dim] = o.astype(o_ref.dtype)


def attn(q, k, v, *, heads, rows, n_seq, seq_len, name):
    t, dm = q.shape
    m = k.shape[1]
    blk = n_seq * rows
    per_seq = seq_len // rows
    kv_spec = pl.BlockSpec((n_seq, m, dm), lambda i: (i // per_seq, 0, 0))
    kern = functools.partial(_attn_kernel, n_seq=n_seq, rows=rows, heads=heads, head_dim=dm // heads)
    return pl.pallas_call(
        kern, grid=(t // blk,),
        in_specs=[pl.BlockSpec((blk, dm), lambda i: (i, 0)), kv_spec, kv_spec],
        out_specs=pl.BlockSpec((blk, dm), lambda i: (i, 0)),
        out_shape=jax.ShapeDtypeStruct((t, dm), q.dtype),
        compiler_params=_params("parallel"), name=name,
    )(q, k, v)


def _matmul_res_kernel(h_ref, o_ref, w_ref, out_ref):
    out_ref[...] = h_ref[...] + jnp.dot(o_ref[...].astype(BF16), w_ref[...], preferred_element_type=F32)


def matmul_res(h, o, w, *, tm, tn, name):
    t, dm = h.shape
    kdim = o.shape[1]
    return pl.pallas_call(
        _matmul_res_kernel, grid=(t // tm, dm // tn),
        in_specs=[pl.BlockSpec((tm, tn), lambda i, j: (i, j)),
                  pl.BlockSpec((tm, kdim), lambda i, j: (i, 0)),
                  pl.BlockSpec((kdim, tn), lambda i, j: (0, j))],
        out_specs=pl.BlockSpec((tm, tn), lambda i, j: (i, j)),
        out_shape=jax.ShapeDtypeStruct((t, dm), F32),
        compiler_params=_params("parallel", "arbitrary"), name=name,
    )(h, o, w)


def _mlp_kernel(h_ref, g_ref, wup_ref, wdn_ref, gf_ref, y_ref, hn_ref):
    j = pl.program_id(1)

    @pl.when(j == 0)
    def _():
        h = h_ref[...]
        hn_ref[...] = _rms(h, g_ref[...]).astype(BF16)
        y_ref[...] = h

    a = jnp.dot(hn_ref[...], wup_ref[...], preferred_element_type=F32)
    a = jnp.square(jnp.maximum(a, 0.0)).astype(BF16)
    y_ref[...] += jnp.dot(a, wdn_ref[...], preferred_element_type=F32)

    @pl.when(j == pl.num_programs(1) - 1)
    def _():
        y_ref[...] = _rms(y_ref[...], gf_ref[...])


def mlp(h, g, w_up, w_down, g_final, *, tm, fc, name):
    t, dm = h.shape
    dff = w_up.shape[1]
    return pl.pallas_call(
        _mlp_kernel, grid=(t // tm, dff // fc),
        in_specs=[pl.BlockSpec((tm, dm), lambda i, j: (i, 0)),
                  pl.BlockSpec((1, dm), lambda i, j: (0, 0)),
                  pl.BlockSpec((dm, fc), lambda i, j: (0, j)),
                  pl.BlockSpec((fc, dm), lambda i, j: (j, 0)),
                  pl.BlockSpec((1, dm), lambda i, j: (0, 0))],
        out_specs=pl.BlockSpec((tm, dm), lambda i, j: (i, 0)),
        out_shape=jax.ShapeDtypeStruct((t, dm), F32),
        scratch_shapes=[pltpu.VMEM((tm, dm), BF16)],
        compiler_params=_params("parallel", "arbitrary"), name=name,
    )(h, g.reshape(1, dm), w_up, w_down, g_final.reshape(1, dm))


MIX_TM = 256


def _gate_operands(w_spatial, b_spatial, gate_len, a_width):
    groups = w_spatial.shape[0]
    reps = MIX_TM // gate_len
    w_t = jnp.tile(w_spatial[:, :gate_len, :gate_len], (1, reps, reps))
    b_t = jnp.tile(b_spatial[:, :gate_len], (1, reps))
    b_slab = jnp.repeat(b_t.T, a_width // groups, axis=1)
    return w_t, b_slab


def _layer(x, conv_prev, k_mem, v_mem, seq_len, tag, wts, g_final):
    (norm_mix_g, w_in, ln_v_g, ln_v_b, w_spatial, b_spatial, conv_w, w_a, w_b, w_mix,
     norm_x_g, w_q, w_xo, norm_mlp_g, w_up, w_down) = wts
    t, dm = x.shape
    aw = w_a.shape[0]
    heads = k_mem.shape[2]
    prompt = conv_prev is None
    gate_len = CHUNK if seq_len % CHUNK == 0 else seq_len
    assert gate_len & (gate_len - 1) == 0 and MIX_TM % gate_len == 0
    tm = min(1024, t)

    proj = rms_matmul(x, norm_mix_g, w_in, BF16, tm=tm, tn=1024, name=f"in_proj_{tag}")
    w_t, b_slab = _gate_operands(w_spatial, b_spatial, gate_len, aw)
    if prompt:
        prev = None
    else:
        e1 = jnp.pad(conv_prev[:, 1:2], ((0, 0), (0, seq_len - 1), (0, 0))).reshape(t, -1)
        e2 = jnp.pad(conv_prev, ((0, 0), (0, seq_len - 2), (0, 0))).reshape(t, -1)
        prev = (e1, e2)
    outs = mix(x, proj, prev, ln_v_g, ln_v_b, w_t, b_slab, conv_w, w_a, w_b, w_mix,
               tm=MIX_TM, gate_len=gate_len, seq_len=seq_len, name=f"mix_{tag}")
    h = outs[0]

    n_seq_total = t // seq_len
    k2 = k_mem.reshape(n_seq_total, k_mem.shape[1], dm)
    v2 = v_mem.reshape(n_seq_total, v_mem.shape[1], dm)
    if prompt:
        q = rms_matmul(h, norm_x_g, w_q, BF16, tm=tm, tn=1024, name=f"q_proj_{tag}")
        o = attn(q, k2, v2, heads=heads, rows=512, n_seq=1, seq_len=seq_len, name=f"attn_{tag}")
    else:
        q = rms_matmul(h, norm_x_g, w_q, F32, tm=tm, tn=1024, name=f"q_proj_{tag}")
        o = attn(q, k2, v2, heads=heads, rows=seq_len, n_seq=2, seq_len=seq_len, name=f"attn_{tag}")
    h = matmul_res(h, o, w_xo, tm=tm, tn=1024, name=f"x_out_{tag}")
    y = mlp(h, norm_mlp_g, w_up, w_down, g_final, tm=tm, fc=512, name=f"mlp_{tag}")
    return y, outs[1:]


def kernel(x_prompt, x_sample, state_conv, cache_mem_k, cache_mem_v, mem_prompt, norm_mix_g, w_in, ln_v_g, ln_v_b, w_spatial, b_spatial, conv_w, w_branch_a, w_branch_b, w_mix_out, norm_x_g, norm_mem_g, w_q, w_k, w_v, w_x_out, norm_mlp_g, w_up, w_down, norm_final_g):
    depth = w_in.shape[0]
    assert depth == 1, "the final rmsnorm is fused into the single layer's MLP kernel"
    nb, seq, dm = x_prompt.shape
    nd, dseq, _ = x_sample.shape
    n_mem, heads, hd = cache_mem_k.shape[2:]
    l = 0
    bf = lambda w: w[l].astype(BF16)
    wts = (norm_mix_g[l], bf(w_in), ln_v_g[l], ln_v_b[l], w_spatial[l], b_spatial[l], conv_w[l],
           bf(w_branch_a), bf(w_branch_b), bf(w_mix_out), norm_x_g[l], bf(w_q), bf(w_x_out),
           norm_mlp_g[l], bf(w_up), bf(w_down))

    mem = mem_prompt.reshape(nb * n_mem, dm)
    k_p = rms_matmul(mem, norm_mem_g[l], bf(w_k), F32, tm=nb * n_mem, tn=1024, name="mem_k")
    v_p = rms_matmul(mem, norm_mem_g[l], bf(w_v), F32, tm=nb * n_mem, tn=1024, name="mem_v")
    k_p = k_p.reshape(nb, n_mem, heads, hd)
    v_p = v_p.reshape(nb, n_mem, heads, hd)

    y_p, (tail_p,) = _layer(x_prompt.reshape(nb * seq, dm), None, k_p, v_p, seq, "prompt", wts, norm_final_g)
    y_s, (p_s, vn_s) = _layer(x_sample.reshape(nd * dseq, dm), state_conv[l], cache_mem_k[l], cache_mem_v[l],
                              dseq, "sample", wts, norm_final_g)

    bw = p_s.shape[1]
    keep = conv_w.shape[1] - 1
    conv_p = tail_p.reshape(nb, seq // MIX_TM, 8, bw)[:, -1, 8 - keep:, :]
    conv_s = p_s.reshape(nd, dseq, bw)[:, dseq - keep:, :]
    return (y_p.reshape(nb, seq, dm), y_s.reshape(nd, dseq, dm), k_p[None], v_p[None],
            conv_p[None], conv_s[None], vn_s.reshape(1, nd, dseq, -1))
```

```python
import functools

import jax
import jax.numpy as jnp
from jax import lax
from jax.experimental import pallas as pl
from jax.experimental.pallas import tpu as pltpu

EPS = 1e-6
CHUNK = 128
F32 = jnp.float32
BF16 = jnp.bfloat16

V7X_VMEM_BYTES = 64 * 1024 * 1024
VMEM_LIMIT = V7X_VMEM_BYTES * 7 // 8


def _params(*sem):
    return pltpu.CompilerParams(dimension_semantics=sem, vmem_limit_bytes=VMEM_LIMIT)


def _rms(x, g):
    r = lax.rsqrt(jnp.mean(x * x, axis=-1, keepdims=True) + EPS)
    return (x * r) * g


def _resident(shape):
    return pl.BlockSpec(shape, lambda *_: (0,) * len(shape), pipeline_mode=pl.Buffered(1))


def _rms_matmul_kernel(x_ref, g_ref, w_ref, o_ref, xn_ref):
    @pl.when(pl.program_id(1) == 0)
    def _():
        xn_ref[...] = _rms(x_ref[...], g_ref[...]).astype(BF16)

    o_ref[...] = jnp.dot(xn_ref[...], w_ref[...], preferred_element_type=F32).astype(o_ref.dtype)


def rms_matmul(x, g, w, out_dtype, *, tm, tn, name):
    t, d = x.shape
    n = w.shape[1]
    return pl.pallas_call(
        _rms_matmul_kernel,
        grid=(t // tm, n // tn),
        in_specs=[
            pl.BlockSpec((tm, d), lambda i, j: (i, 0)),
            pl.BlockSpec((1, d), lambda i, j: (0, 0)),
            pl.BlockSpec((d, tn), lambda i, j: (0, j)),
        ],
        out_specs=pl.BlockSpec((tm, tn), lambda i, j: (i, j)),
        out_shape=jax.ShapeDtypeStruct((t, n), out_dtype),
        scratch_shapes=[pltpu.VMEM((tm, d), BF16)],
        compiler_params=_params("parallel", "arbitrary"),
        name=name,
    )(x, g.reshape(1, d), w)


def _mix_kernel(*refs, tm, gate_len, widths, groups, prompt, tiles_per_seq):
    if prompt:
        (x_ref, proj_ref, cgp_ref, xinp_ref, lng_ref, lnb_ref, wsp_ref, bsp_ref, cw_ref,
         wa_ref, wb_ref, wmix_ref, h_ref, tail_ref) = refs
    else:
        (x_ref, proj_ref, e1_ref, e2_ref, lng_ref, lnb_ref, wsp_ref, bsp_ref, cw_ref,
         wa_ref, wb_ref, wmix_ref, h_ref, p_ref, v_ref) = refs
    aw, bw, dm = widths
    o_u, o_v, o_bg, o_cg, o_xin, o_ga, o_gb = (0, aw, 2 * aw, 2 * aw + bw, 2 * aw + 2 * bw,
                                               2 * aw + 3 * bw, 2 * aw + 3 * bw + dm)

    def col(o, w):
        return proj_ref[:, o:o + w].astype(F32)

    v = col(o_v, aw)
    mu = jnp.mean(v, axis=-1, keepdims=True)
    vc = v - mu
    var = jnp.mean(vc * vc, axis=-1, keepdims=True)
    vn = (vc * lax.rsqrt(var + EPS)) * lng_ref[...] + lnb_ref[...]
    if not prompt:
        v_ref[...] = vn
    vb = vn.astype(BF16)
    row = lax.broadcasted_iota(jnp.int32, (tm, tm), 0)
    cidx = lax.broadcasted_iota(jnp.int32, (tm, tm), 1)
    keep = ((row ^ cidx) < gate_len) & (cidx <= row)
    gd = aw // groups
    zs = []
    for g in range(groups):
        ws = jnp.where(keep, wsp_ref[g], 0.0).astype(BF16)
        zs.append(jnp.dot(ws, vb[:, g * gd:(g + 1) * gd], preferred_element_type=F32))
    z = jnp.concatenate(zs, axis=1) + bsp_ref[...]
    y_a = (col(o_u, aw) * z).astype(BF16)

    p = col(o_cg, bw) * col(o_xin, bw)
    trow = lax.broadcasted_iota(jnp.int32, (tm, bw), 0)
    if prompt:
        fresh = pl.program_id(0) % tiles_per_seq == 0
        pp = cgp_ref[...].astype(F32) * xinp_ref[...].astype(F32)
        pp = jnp.where(fresh, 0.0, pp)
        last1 = pp[-1:, :]
        last2 = pp[-2:-1, :]
        e1 = jnp.where(trow == 0, last1, 0.0)
        e2 = jnp.where(trow == 0, last2, jnp.where(trow == 1, last1, 0.0))
        tail_ref[...] = p[tm - 8:, :]
    else:
        trow = trow & (gate_len - 1)
        e1 = e1_ref[...]
        e2 = e2_ref[...]
        p_ref[...] = p
    s1 = jnp.where(trow >= 1, pltpu.roll(p, 1, 0), 0.0) + e1
    s2 = jnp.where(trow >= 2, pltpu.roll(p, 2, 0), 0.0) + e2
    conv = cw_ref[0:1, :] * s2 + cw_ref[1:2, :] * s1 + cw_ref[2:3, :] * p
    y_b = (col(o_bg, bw) * conv).astype(BF16)

    merged = (jax.nn.sigmoid(col(o_ga, dm)) * jnp.dot(y_a, wa_ref[...], preferred_element_type=F32)
              + jax.nn.sigmoid(col(o_gb, dm)) * jnp.dot(y_b, wb_ref[...], preferred_element_type=F32))
    h_ref[...] = x_ref[...] + jnp.dot(merged.astype(BF16), wmix_ref[...], preferred_element_type=F32)


def mix(x, proj, prev, ln_g, ln_b, w_sp, b_sp, conv_w, wa, wb, wmix, *, tm, gate_len, seq_len, name):
    t, dm = x.shape
    aw, bw = wa.shape[0], wb.shape[0]
    groups = w_sp.shape[0]
    prompt = prev is None
    nt = t // tm
    row_spec = lambda w: pl.BlockSpec((tm, w), lambda i: (i, 0))
    in_specs = [row_spec(dm), row_spec(proj.shape[1])]
    if prompt:
        prev_rows = 16
        cg_blk = (2 * aw + bw) // bw
        in_specs += [
            pl.BlockSpec((prev_rows, bw), lambda i: (jnp.maximum(i * (tm // prev_rows) - 1, 0), cg_blk)),
            pl.BlockSpec((prev_rows, bw), lambda i: (jnp.maximum(i * (tm // prev_rows) - 1, 0), cg_blk + 1)),
        ]
        extra = (proj, proj)
    else:
        in_specs += [row_spec(bw), row_spec(bw)]
        extra = prev
    in_specs += [_resident((1, aw)), _resident((1, aw)), _resident(w_sp.shape), _resident(b_sp.shape),
                 _resident(conv_w.shape), _resident(wa.shape), _resident(wb.shape), _resident(wmix.shape)]
    out_specs = [row_spec(dm)]
    out_shape = [jax.ShapeDtypeStruct((t, dm), F32)]
    if prompt:
        out_specs.append(pl.BlockSpec((8, bw), lambda i: (i, 0)))
        out_shape.append(jax.ShapeDtypeStruct((nt * 8, bw), F32))
    else:
        out_specs += [row_spec(bw), row_spec(aw)]
        out_shape += [jax.ShapeDtypeStruct((t, bw), F32), jax.ShapeDtypeStruct((t, aw), F32)]
    kern = functools.partial(_mix_kernel, tm=tm, gate_len=gate_len, widths=(aw, bw, dm), groups=groups,
                             prompt=prompt, tiles_per_seq=max(seq_len // tm, 1))
    return pl.pallas_call(
        kern, grid=(nt,), in_specs=in_specs, out_specs=out_specs, out_shape=out_shape,
        compiler_params=_params("parallel"), name=name,
    )(x, proj, *extra, ln_g.reshape(1, aw), ln_b.reshape(1, aw), w_sp, b_sp, conv_w, wa, wb, wmix)


def _attn_kernel(q_ref, k_ref, v_ref, o_ref, *, n_seq, rows, heads, head_dim):
    scale = head_dim ** -0.5
    for b in range(n_seq):
        r0 = b * rows
        for h in range(heads):
            c0 = h * head_dim
            q = q_ref[r0:r0 + rows, c0:c0 + head_dim].astype(BF16)
            k = k_ref[b, :, c0:c0 + head_dim].astype(BF16)
            s = lax.dot_general(q, k, (((1,), (1,)), ((), ())), preferred_element_type=F32) * scale
            e = jnp.exp(s - jnp.max(s, axis=-1, keepdims=True))
            p = (e / jnp.sum(e, axis=-1, keepdims=True)).astype(BF16)
            v = v_ref[b, :, c0:c0 + head_dim].astype(BF16)
            o = jnp.dot(p, v, preferred_element_type=F32)
            o_ref[r0:r0 + rows, c0:c0 + head_dim] = o.astype(o_ref.dtype)


def attn(q, k, v, *, heads, rows, n_seq, seq_len, name):
    t, dm = q.shape
    m = k.shape[1]
    blk = n_seq * rows
    per_seq = seq_len // rows
    kv_spec = pl.BlockSpec((n_seq, m, dm), lambda i: (i // per_seq, 0, 0))
    kern = functools.partial(_attn_kernel, n_seq=n_seq, rows=rows, heads=heads, head_dim=dm // heads)
    return pl.pallas_call(
        kern, grid=(t // blk,),
        in_specs=[pl.BlockSpec((blk, dm), lambda i: (i, 0)), kv_spec, kv_spec],
        out_specs=pl.BlockSpec((blk, dm), lambda i: (i, 0)),
        out_shape=jax.ShapeDtypeStruct((t, dm), q.dtype),
        compiler_params=_params("parallel"), name=name,
    )(q, k, v)


LANES = 128


def _attn_rows_kernel(q_ref, k_ref, v_ref, o_ref, *, n_seq, rows, heads, head_dim, n_mem):
    scale = head_dim ** -0.5
    slabs = head_dim // LANES
    per_seq = n_mem * slabs * heads
    width = n_mem * slabs
    pairs = [(b, h) for b in range(n_seq) for h in range(heads)]
    lane_slab = lax.broadcasted_iota(jnp.int32, (rows, width), 1) & (slabs - 1)
    is_slab = [lane_slab == c for c in range(slabs)]

    parts = []
    for b, h in pairs:
        r0, c0 = b * rows, h * head_dim
        qh = jnp.concatenate(
            [q_ref[r0:r0 + rows, c0 + c * LANES:c0 + (c + 1) * LANES] for c in range(slabs)], axis=0
        ).astype(BF16)
        kh = k_ref[pl.ds(b * per_seq + h, width, stride=heads), :].astype(BF16)
        g = lax.dot_general(qh, kh, (((1,), (1,)), ((), ())), preferred_element_type=F32)
        s = g[0:rows]
        for c in range(1, slabs):
            s = jnp.where(is_slab[c], g[c * rows:(c + 1) * rows], s)
        parts.append(s)
    s = jnp.concatenate(parts, axis=0)

    lane = lax.broadcasted_iota(jnp.int32, s.shape, 1)
    step = 1
    while step < slabs:
        s = s + jnp.where((lane & step) != 0, pltpu.roll(s, step, 1), pltpu.roll(s, width - step, 1))
        step *= 2
    s = s * scale
    e = jnp.exp(s - jnp.max(s, axis=-1, keepdims=True))
    p = e / (jnp.sum(e, axis=-1, keepdims=True) * (1.0 / slabs))

    for i, (b, h) in enumerate(pairs):
        r0, c0 = b * rows, h * head_dim
        ph = p[i * rows:(i + 1) * rows]
        w = jnp.concatenate([jnp.where(is_slab[c], ph, 0.0) for c in range(slabs)], axis=0).astype(BF16)
        vh = v_ref[pl.ds(b * per_seq + h, width, stride=heads), :].astype(BF16)
        o = jnp.dot(w, vh, preferred_element_type=F32)
        for c in range(slabs):
            o_ref[r0:r0 + rows, c0 + c * LANES:c0 + (c + 1) * LANES] = o[c * rows:(c + 1) * rows]


def _rows_view(a):
    n, m, heads, hd = a.shape
    a = a.reshape(n * m, heads, hd // LANES, LANES)
    return a.transpose(0, 2, 1, 3).reshape(n * m * hd // LANES * heads, LANES)


def attn_rows(q, k, v, *, n_seq, name):
    n, n_mem, heads, hd = k.shape
    t, dm = q.shape
    rows = t // n
    slabs = hd // LANES
    assert slabs & (slabs - 1) == 0 and rows % 8 == 0
    per_seq = n_mem * slabs * heads
    kv_spec = pl.BlockSpec((n_seq * per_seq, LANES), lambda i: (i, 0))
    q_spec = pl.BlockSpec((n_seq * rows, dm), lambda i: (i, 0))
    kern = functools.partial(_attn_rows_kernel, n_seq=n_seq, rows=rows, heads=heads, head_dim=hd, n_mem=n_mem)
    return pl.pallas_call(
        kern, grid=(n // n_seq,),
        in_specs=[q_spec, kv_spec, kv_spec], out_specs=q_spec,
        out_shape=jax.ShapeDtypeStruct((t, dm), F32),
        compiler_params=_params("parallel"), name=name,
    )(q, _rows_view(k), _rows_view(v))


def _matmul_res_kernel(h_ref, o_ref, w_ref, out_ref):
    out_ref[...] = h_ref[...] + jnp.dot(o_ref[...].astype(BF16), w_ref[...], preferred_element_type=F32)


def matmul_res(h, o, w, *, tm, tn, name):
    t, dm = h.shape
    kdim = o.shape[1]
    return pl.pallas_call(
        _matmul_res_kernel, grid=(t // tm, dm // tn),
        in_specs=[pl.BlockSpec((tm, tn), lambda i, j: (i, j)),
                  pl.BlockSpec((tm, kdim), lambda i, j: (i, 0)),
                  pl.BlockSpec((kdim, tn), lambda i, j: (0, j))],
        out_specs=pl.BlockSpec((tm, tn), lambda i, j: (i, j)),
        out_shape=jax.ShapeDtypeStruct((t, dm), F32),
        compiler_params=_params("parallel", "arbitrary"), name=name,
    )(h, o, w)


def _mlp_kernel(h_ref, g_ref, wup_ref, wdn_ref, gf_ref, y_ref, hn_ref):
    j = pl.program_id(1)

    @pl.when(j == 0)
    def _():
        h = h_ref[...]
        hn_ref[...] = _rms(h, g_ref[...]).astype(BF16)
        y_ref[...] = h

    a = jnp.dot(hn_ref[...], wup_ref[...], preferred_element_type=F32)
    a = jnp.square(jnp.maximum(a, 0.0)).astype(BF16)
    y_ref[...] += jnp.dot(a, wdn_ref[...], preferred_element_type=F32)

    @pl.when(j == pl.num_programs(1) - 1)
    def _():
        y_ref[...] = _rms(y_ref[...], gf_ref[...])


def mlp(h, g, w_up, w_down, g_final, *, tm, fc, name):
    t, dm = h.shape
    dff = w_up.shape[1]
    return pl.pallas_call(
        _mlp_kernel, grid=(t // tm, dff // fc),
        in_specs=[pl.BlockSpec((tm, dm), lambda i, j: (i, 0)),
                  pl.BlockSpec((1, dm), lambda i, j: (0, 0)),
                  pl.BlockSpec((dm, fc), lambda i, j: (0, j)),
                  pl.BlockSpec((fc, dm), lambda i, j: (j, 0)),
                  pl.BlockSpec((1, dm), lambda i, j: (0, 0))],
        out_specs=pl.BlockSpec((tm, dm), lambda i, j: (i, 0)),
        out_shape=jax.ShapeDtypeStruct((t, dm), F32),
        scratch_shapes=[pltpu.VMEM((tm, dm), BF16)],
        compiler_params=_params("parallel", "arbitrary"), name=name,
    )(h, g.reshape(1, dm), w_up, w_down, g_final.reshape(1, dm))


MIX_TM = 256


def _gate_operands(w_spatial, b_spatial, gate_len, a_width):
    groups = w_spatial.shape[0]
    reps = MIX_TM // gate_len
    w_t = jnp.tile(w_spatial[:, :gate_len, :gate_len], (1, reps, reps))
    b_t = jnp.tile(b_spatial[:, :gate_len], (1, reps))
    b_slab = jnp.repeat(b_t.T, a_width // groups, axis=1)
    return w_t, b_slab


def _layer(x, conv_prev, k_mem, v_mem, seq_len, tag, wts, g_final):
    (norm_mix_g, w_in, ln_v_g, ln_v_b, w_spatial, b_spatial, conv_w, w_a, w_b, w_mix,
     norm_x_g, w_q, w_xo, norm_mlp_g, w_up, w_down) = wts
    t, dm = x.shape
    aw = w_a.shape[0]
    heads = k_mem.shape[2]
    prompt = conv_prev is None
    gate_len = CHUNK if seq_len % CHUNK == 0 else seq_len
    assert gate_len & (gate_len - 1) == 0 and MIX_TM % gate_len == 0
    tm = min(1024, t)

    proj = rms_matmul(x, norm_mix_g, w_in, BF16, tm=tm, tn=1024, name=f"in_proj_{tag}")
    w_t, b_slab = _gate_operands(w_spatial, b_spatial, gate_len, aw)
    if prompt:
        prev = None
    else:
        e1 = jnp.pad(conv_prev[:, 1:2], ((0, 0), (0, seq_len - 1), (0, 0))).reshape(t, -1)
        e2 = jnp.pad(conv_prev, ((0, 0), (0, seq_len - 2), (0, 0))).reshape(t, -1)
        prev = (e1, e2)
    outs = mix(x, proj, prev, ln_v_g, ln_v_b, w_t, b_slab, conv_w, w_a, w_b, w_mix,
               tm=MIX_TM, gate_len=gate_len, seq_len=seq_len, name=f"mix_{tag}")
    h = outs[0]

    if prompt:
        n_seq_total = t // seq_len
        k2 = k_mem.reshape(n_seq_total, k_mem.shape[1], dm)
        v2 = v_mem.reshape(n_seq_total, v_mem.shape[1], dm)
        q = rms_matmul(h, norm_x_g, w_q, BF16, tm=tm, tn=1024, name=f"q_proj_{tag}")
        o = attn(q, k2, v2, heads=heads, rows=512, n_seq=1, seq_len=seq_len, name=f"attn_{tag}")
    else:
        q = rms_matmul(h, norm_x_g, w_q, F32, tm=tm, tn=1024, name=f"q_proj_{tag}")
        o = attn_rows(q, k_mem, v_mem, n_seq=4, name=f"attn_{tag}")
    h = matmul_res(h, o, w_xo, tm=tm, tn=1024, name=f"x_out_{tag}")
    y = mlp(h, norm_mlp_g, w_up, w_down, g_final, tm=tm, fc=512, name=f"mlp_{tag}")
    return y, outs[1:]


def kernel(x_prompt, x_sample, state_conv, cache_mem_k, cache_mem_v, mem_prompt, norm_mix_g, w_in, ln_v_g, ln_v_b, w_spatial, b_spatial, conv_w, w_branch_a, w_branch_b, w_mix_out, norm_x_g, norm_mem_g, w_q, w_k, w_v, w_x_out, norm_mlp_g, w_up, w_down, norm_final_g):
    depth = w_in.shape[0]
    assert depth == 1, "the final rmsnorm is fused into the single layer's MLP kernel"
    nb, seq, dm = x_prompt.shape
    nd, dseq, _ = x_sample.shape
    n_mem, heads, hd = cache_mem_k.shape[2:]
    l = 0
    bf = lambda w: w[l].astype(BF16)
    wts = (norm_mix_g[l], bf(w_in), ln_v_g[l], ln_v_b[l], w_spatial[l], b_spatial[l], conv_w[l],
           bf(w_branch_a), bf(w_branch_b), bf(w_mix_out), norm_x_g[l], bf(w_q), bf(w_x_out),
           norm_mlp_g[l], bf(w_up), bf(w_down))

    mem = mem_prompt.reshape(nb * n_mem, dm)
    k_p = rms_matmul(mem, norm_mem_g[l], bf(w_k), F32, tm=nb * n_mem, tn=1024, name="mem_k")
    v_p = rms_matmul(mem, norm_mem_g[l], bf(w_v), F32, tm=nb * n_mem, tn=1024, name="mem_v")
    k_p = k_p.reshape(nb, n_mem, heads, hd)
    v_p = v_p.reshape(nb, n_mem, heads, hd)

    y_p, (tail_p,) = _layer(x_prompt.reshape(nb * seq, dm), None, k_p, v_p, seq, "prompt", wts, norm_final_g)
    y_s, (p_s, vn_s) = _layer(x_sample.reshape(nd * dseq, dm), state_conv[l], cache_mem_k[l], cache_mem_v[l],
                              dseq, "sample", wts, norm_final_g)

    bw = p_s.shape[1]
    keep = conv_w.shape[1] - 1
    conv_p = tail_p.reshape(nb, seq // MIX_TM, 8, bw)[:, -1, 8 - keep:, :]
    conv_s = p_s.reshape(nd, dseq, bw)[:, dseq - keep:, :]
    return (y_p.reshape(nb, seq, dm), y_s.reshape(nd, dseq, dm), k_p[None], v_p[None],
            conv_p[None], conv_s[None], vn_s.reshape(1, nd, dseq, -1))
```

```python
import functools

import jax
import jax.numpy as jnp
from jax import lax
from jax.experimental import pallas as pl
from jax.experimental.pallas import tpu as pltpu

EPS = 1e-6
CHUNK = 128
F32 = jnp.float32
BF16 = jnp.bfloat16

V7X_VMEM_BYTES = 64 * 1024 * 1024
VMEM_LIMIT = V7X_VMEM_BYTES * 7 // 8


def _params(*sem):
    return pltpu.CompilerParams(dimension_semantics=sem, vmem_limit_bytes=VMEM_LIMIT)


def _rms(x, g):
    r = lax.rsqrt(jnp.mean(x * x, axis=-1, keepdims=True) + EPS)
    return (x * r) * g


def _resident(shape):
    return pl.BlockSpec(shape, lambda *_: (0,) * len(shape), pipeline_mode=pl.Buffered(1))


def _bf16_tile(w_ref, wb_ref):
    if wb_ref is None:
        return w_ref[...].astype(BF16)
    wb_ref[...] = w_ref[...].astype(BF16)
    return wb_ref[...]


def _rms_matmul_kernel(x_ref, g_ref, w_ref, *rest, emit_w):
    o_ref, wb_ref, xn_ref = rest if emit_w else (rest[0], None, rest[1])

    @pl.when(pl.program_id(1) == 0)
    def _():
        xn_ref[...] = _rms(x_ref[...], g_ref[...]).astype(BF16)

    w = _bf16_tile(w_ref, wb_ref)
    o_ref[...] = jnp.dot(xn_ref[...], w, preferred_element_type=F32).astype(o_ref.dtype)


def rms_matmul(x, g, w, out_dtype, *, tm, tn, name, emit_w=False):
    t, d = x.shape
    n = w.shape[1]
    one_tile = t == tm
    assert one_tile or not emit_w
    x_spec = (pl.BlockSpec((tm, d), lambda i, j: (i, 0), pipeline_mode=pl.Buffered(1)) if one_tile
              else pl.BlockSpec((tm, d), lambda i, j: (i, 0)))
    w_spec = pl.BlockSpec((d, tn), lambda i, j: (0, j))
    out_specs = [pl.BlockSpec((tm, tn), lambda i, j: (i, j))]
    out_shape = [jax.ShapeDtypeStruct((t, n), out_dtype)]
    if emit_w:
        out_specs.append(w_spec)
        out_shape.append(jax.ShapeDtypeStruct(w.shape, BF16))
    outs = pl.pallas_call(
        functools.partial(_rms_matmul_kernel, emit_w=emit_w),
        grid=(t // tm, n // tn),
        in_specs=[x_spec, pl.BlockSpec((1, d), lambda i, j: (0, 0)), w_spec],
        out_specs=out_specs, out_shape=out_shape,
        scratch_shapes=[pltpu.VMEM((tm, d), BF16)],
        compiler_params=_params("parallel", "arbitrary"),
        name=name,
    )(x, g.reshape(1, d), w)
    return outs if emit_w else outs[0]


def _mix_kernel(*refs, tm, gate_len, widths, groups, prompt, tiles_per_seq):
    if prompt:
        (x_ref, proj_ref, cgp_ref, xinp_ref, lng_ref, lnb_ref, wsp_ref, bsp_ref, cw_ref,
         wa_ref, wb_ref, wmix_ref, h_ref, tail_ref, gate_ref, gbias_ref) = refs
    else:
        (x_ref, proj_ref, e1_ref, e2_ref, lng_ref, lnb_ref, wsp_ref, bsp_ref, cw_ref,
         wa_ref, wb_ref, wmix_ref, h_ref, p_ref, v_ref, gate_ref, gbias_ref) = refs
    aw, bw, dm = widths
    o_u, o_v, o_bg, o_cg, o_xin, o_ga, o_gb = (0, aw, 2 * aw, 2 * aw + bw, 2 * aw + 2 * bw,
                                               2 * aw + 3 * bw, 2 * aw + 3 * bw + dm)

    def col(o, w):
        return proj_ref[:, o:o + w].astype(F32)

    v = col(o_v, aw)
    mu = jnp.mean(v, axis=-1, keepdims=True)
    vc = v - mu
    var = jnp.mean(vc * vc, axis=-1, keepdims=True)
    vn = (vc * lax.rsqrt(var + EPS)) * lng_ref[...] + lnb_ref[...]
    if not prompt:
        v_ref[...] = vn
    vb = vn.astype(BF16)
    gd = aw // groups

    @pl.when(pl.program_id(0) == 0)
    def _():
        row = lax.broadcasted_iota(jnp.int32, (tm, tm), 0)
        cidx = lax.broadcasted_iota(jnp.int32, (tm, tm), 1)
        keep = ((row ^ cidx) < gate_len) & (cidx <= row)
        cs = wsp_ref.shape[1]
        pick = (lax.broadcasted_iota(jnp.int32, (tm, cs), 1)
                == (lax.broadcasted_iota(jnp.int32, (tm, cs), 0) & (gate_len - 1)))
        pick_b = jnp.where(pick, 1.0, 0.0).astype(BF16)
        for g in range(groups):
            w_rows = jnp.dot(pick_b, wsp_ref[g].astype(BF16), preferred_element_type=F32).astype(BF16)
            w_full = lax.dot_general(w_rows, pick_b, (((1,), (1,)), ((), ())), preferred_element_type=F32)
            gate_ref[g] = jnp.where(keep, w_full, 0.0).astype(BF16)
            gbias_ref[g] = jnp.sum(jnp.where(pick, bsp_ref[g:g + 1, :], 0.0), axis=1, keepdims=True)

    zs = [jnp.dot(gate_ref[g], vb[:, g * gd:(g + 1) * gd], preferred_element_type=F32) + gbias_ref[g]
          for g in range(groups)]
    z = jnp.concatenate(zs, axis=1)
    y_a = (col(o_u, aw) * z).astype(BF16)

    p = col(o_cg, bw) * col(o_xin, bw)
    trow = lax.broadcasted_iota(jnp.int32, (tm, bw), 0)
    if prompt:
        fresh = pl.program_id(0) % tiles_per_seq == 0
        pp = cgp_ref[...].astype(F32) * xinp_ref[...].astype(F32)
        pp = jnp.where(fresh, 0.0, pp)
        last1 = pp[-1:, :]
        last2 = pp[-2:-1, :]
        e1 = jnp.where(trow == 0, last1, 0.0)
        e2 = jnp.where(trow == 0, last2, jnp.where(trow == 1, last1, 0.0))
        tail_ref[...] = p[tm - 8:, :]
    else:
        trow = trow & (gate_len - 1)
        e1 = e1_ref[...]
        e2 = e2_ref[...]
        p_ref[...] = p
    s1 = jnp.where(trow >= 1, pltpu.roll(p, 1, 0), 0.0) + e1
    s2 = jnp.where(trow >= 2, pltpu.roll(p, 2, 0), 0.0) + e2
    conv = cw_ref[0:1, :] * s2 + cw_ref[1:2, :] * s1 + cw_ref[2:3, :] * p
    y_b = (col(o_bg, bw) * conv).astype(BF16)

    merged = (jax.nn.sigmoid(col(o_ga, dm)) * jnp.dot(y_a, wa_ref[...], preferred_element_type=F32)
              + jax.nn.sigmoid(col(o_gb, dm)) * jnp.dot(y_b, wb_ref[...], preferred_element_type=F32))
    h_ref[...] = x_ref[...] + jnp.dot(merged.astype(BF16), wmix_ref[...], preferred_element_type=F32)


def mix(x, proj, prev, ln_g, ln_b, w_sp, b_sp, conv_w, wa, wb, wmix, *, tm, gate_len, seq_len, name):
    t, dm = x.shape
    aw, bw = wa.shape[0], wb.shape[0]
    groups = w_sp.shape[0]
    prompt = prev is None
    nt = t // tm
    row_spec = lambda w: pl.BlockSpec((tm, w), lambda i: (i, 0))
    in_specs = [row_spec(dm), row_spec(proj.shape[1])]
    if prompt:
        prev_rows = 16
        cg_blk = (2 * aw + bw) // bw
        in_specs += [
            pl.BlockSpec((prev_rows, bw), lambda i: (jnp.maximum(i * (tm // prev_rows) - 1, 0), cg_blk)),
            pl.BlockSpec((prev_rows, bw), lambda i: (jnp.maximum(i * (tm // prev_rows) - 1, 0), cg_blk + 1)),
        ]
        extra = (proj, proj)
    else:
        in_specs += [row_spec(bw), row_spec(bw)]
        extra = prev
    in_specs += [_resident((1, aw)), _resident((1, aw)), _resident(w_sp.shape), _resident(b_sp.shape),
                 _resident(conv_w.shape), _resident(wa.shape), _resident(wb.shape), _resident(wmix.shape)]
    out_specs = [row_spec(dm)]
    out_shape = [jax.ShapeDtypeStruct((t, dm), F32)]
    if prompt:
        out_specs.append(pl.BlockSpec((8, bw), lambda i: (i, 0)))
        out_shape.append(jax.ShapeDtypeStruct((nt * 8, bw), F32))
    else:
        out_specs += [row_spec(bw), row_spec(aw)]
        out_shape += [jax.ShapeDtypeStruct((t, bw), F32), jax.ShapeDtypeStruct((t, aw), F32)]
    kern = functools.partial(_mix_kernel, tm=tm, gate_len=gate_len, widths=(aw, bw, dm), groups=groups,
                             prompt=prompt, tiles_per_seq=max(seq_len // tm, 1))
    return pl.pallas_call(
        kern, grid=(nt,), in_specs=in_specs, out_specs=out_specs, out_shape=out_shape,
        scratch_shapes=[pltpu.VMEM((groups, tm, tm), BF16), pltpu.VMEM((groups, tm, 1), F32)],
        compiler_params=_params("arbitrary"), name=name,
    )(x, proj, *extra, ln_g.reshape(1, aw), ln_b.reshape(1, aw), w_sp, b_sp, conv_w, wa, wb, wmix)


def _attn_kernel(q_ref, k_ref, v_ref, o_ref, *, n_seq, rows, heads, head_dim):
    scale = head_dim ** -0.5
    for b in range(n_seq):
        r0 = b * rows
        for h in range(heads):
            c0 = h * head_dim
            q = q_ref[r0:r0 + rows, c0:c0 + head_dim].astype(BF16)
            k = k_ref[b, :, c0:c0 + head_dim].astype(BF16)
            s = lax.dot_general(q, k, (((1,), (1,)), ((), ())), preferred_element_type=F32) * scale
            e = jnp.exp(s - jnp.max(s, axis=-1, keepdims=True))
            p = (e / jnp.sum(e, axis=-1, keepdims=True)).astype(BF16)
            v = v_ref[b, :, c0:c0 + head_dim].astype(BF16)
            o = jnp.dot(p, v, preferred_element_type=F32)
            o_ref[r0:r0 + rows, c0:c0 + head_dim] = o.astype(o_ref.dtype)


def attn(q, k, v, *, heads, rows, n_seq, seq_len, name):
    t, dm = q.shape
    m = k.shape[1]
    blk = n_seq * rows
    per_seq = seq_len // rows
    kv_spec = pl.BlockSpec((n_seq, m, dm), lambda i: (i // per_seq, 0, 0))
    kern = functools.partial(_attn_kernel, n_seq=n_seq, rows=rows, heads=heads, head_dim=dm // heads)
    return pl.pallas_call(
        kern, grid=(t // blk,),
        in_specs=[pl.BlockSpec((blk, dm), lambda i: (i, 0)), kv_spec, kv_spec],
        out_specs=pl.BlockSpec((blk, dm), lambda i: (i, 0)),
        out_shape=jax.ShapeDtypeStruct((t, dm), q.dtype),
        compiler_params=_params("parallel"), name=name,
    )(q, k, v)


LANES = 128


def _attn_rows_kernel(q_ref, k_ref, v_ref, o_ref, *, n_seq, rows, heads, head_dim, n_mem):
    scale = head_dim ** -0.5
    slabs = head_dim // LANES
    per_seq = n_mem * slabs * heads
    width = n_mem * slabs
    pairs = [(b, h) for b in range(n_seq) for h in range(heads)]
    lane_slab = lax.broadcasted_iota(jnp.int32, (rows, width), 1) & (slabs - 1)
    is_slab = [lane_slab == c for c in range(slabs)]

    parts = []
    for b, h in pairs:
        r0, c0 = b * rows, h * head_dim
        qh = jnp.concatenate(
            [q_ref[r0:r0 + rows, c0 + c * LANES:c0 + (c + 1) * LANES] for c in range(slabs)], axis=0
        ).astype(BF16)
        kh = k_ref[pl.ds(b * per_seq + h, width, stride=heads), :].astype(BF16)
        g = lax.dot_general(qh, kh, (((1,), (1,)), ((), ())), preferred_element_type=F32)
        s = g[0:rows]
        for c in range(1, slabs):
            s = jnp.where(is_slab[c], g[c * rows:(c + 1) * rows], s)
        parts.append(s)
    s = jnp.concatenate(parts, axis=0)

    lane = lax.broadcasted_iota(jnp.int32, s.shape, 1)
    step = 1
    while step < slabs:
        s = s + jnp.where((lane & step) != 0, pltpu.roll(s, step, 1), pltpu.roll(s, width - step, 1))
        step *= 2
    s = s * scale
    e = jnp.exp(s - jnp.max(s, axis=-1, keepdims=True))
    p = e / (jnp.sum(e, axis=-1, keepdims=True) * (1.0 / slabs))

    for i, (b, h) in enumerate(pairs):
        r0, c0 = b * rows, h * head_dim
        ph = p[i * rows:(i + 1) * rows]
        w = jnp.concatenate([jnp.where(is_slab[c], ph, 0.0) for c in range(slabs)], axis=0).astype(BF16)
        vh = v_ref[pl.ds(b * per_seq + h, width, stride=heads), :].astype(BF16)
        o = jnp.dot(w, vh, preferred_element_type=F32)
        for c in range(slabs):
            o_ref[r0:r0 + rows, c0 + c * LANES:c0 + (c + 1) * LANES] = o[c * rows:(c + 1) * rows]


def _rows_view(a):
    n, m, heads, hd = a.shape
    a = a.reshape(n * m, heads, hd // LANES, LANES)
    return a.transpose(0, 2, 1, 3).reshape(n * m * hd // LANES * heads, LANES)


def attn_rows(q, k, v, *, n_seq, name):
    n, n_mem, heads, hd = k.shape
    t, dm = q.shape
    rows = t // n
    slabs = hd // LANES
    assert slabs & (slabs - 1) == 0 and rows % 8 == 0
    per_seq = n_mem * slabs * heads
    kv_spec = pl.BlockSpec((n_seq * per_seq, LANES), lambda i: (i, 0))
    q_spec = pl.BlockSpec((n_seq * rows, dm), lambda i: (i, 0))
    kern = functools.partial(_attn_rows_kernel, n_seq=n_seq, rows=rows, heads=heads, head_dim=hd, n_mem=n_mem)
    return pl.pallas_call(
        kern, grid=(n // n_seq,),
        in_specs=[q_spec, kv_spec, kv_spec], out_specs=q_spec,
        out_shape=jax.ShapeDtypeStruct((t, dm), F32),
        compiler_params=_params("parallel"), name=name,
    )(q, _rows_view(k), _rows_view(v))


def _matmul_res_kernel(h_ref, o_ref, w_ref, out_ref, wb_ref=None):
    w = _bf16_tile(w_ref, wb_ref)
    out_ref[...] = h_ref[...] + jnp.dot(o_ref[...].astype(BF16), w, preferred_element_type=F32)


def matmul_res(h, o, w, *, tm, tn, name, emit_w=False):
    t, dm = h.shape
    kdim = o.shape[1]
    one_tile = t == tm
    assert one_tile or not emit_w
    o_spec = (pl.BlockSpec((tm, kdim), lambda i, j: (i, 0), pipeline_mode=pl.Buffered(1)) if one_tile
              else pl.BlockSpec((tm, kdim), lambda i, j: (i, 0)))
    w_spec = pl.BlockSpec((kdim, tn), lambda i, j: (0, j))
    out_specs = [pl.BlockSpec((tm, tn), lambda i, j: (i, j))]
    out_shape = [jax.ShapeDtypeStruct((t, dm), F32)]
    if emit_w:
        out_specs.append(w_spec)
        out_shape.append(jax.ShapeDtypeStruct(w.shape, BF16))
    outs = pl.pallas_call(
        _matmul_res_kernel, grid=(t // tm, dm // tn),
        in_specs=[pl.BlockSpec((tm, tn), lambda i, j: (i, j)), o_spec, w_spec],
        out_specs=out_specs, out_shape=out_shape,
        compiler_params=_params("parallel", "arbitrary"), name=name,
    )(h, o, w)
    return outs if emit_w else outs[0]


def _mlp_kernel(h_ref, g_ref, wup_ref, wdn_ref, gf_ref, y_ref, *rest, emit_w):
    wub_ref, wdb_ref, hn_ref = rest if emit_w else (None, None, rest[0])
    j = pl.program_id(1)

    @pl.when(j == 0)
    def _():
        h = h_ref[...]
        hn_ref[...] = _rms(h, g_ref[...]).astype(BF16)
        y_ref[...] = h

    a = jnp.dot(hn_ref[...], _bf16_tile(wup_ref, wub_ref), preferred_element_type=F32)
    a = jnp.square(jnp.maximum(a, 0.0)).astype(BF16)
    y_ref[...] += jnp.dot(a, _bf16_tile(wdn_ref, wdb_ref), preferred_element_type=F32)

    @pl.when(j == pl.num_programs(1) - 1)
    def _():
        y_ref[...] = _rms(y_ref[...], gf_ref[...])


def mlp(h, g, w_up, w_down, g_final, *, tm, fc, name, emit_w=False):
    t, dm = h.shape
    dff = w_up.shape[1]
    one_tile = t == tm
    assert one_tile or not emit_w
    once = dict(pipeline_mode=pl.Buffered(1)) if one_tile else {}
    up_spec = pl.BlockSpec((dm, fc), lambda i, j: (0, j))
    dn_spec = pl.BlockSpec((fc, dm), lambda i, j: (j, 0))
    out_specs = [pl.BlockSpec((tm, dm), lambda i, j: (i, 0), **once)]
    out_shape = [jax.ShapeDtypeStruct((t, dm), F32)]
    if emit_w:
        out_specs += [up_spec, dn_spec]
        out_shape += [jax.ShapeDtypeStruct(w_up.shape, BF16), jax.ShapeDtypeStruct(w_down.shape, BF16)]
    outs = pl.pallas_call(
        functools.partial(_mlp_kernel, emit_w=emit_w), grid=(t // tm, dff // fc),
        in_specs=[pl.BlockSpec((tm, dm), lambda i, j: (i, 0), **once),
                  pl.BlockSpec((1, dm), lambda i, j: (0, 0)),
                  up_spec, dn_spec,
                  pl.BlockSpec((1, dm), lambda i, j: (0, 0))],
        out_specs=out_specs, out_shape=out_shape,
        scratch_shapes=[pltpu.VMEM((tm, dm), BF16)],
        compiler_params=_params("parallel", "arbitrary"), name=name,
    )(h, g.reshape(1, dm), w_up, w_down, g_final.reshape(1, dm))
    return outs if emit_w else outs[0]


MIX_TM = 256


def _layer(x, conv_prev, k_mem, v_mem, seq_len, tag, wts, g_final):
    (norm_mix_g, w_in, ln_v_g, ln_v_b, w_spatial, b_spatial, conv_w, w_a, w_b, w_mix,
     norm_x_g, w_q, w_xo, norm_mlp_g, w_up, w_down) = wts
    t, dm = x.shape
    heads = k_mem.shape[2]
    prompt = conv_prev is None
    gate_len = CHUNK if seq_len % CHUNK == 0 else seq_len
    assert gate_len & (gate_len - 1) == 0 and MIX_TM % gate_len == 0 and gate_len <= w_spatial.shape[1]
    tm = min(1024, t)
    emit = w_in.dtype != BF16
    first = (lambda r: r[0]) if emit else (lambda r: r)
    cast = {}

    r = rms_matmul(x, norm_mix_g, w_in, BF16, tm=tm, tn=1024, name=f"in_proj_{tag}", emit_w=emit)
    proj = first(r)
    if prompt:
        prev = None
    else:
        e1 = jnp.pad(conv_prev[:, 1:2], ((0, 0), (0, seq_len - 1), (0, 0))).reshape(t, -1)
        e2 = jnp.pad(conv_prev, ((0, 0), (0, seq_len - 2), (0, 0))).reshape(t, -1)
        prev = (e1, e2)
    outs = mix(x, proj, prev, ln_v_g, ln_v_b, w_spatial, b_spatial, conv_w, w_a, w_b, w_mix,
               tm=MIX_TM, gate_len=gate_len, seq_len=seq_len, name=f"mix_{tag}")
    h = outs[0]

    if prompt:
        n_seq_total = t // seq_len
        k2 = k_mem.reshape(n_seq_total, k_mem.shape[1], dm)
        v2 = v_mem.reshape(n_seq_total, v_mem.shape[1], dm)
        q = rms_matmul(h, norm_x_g, w_q, BF16, tm=tm, tn=1024, name=f"q_proj_{tag}")
        o = attn(q, k2, v2, heads=heads, rows=512, n_seq=1, seq_len=seq_len, name=f"attn_{tag}")
    else:
        rq = rms_matmul(h, norm_x_g, w_q, F32, tm=tm, tn=1024, name=f"q_proj_{tag}", emit_w=emit)
        o = attn_rows(first(rq), k_mem, v_mem, n_seq=4, name=f"attn_{tag}")
    rx = matmul_res(h, o, w_xo, tm=tm, tn=1024, name=f"x_out_{tag}", emit_w=emit)
    rm = mlp(first(rx), norm_mlp_g, w_up, w_down, g_final, tm=tm, fc=512, name=f"mlp_{tag}", emit_w=emit)
    if emit:
        cast = dict(w_in=r[1], w_q=rq[1], w_xo=rx[1], w_up=rm[1], w_down=rm[2])
    return first(rm), outs[1:], cast


def kernel(x_prompt, x_sample, state_conv, cache_mem_k, cache_mem_v, mem_prompt, norm_mix_g, w_in, ln_v_g, ln_v_b, w_spatial, b_spatial, conv_w, w_branch_a, w_branch_b, w_mix_out, norm_x_g, norm_mem_g, w_q, w_k, w_v, w_x_out, norm_mlp_g, w_up, w_down, norm_final_g):
    depth = w_in.shape[0]
    assert depth == 1, "the final rmsnorm is fused into the single layer's MLP kernel"
    nb, seq, dm = x_prompt.shape
    nd, dseq, _ = x_sample.shape
    n_mem, heads, hd = cache_mem_k.shape[2:]
    l = 0
    bf = lambda w: w[l].astype(BF16)

    def layer_weights(w_in_, w_q_, w_xo_, w_up_, w_down_):
        return (norm_mix_g[l], w_in_, ln_v_g[l], ln_v_b[l], w_spatial[l], b_spatial[l], conv_w[l],
                bf(w_branch_a), bf(w_branch_b), bf(w_mix_out), norm_x_g[l], w_q_, w_xo_,
                norm_mlp_g[l], w_up_, w_down_)

    y_s, (p_s, vn_s), wb = _layer(x_sample.reshape(nd * dseq, dm), state_conv[l], cache_mem_k[l], cache_mem_v[l],
                                  dseq, "sample", layer_weights(w_in[l], w_q[l], w_x_out[l], w_up[l], w_down[l]),
                                  norm_final_g)

    mem = mem_prompt.reshape(nb * n_mem, dm)
    k_p = rms_matmul(mem, norm_mem_g[l], w_k[l], F32, tm=nb * n_mem, tn=1024, name="mem_k")
    v_p = rms_matmul(mem, norm_mem_g[l], w_v[l], F32, tm=nb * n_mem, tn=1024, name="mem_v")
    k_p = k_p.reshape(nb, n_mem, heads, hd)
    v_p = v_p.reshape(nb, n_mem, heads, hd)

    y_p, (tail_p,), _ = _layer(x_prompt.reshape(nb * seq, dm), None, k_p, v_p, seq, "prompt",
                               layer_weights(wb["w_in"], wb["w_q"], wb["w_xo"], wb["w_up"], wb["w_down"]),
                               norm_final_g)

    bw = p_s.shape[1]
    keep = conv_w.shape[1] - 1
    conv_p = tail_p.reshape(nb, seq // MIX_TM, 8, bw)[:, -1, 8 - keep:, :]
    conv_s = p_s.reshape(nd, dseq, bw)[:, dseq - keep:, :]
    return (y_p.reshape(nb, seq, dm), y_s.reshape(nd, dseq, dm), k_p[None], v_p[None],
            conv_p[None], conv_s[None], vn_s.reshape(1, nd, dseq, -1))
```

```python
import functools

import jax
import jax.numpy as jnp
from jax import lax
from jax.experimental import pallas as pl
from jax.experimental.pallas import tpu as pltpu

EPS = 1e-6
CHUNK = 128
F32 = jnp.float32
BF16 = jnp.bfloat16

V7X_VMEM_BYTES = 64 * 1024 * 1024
VMEM_LIMIT = V7X_VMEM_BYTES * 7 // 8


def _params(*sem):
    return pltpu.CompilerParams(dimension_semantics=sem, vmem_limit_bytes=VMEM_LIMIT)


def _rms(x, g):
    r = lax.rsqrt(jnp.mean(x * x, axis=-1, keepdims=True) + EPS)
    return (x * r) * g


def _resident(shape):
    return pl.BlockSpec(shape, lambda *_: (0,) * len(shape), pipeline_mode=pl.Buffered(1))


def _bf16_tile(w_ref, wb_ref):
    if wb_ref is None:
        return w_ref[...].astype(BF16)
    wb_ref[...] = w_ref[...].astype(BF16)
    return wb_ref[...]


def _rms_matmul_kernel(x_ref, g_ref, w_ref, *rest, emit_w, n_side):
    side_in, rest = rest[:n_side], rest[n_side:]
    o_ref, rest = rest[0], rest[1:]
    wb_ref, rest = (rest[0], rest[1:]) if emit_w else (None, rest)
    side_out, xn_ref = rest[:n_side], rest[n_side]

    @pl.when(pl.program_id(1) == 0)
    def _():
        xn_ref[...] = _rms(x_ref[...], g_ref[...]).astype(BF16)

    w = _bf16_tile(w_ref, wb_ref)
    o_ref[...] = jnp.dot(xn_ref[...], w, preferred_element_type=F32).astype(o_ref.dtype)
    for src, dst in zip(side_in, side_out):
        dst[...] = src[...].astype(BF16)


BF16_SUBLANES = 16


def _side_chunks(rows, steps):
    return max(n for n in range(1, steps + 1) if rows % n == 0 and (rows // n) % BF16_SUBLANES == 0)


def rms_matmul(x, g, w, out_dtype, *, tm, tn, name, emit_w=False, side_cast=()):
    t, d = x.shape
    n = w.shape[1]
    one_tile = t == tm
    assert one_tile or not emit_w
    nj = n // tn
    steps = (t // tm) * nj
    x_spec = (pl.BlockSpec((tm, d), lambda i, j: (i, 0), pipeline_mode=pl.Buffered(1)) if one_tile
              else pl.BlockSpec((tm, d), lambda i, j: (i, 0)))
    w_spec = pl.BlockSpec((d, tn), lambda i, j: (0, j))
    out_specs = [pl.BlockSpec((tm, tn), lambda i, j: (i, j))]
    out_shape = [jax.ShapeDtypeStruct((t, n), out_dtype)]
    if emit_w:
        out_specs.append(w_spec)
        out_shape.append(jax.ShapeDtypeStruct(w.shape, BF16))
    side_specs = []
    for s in side_cast:
        chunks = _side_chunks(s.shape[0], steps)
        side_specs.append(pl.BlockSpec((s.shape[0] // chunks, s.shape[1]),
                                       lambda i, j, c=chunks: (jnp.minimum(i * nj + j, c - 1), 0)))
        out_shape.append(jax.ShapeDtypeStruct(s.shape, BF16))
    outs = pl.pallas_call(
        functools.partial(_rms_matmul_kernel, emit_w=emit_w, n_side=len(side_cast)),
        grid=(t // tm, nj),
        in_specs=[x_spec, pl.BlockSpec((1, d), lambda i, j: (0, 0)), w_spec] + side_specs,
        out_specs=out_specs + side_specs, out_shape=out_shape,
        scratch_shapes=[pltpu.VMEM((tm, d), BF16)],
        compiler_params=_params("arbitrary", "arbitrary") if side_cast else _params("parallel", "arbitrary"),
        name=name,
    )(x, g.reshape(1, d), w, *side_cast)
    return outs if len(outs) > 1 else outs[0]


def _mix_kernel(*refs, tm, gate_len, widths, groups, prompt, tiles_per_seq):
    if prompt:
        (x_ref, proj_ref, cgp_ref, xinp_ref, lng_ref, lnb_ref, wsp_ref, bsp_ref, cw_ref,
         wa_ref, wb_ref, wmix_ref, h_ref, tail_ref, gate_ref, gbias_ref) = refs
    else:
        (x_ref, proj_ref, e1_ref, e2_ref, lng_ref, lnb_ref, wsp_ref, bsp_ref, cw_ref,
         wa_ref, wb_ref, wmix_ref, h_ref, p_ref, v_ref, gate_ref, gbias_ref) = refs
    aw, bw, dm = widths
    o_u, o_v, o_bg, o_cg, o_xin, o_ga, o_gb = (0, aw, 2 * aw, 2 * aw + bw, 2 * aw + 2 * bw,
                                               2 * aw + 3 * bw, 2 * aw + 3 * bw + dm)

    def col(o, w):
        return proj_ref[:, o:o + w].astype(F32)

    v = col(o_v, aw)
    mu = jnp.mean(v, axis=-1, keepdims=True)
    vc = v - mu
    var = jnp.mean(vc * vc, axis=-1, keepdims=True)
    vn = (vc * lax.rsqrt(var + EPS)) * lng_ref[...] + lnb_ref[...]
    if not prompt:
        v_ref[...] = vn
    vb = vn.astype(BF16)
    gd = aw // groups

    @pl.when(pl.program_id(0) == 0)
    def _():
        row = lax.broadcasted_iota(jnp.int32, (tm, tm), 0)
        cidx = lax.broadcasted_iota(jnp.int32, (tm, tm), 1)
        keep = ((row ^ cidx) < gate_len) & (cidx <= row)
        cs = wsp_ref.shape[1]
        pick = (lax.broadcasted_iota(jnp.int32, (tm, cs), 1)
                == (lax.broadcasted_iota(jnp.int32, (tm, cs), 0) & (gate_len - 1)))
        pick_b = jnp.where(pick, 1.0, 0.0).astype(BF16)
        for g in range(groups):
            w_rows = jnp.dot(pick_b, wsp_ref[g].astype(BF16), preferred_element_type=F32).astype(BF16)
            w_full = lax.dot_general(w_rows, pick_b, (((1,), (1,)), ((), ())), preferred_element_type=F32)
            gate_ref[g] = jnp.where(keep, w_full, 0.0).astype(BF16)
            gbias_ref[g] = jnp.sum(jnp.where(pick, bsp_ref[g:g + 1, :], 0.0), axis=1, keepdims=True)

    zs = [jnp.dot(gate_ref[g], vb[:, g * gd:(g + 1) * gd], preferred_element_type=F32) + gbias_ref[g]
          for g in range(groups)]
    z = jnp.concatenate(zs, axis=1)
    y_a = (col(o_u, aw) * z).astype(BF16)

    p = col(o_cg, bw) * col(o_xin, bw)
    trow = lax.broadcasted_iota(jnp.int32, (tm, bw), 0)
    if prompt:
        fresh = pl.program_id(0) % tiles_per_seq == 0
        pp = cgp_ref[...].astype(F32) * xinp_ref[...].astype(F32)
        pp = jnp.where(fresh, 0.0, pp)
        last1 = pp[-1:, :]
        last2 = pp[-2:-1, :]
        e1 = jnp.where(trow == 0, last1, 0.0)
        e2 = jnp.where(trow == 0, last2, jnp.where(trow == 1, last1, 0.0))
        tail_ref[...] = p[tm - 8:, :]
    else:
        trow = trow & (gate_len - 1)
        e1 = e1_ref[...]
        e2 = e2_ref[...]
        p_ref[...] = p
    s1 = jnp.where(trow >= 1, pltpu.roll(p, 1, 0), 0.0) + e1
    s2 = jnp.where(trow >= 2, pltpu.roll(p, 2, 0), 0.0) + e2
    conv = cw_ref[0:1, :] * s2 + cw_ref[1:2, :] * s1 + cw_ref[2:3, :] * p
    y_b = (col(o_bg, bw) * conv).astype(BF16)

    merged = (jax.nn.sigmoid(col(o_ga, dm)) * jnp.dot(y_a, wa_ref[...], preferred_element_type=F32)
              + jax.nn.sigmoid(col(o_gb, dm)) * jnp.dot(y_b, wb_ref[...], preferred_element_type=F32))
    h_ref[...] = x_ref[...] + jnp.dot(merged.astype(BF16), wmix_ref[...], preferred_element_type=F32)


def mix(x, proj, prev, ln_g, ln_b, w_sp, b_sp, conv_w, wa, wb, wmix, *, tm, gate_len, seq_len, name):
    t, dm = x.shape
    aw, bw = wa.shape[0], wb.shape[0]
    groups = w_sp.shape[0]
    prompt = prev is None
    nt = t // tm
    row_spec = lambda w: pl.BlockSpec((tm, w), lambda i: (i, 0))
    in_specs = [row_spec(dm), row_spec(proj.shape[1])]
    if prompt:
        prev_rows = 16
        cg_blk = (2 * aw + bw) // bw
        in_specs += [
            pl.BlockSpec((prev_rows, bw), lambda i: (jnp.maximum(i * (tm // prev_rows) - 1, 0), cg_blk)),
            pl.BlockSpec((prev_rows, bw), lambda i: (jnp.maximum(i * (tm // prev_rows) - 1, 0), cg_blk + 1)),
        ]
        extra = (proj, proj)
    else:
        in_specs += [row_spec(bw), row_spec(bw)]
        extra = prev
    in_specs += [_resident((1, aw)), _resident((1, aw)), _resident(w_sp.shape), _resident(b_sp.shape),
                 _resident(conv_w.shape), _resident(wa.shape), _resident(wb.shape), _resident(wmix.shape)]
    out_specs = [row_spec(dm)]
    out_shape = [jax.ShapeDtypeStruct((t, dm), F32)]
    if prompt:
        out_specs.append(pl.BlockSpec((8, bw), lambda i: (i, 0)))
        out_shape.append(jax.ShapeDtypeStruct((nt * 8, bw), F32))
    else:
        out_specs += [row_spec(bw), row_spec(aw)]
        out_shape += [jax.ShapeDtypeStruct((t, bw), F32), jax.ShapeDtypeStruct((t, aw), F32)]
    kern = functools.partial(_mix_kernel, tm=tm, gate_len=gate_len, widths=(aw, bw, dm), groups=groups,
                             prompt=prompt, tiles_per_seq=max(seq_len // tm, 1))
    return pl.pallas_call(
        kern, grid=(nt,), in_specs=in_specs, out_specs=out_specs, out_shape=out_shape,
        scratch_shapes=[pltpu.VMEM((groups, tm, tm), BF16), pltpu.VMEM((groups, tm, 1), F32)],
        compiler_params=_params("arbitrary"), name=name,
    )(x, proj, *extra, ln_g.reshape(1, aw), ln_b.reshape(1, aw), w_sp, b_sp, conv_w, wa, wb, wmix)


def _attn_block_kernel(h_ref, g_ref, wq_ref, k_ref, v_ref, wxo_ref, gn_ref, out_ref, hn_ref, *, heads, head_dim):
    scale = head_dim ** -0.5
    h = h_ref[...]
    q = jnp.dot(_rms(h, g_ref[...]).astype(BF16), wq_ref[...], preferred_element_type=F32).astype(BF16)
    outs = []
    for hd in range(heads):
        cols = slice(hd * head_dim, (hd + 1) * head_dim)
        k = k_ref[0, :, cols].astype(BF16)
        s = lax.dot_general(q[:, cols], k, (((1,), (1,)), ((), ())), preferred_element_type=F32) * scale
        e = jnp.exp(s - jnp.max(s, axis=-1, keepdims=True))
        p = (e / jnp.sum(e, axis=-1, keepdims=True)).astype(BF16)
        outs.append(jnp.dot(p, v_ref[0, :, cols].astype(BF16), preferred_element_type=F32).astype(BF16))
    o = jnp.concatenate(outs, axis=1)
    _residual_and_norm(h, o, wxo_ref, gn_ref, out_ref, hn_ref)


def _residual_and_norm(h, o, wxo_ref, gn_ref, out_ref, hn_ref):
    out = h + jnp.dot(o, wxo_ref[...], preferred_element_type=F32)
    out_ref[...] = out
    hn_ref[...] = _rms(out, gn_ref[...]).astype(BF16)


def attn_block(h, g, wq, k, v, wxo, g_next, *, heads, tm, seq_len, name):
    t, dm = h.shape
    m = k.shape[1]
    per_seq = seq_len // tm
    row_spec = pl.BlockSpec((tm, dm), lambda i: (i, 0))
    kv_spec = pl.BlockSpec((1, m, dm), lambda i: (i // per_seq, 0, 0))
    kern = functools.partial(_attn_block_kernel, heads=heads, head_dim=dm // heads)
    return pl.pallas_call(
        kern, grid=(t // tm,),
        in_specs=[row_spec, _resident((1, dm)), _resident(wq.shape), kv_spec, kv_spec, _resident(wxo.shape),
                  _resident((1, dm))],
        out_specs=[row_spec, row_spec],
        out_shape=[jax.ShapeDtypeStruct((t, dm), F32), jax.ShapeDtypeStruct((t, dm), BF16)],
        compiler_params=_params("parallel"), name=name,
    )(h, g.reshape(1, dm), wq, k, v, wxo, g_next.reshape(1, dm))


LANES = 128


def _attn_rows_kernel(q_ref, k_ref, v_ref, o_ref, *, n_seq, rows, heads, head_dim, n_mem):
    scale = head_dim ** -0.5
    slabs = head_dim // LANES
    per_seq = n_mem * slabs * heads
    width = n_mem * slabs
    pairs = [(b, h) for b in range(n_seq) for h in range(heads)]
    lane_slab = lax.broadcasted_iota(jnp.int32, (rows, width), 1) & (slabs - 1)
    is_slab = [lane_slab == c for c in range(slabs)]

    parts = []
    for b, h in pairs:
        r0, c0 = b * rows, h * head_dim
        qh = jnp.concatenate(
            [q_ref[r0:r0 + rows, c0 + c * LANES:c0 + (c + 1) * LANES] for c in range(slabs)], axis=0
        ).astype(BF16)
        kh = k_ref[pl.ds(b * per_seq + h, width, stride=heads), :].astype(BF16)
        g = lax.dot_general(qh, kh, (((1,), (1,)), ((), ())), preferred_element_type=F32)
        s = g[0:rows]
        for c in range(1, slabs):
            s = jnp.where(is_slab[c], g[c * rows:(c + 1) * rows], s)
        parts.append(s)
    s = jnp.concatenate(parts, axis=0)

    lane = lax.broadcasted_iota(jnp.int32, s.shape, 1)
    step = 1
    while step < slabs:
        s = s + jnp.where((lane & step) != 0, pltpu.roll(s, step, 1), pltpu.roll(s, width - step, 1))
        step *= 2
    s = s * scale
    e = jnp.exp(s - jnp.max(s, axis=-1, keepdims=True))
    p = e / (jnp.sum(e, axis=-1, keepdims=True) * (1.0 / slabs))

    for i, (b, h) in enumerate(pairs):
        r0, c0 = b * rows, h * head_dim
        ph = p[i * rows:(i + 1) * rows]
        w = jnp.concatenate([jnp.where(is_slab[c], ph, 0.0) for c in range(slabs)], axis=0).astype(BF16)
        vh = v_ref[pl.ds(b * per_seq + h, width, stride=heads), :].astype(BF16)
        o = jnp.dot(w, vh, preferred_element_type=F32)
        for c in range(slabs):
            o_ref[r0:r0 + rows, c0 + c * LANES:c0 + (c + 1) * LANES] = o[c * rows:(c + 1) * rows]


def _rows_view(a):
    n, m, heads, hd = a.shape
    a = a.reshape(n * m, heads, hd // LANES, LANES)
    return a.transpose(0, 2, 1, 3).reshape(n * m * hd // LANES * heads, LANES)


def attn_rows(q, k, v, *, n_seq, name):
    n, n_mem, heads, hd = k.shape
    t, dm = q.shape
    rows = t // n
    slabs = hd // LANES
    assert slabs & (slabs - 1) == 0 and rows % 8 == 0
    per_seq = n_mem * slabs * heads
    kv_spec = pl.BlockSpec((n_seq * per_seq, LANES), lambda i: (i, 0))
    q_spec = pl.BlockSpec((n_seq * rows, dm), lambda i: (i, 0))
    kern = functools.partial(_attn_rows_kernel, n_seq=n_seq, rows=rows, heads=heads, head_dim=hd, n_mem=n_mem)
    return pl.pallas_call(
        kern, grid=(n // n_seq,),
        in_specs=[q_spec, kv_spec, kv_spec], out_specs=q_spec,
        out_shape=jax.ShapeDtypeStruct((t, dm), F32),
        compiler_params=_params("parallel"), name=name,
    )(q, _rows_view(k), _rows_view(v))


def _attn_out_kernel(h_ref, o_ref, wxo_ref, gn_ref, out_ref, hn_ref):
    _residual_and_norm(h_ref[...], o_ref[...].astype(BF16), wxo_ref, gn_ref, out_ref, hn_ref)


def attn_out(h, o, wxo, g_next, *, tm, name):
    t, dm = h.shape
    row_spec = pl.BlockSpec((tm, dm), lambda i: (i, 0))
    return pl.pallas_call(
        _attn_out_kernel, grid=(t // tm,),
        in_specs=[row_spec, row_spec, _resident(wxo.shape), _resident((1, dm))],
        out_specs=[row_spec, row_spec],
        out_shape=[jax.ShapeDtypeStruct((t, dm), F32), jax.ShapeDtypeStruct((t, dm), BF16)],
        compiler_params=_params("parallel"), name=name,
    )(h, o, wxo, g_next.reshape(1, dm))


def _mlp_kernel(hn_ref, h_hbm, wup_ref, wdn_ref, gf_ref, y_ref, sem, *, tm):
    i, j = pl.program_id(0), pl.program_id(1)
    residual_copy = pltpu.make_async_copy(h_hbm.at[pl.ds(i * tm, tm), :], y_ref, sem.at[0])

    @pl.when(j == 0)
    def _():
        residual_copy.start()

    a = jnp.dot(hn_ref[...], wup_ref[...], preferred_element_type=F32)
    a = jnp.square(jnp.maximum(a, 0.0)).astype(BF16)

    @pl.when(j == 0)
    def _():
        residual_copy.wait()

    y_ref[...] += jnp.dot(a, wdn_ref[...], preferred_element_type=F32)

    @pl.when(j == pl.num_programs(1) - 1)
    def _():
        y_ref[...] = _rms(y_ref[...], gf_ref[...])


def mlp(hn, h, w_up, w_down, g_final, *, tm, fc, name):
    t, dm = h.shape
    dff = w_up.shape[1]
    return pl.pallas_call(
        functools.partial(_mlp_kernel, tm=tm), grid=(t // tm, dff // fc),
        in_specs=[pl.BlockSpec((tm, dm), lambda i, j: (i, 0)),
                  pl.BlockSpec(memory_space=pl.ANY),
                  pl.BlockSpec((dm, fc), lambda i, j: (0, j)),
                  pl.BlockSpec((fc, dm), lambda i, j: (j, 0)),
                  pl.BlockSpec((1, dm), lambda i, j: (0, 0))],
        out_specs=pl.BlockSpec((tm, dm), lambda i, j: (i, 0)),
        out_shape=jax.ShapeDtypeStruct((t, dm), F32),
        scratch_shapes=[pltpu.SemaphoreType.DMA((1,))],
        compiler_params=_params("arbitrary", "arbitrary"), name=name,
    )(hn, h, w_up, w_down, g_final.reshape(1, dm))


MIX_TM = 256


ROW_TM = 1024
COL_TN = 1024
ATTN_TM = 512
MLP_FC = 512


def _gate_len(seq_len, w_spatial):
    gate_len = CHUNK if seq_len % CHUNK == 0 else seq_len
    assert gate_len & (gate_len - 1) == 0 and MIX_TM % gate_len == 0 and gate_len <= w_spatial.shape[1]
    return gate_len


def kernel(x_prompt, x_sample, state_conv, cache_mem_k, cache_mem_v, mem_prompt, norm_mix_g, w_in, ln_v_g, ln_v_b, w_spatial, b_spatial, conv_w, w_branch_a, w_branch_b, w_mix_out, norm_x_g, norm_mem_g, w_q, w_k, w_v, w_x_out, norm_mlp_g, w_up, w_down, norm_final_g):
    depth = w_in.shape[0]
    assert depth == 1, "the final rmsnorm is fused into the single layer's MLP kernel"
    nb, seq, dm = x_prompt.shape
    nd, dseq, _ = x_sample.shape
    n_mem, heads, hd = cache_mem_k.shape[2:]
    l = 0
    xs = x_sample.reshape(nd * dseq, dm)
    xp = x_prompt.reshape(nb * seq, dm)
    assert xs.shape[0] == ROW_TM

    proj_s, w_in_b = rms_matmul(xs, norm_mix_g[l], w_in[l], BF16, tm=ROW_TM, tn=COL_TN, emit_w=True,
                                name="in_proj_sample")
    proj_p, w_a, w_b, w_mix, w_q_b, w_xo, w_up_b, w_down_b = rms_matmul(
        xp, norm_mix_g[l], w_in_b, BF16, tm=ROW_TM, tn=COL_TN, name="in_proj_prompt",
        side_cast=(w_branch_a[l], w_branch_b[l], w_mix_out[l], w_q[l], w_x_out[l], w_up[l], w_down[l]))
    mix_w = (ln_v_g[l], ln_v_b[l], w_spatial[l], b_spatial[l], conv_w[l], w_a, w_b, w_mix)

    prev = state_conv[l]
    e1 = jnp.pad(prev[:, 1:2], ((0, 0), (0, dseq - 1), (0, 0))).reshape(nd * dseq, -1)
    e2 = jnp.pad(prev, ((0, 0), (0, dseq - 2), (0, 0))).reshape(nd * dseq, -1)
    h_s, p_s, vn_s = mix(xs, proj_s, (e1, e2), *mix_w, tm=MIX_TM, gate_len=_gate_len(dseq, w_spatial[l]),
                         seq_len=dseq, name="mix_sample")
    q_s = rms_matmul(h_s, norm_x_g[l], w_q_b, F32, tm=ROW_TM, tn=COL_TN, name="q_proj_sample")
    o_s = attn_rows(q_s, cache_mem_k[l], cache_mem_v[l], n_seq=4, name="attn_sample")
    h_s, hn_s = attn_out(h_s, o_s, w_xo, norm_mlp_g[l], tm=ATTN_TM, name="x_out_sample")
    y_s = mlp(hn_s, h_s, w_up_b, w_down_b, norm_final_g, tm=ROW_TM, fc=MLP_FC, name="mlp_sample")

    mem = mem_prompt.reshape(nb * n_mem, dm)
    k_p = rms_matmul(mem, norm_mem_g[l], w_k[l], F32, tm=nb * n_mem, tn=COL_TN, name="mem_k")
    v_p = rms_matmul(mem, norm_mem_g[l], w_v[l], F32, tm=nb * n_mem, tn=COL_TN, name="mem_v")
    h_p, tail_p = mix(xp, proj_p, None, *mix_w, tm=MIX_TM, gate_len=_gate_len(seq, w_spatial[l]),
                      seq_len=seq, name="mix_prompt")
    h_p, hn_p = attn_block(h_p, norm_x_g[l], w_q_b, k_p.reshape(nb, n_mem, dm), v_p.reshape(nb, n_mem, dm), w_xo,
                           norm_mlp_g[l], heads=heads, tm=ATTN_TM, seq_len=seq, name="attn_block_prompt")
    y_p = mlp(hn_p, h_p, w_up_b, w_down_b, norm_final_g, tm=ROW_TM, fc=MLP_FC, name="mlp_prompt")
    k_p = k_p.reshape(nb, n_mem, heads, hd)
    v_p = v_p.reshape(nb, n_mem, heads, hd)

    bw = p_s.shape[1]
    keep = conv_w.shape[1] - 1
    conv_p = tail_p.reshape(nb, seq // MIX_TM, 8, bw)[:, -1, 8 - keep:, :]
    conv_s = p_s.reshape(nd, dseq, bw)[:, dseq - keep:, :]
    return (y_p.reshape(nb, seq, dm), y_s.reshape(nd, dseq, dm), k_p[None], v_p[None],
            conv_p[None], conv_s[None], vn_s.reshape(1, nd, dseq, -1))
```

```python
import functools

import jax
import jax.numpy as jnp
from jax import lax
from jax.experimental import pallas as pl
from jax.experimental.pallas import tpu as pltpu

EPS = 1e-6
CHUNK = 128
F32 = jnp.float32
BF16 = jnp.bfloat16

V7X_VMEM_BYTES = 64 * 1024 * 1024
VMEM_LIMIT = V7X_VMEM_BYTES * 7 // 8


def _params(*sem):
    return pltpu.CompilerParams(dimension_semantics=sem, vmem_limit_bytes=VMEM_LIMIT)


def _rms(x, g):
    r = lax.rsqrt(jnp.mean(x * x, axis=-1, keepdims=True) + EPS)
    return (x * r) * g


def _resident(shape):
    return pl.BlockSpec(shape, lambda *_: (0,) * len(shape), pipeline_mode=pl.Buffered(1))


def _bf16_tile(w_ref, wb_ref):
    if wb_ref is None:
        return w_ref[...].astype(BF16)
    wb_ref[...] = w_ref[...].astype(BF16)
    return wb_ref[...]


def _rms_matmul_kernel(x_ref, g_ref, w_ref, *rest, emit_w, n_side):
    side_in, rest = rest[:n_side], rest[n_side:]
    o_ref, rest = rest[0], rest[1:]
    wb_ref, rest = (rest[0], rest[1:]) if emit_w else (None, rest)
    side_out, xn_ref = rest[:n_side], rest[n_side]

    @pl.when(pl.program_id(1) == 0)
    def _():
        xn_ref[...] = _rms(x_ref[...], g_ref[...]).astype(BF16)

    w = _bf16_tile(w_ref, wb_ref)
    o_ref[...] = jnp.dot(xn_ref[...], w, preferred_element_type=F32).astype(o_ref.dtype)
    for src, dst in zip(side_in, side_out):
        dst[...] = src[...].astype(BF16)


BF16_SUBLANES = 16


def _side_chunks(rows, steps):
    return max(n for n in range(1, steps + 1) if rows % n == 0 and (rows // n) % BF16_SUBLANES == 0)


def rms_matmul(x, g, w, out_dtype, *, tm, tn, name, emit_w=False, side_cast=()):
    t, d = x.shape
    n = w.shape[1]
    one_tile = t == tm
    assert one_tile or not emit_w
    nj = n // tn
    steps = (t // tm) * nj
    x_spec = (pl.BlockSpec((tm, d), lambda i, j: (i, 0), pipeline_mode=pl.Buffered(1)) if one_tile
              else pl.BlockSpec((tm, d), lambda i, j: (i, 0)))
    w_spec = pl.BlockSpec((d, tn), lambda i, j: (0, j))
    out_specs = [pl.BlockSpec((tm, tn), lambda i, j: (i, j))]
    out_shape = [jax.ShapeDtypeStruct((t, n), out_dtype)]
    if emit_w:
        out_specs.append(w_spec)
        out_shape.append(jax.ShapeDtypeStruct(w.shape, BF16))
    side_specs = []
    for s in side_cast:
        chunks = _side_chunks(s.shape[0], steps)
        side_specs.append(pl.BlockSpec((s.shape[0] // chunks, s.shape[1]),
                                       lambda i, j, c=chunks: (jnp.minimum(i * nj + j, c - 1), 0)))
        out_shape.append(jax.ShapeDtypeStruct(s.shape, BF16))
    outs = pl.pallas_call(
        functools.partial(_rms_matmul_kernel, emit_w=emit_w, n_side=len(side_cast)),
        grid=(t // tm, nj),
        in_specs=[x_spec, pl.BlockSpec((1, d), lambda i, j: (0, 0)), w_spec] + side_specs,
        out_specs=out_specs + side_specs, out_shape=out_shape,
        scratch_shapes=[pltpu.VMEM((tm, d), BF16)],
        compiler_params=_params("arbitrary", "arbitrary") if side_cast else _params("parallel", "arbitrary"),
        name=name,
    )(x, g.reshape(1, d), w, *side_cast)
    return outs if len(outs) > 1 else outs[0]


def _mix_kernel(*refs, tm, gate_len, widths, groups, prompt, tiles_per_seq):
    if prompt:
        (x_ref, proj_ref, cgp_ref, xinp_ref, lng_ref, lnb_ref, wsp_ref, bsp_ref, cw_ref,
         wa_ref, wb_ref, wmix_ref, h_ref, tail_ref, gate_ref, gbias_ref) = refs
    else:
        (x_ref, proj_ref, e1_ref, e2_ref, lng_ref, lnb_ref, wsp_ref, bsp_ref, cw_ref,
         wa_ref, wb_ref, wmix_ref, h_ref, p_ref, v_ref, gate_ref, gbias_ref) = refs
    aw, bw, dm = widths
    o_u, o_v, o_bg, o_cg, o_xin, o_ga, o_gb = (0, aw, 2 * aw, 2 * aw + bw, 2 * aw + 2 * bw,
                                               2 * aw + 3 * bw, 2 * aw + 3 * bw + dm)

    def col(o, w):
        return proj_ref[:, o:o + w].astype(F32)

    v = col(o_v, aw)
    mu = jnp.mean(v, axis=-1, keepdims=True)
    vc = v - mu
    var = jnp.mean(vc * vc, axis=-1, keepdims=True)
    vn = (vc * lax.rsqrt(var + EPS)) * lng_ref[...] + lnb_ref[...]
    if not prompt:
        v_ref[...] = vn
    vb = vn.astype(BF16)
    gd = aw // groups

    @pl.when(pl.program_id(0) == 0)
    def _():
        row = lax.broadcasted_iota(jnp.int32, (tm, tm), 0)
        cidx = lax.broadcasted_iota(jnp.int32, (tm, tm), 1)
        keep = ((row ^ cidx) < gate_len) & (cidx <= row)
        cs = wsp_ref.shape[1]
        pick = (lax.broadcasted_iota(jnp.int32, (tm, cs), 1)
                == (lax.broadcasted_iota(jnp.int32, (tm, cs), 0) & (gate_len - 1)))
        pick_b = jnp.where(pick, 1.0, 0.0).astype(BF16)
        for g in range(groups):
            w_rows = jnp.dot(pick_b, wsp_ref[g].astype(BF16), preferred_element_type=F32).astype(BF16)
            w_full = lax.dot_general(w_rows, pick_b, (((1,), (1,)), ((), ())), preferred_element_type=F32)
            gate_ref[g] = jnp.where(keep, w_full, 0.0).astype(BF16)
            gbias_ref[g] = jnp.sum(jnp.where(pick, bsp_ref[g:g + 1, :], 0.0), axis=1, keepdims=True)

    zs = [jnp.dot(gate_ref[g], vb[:, g * gd:(g + 1) * gd], preferred_element_type=F32) + gbias_ref[g]
          for g in range(groups)]
    z = jnp.concatenate(zs, axis=1)
    y_a = (col(o_u, aw) * z).astype(BF16)

    p = col(o_cg, bw) * col(o_xin, bw)
    trow = lax.broadcasted_iota(jnp.int32, (tm, bw), 0)
    if prompt:
        fresh = pl.program_id(0) % tiles_per_seq == 0
        pp = cgp_ref[...].astype(F32) * xinp_ref[...].astype(F32)
        pp = jnp.where(fresh, 0.0, pp)
        last1 = pp[-1:, :]
        last2 = pp[-2:-1, :]
        e1 = jnp.where(trow == 0, last1, 0.0)
        e2 = jnp.where(trow == 0, last2, jnp.where(trow == 1, last1, 0.0))
        tail_ref[...] = p[tm - 8:, :]
    else:
        trow = trow & (gate_len - 1)
        e1 = e1_ref[...]
        e2 = e2_ref[...]
        p_ref[...] = p
    s1 = jnp.where(trow >= 1, pltpu.roll(p, 1, 0), 0.0) + e1
    s2 = jnp.where(trow >= 2, pltpu.roll(p, 2, 0), 0.0) + e2
    conv = cw_ref[0:1, :] * s2 + cw_ref[1:2, :] * s1 + cw_ref[2:3, :] * p
    y_b = (col(o_bg, bw) * conv).astype(BF16)

    merged = (jax.nn.sigmoid(col(o_ga, dm)) * jnp.dot(y_a, wa_ref[...], preferred_element_type=F32)
              + jax.nn.sigmoid(col(o_gb, dm)) * jnp.dot(y_b, wb_ref[...], preferred_element_type=F32))
    h_ref[...] = x_ref[...] + jnp.dot(merged.astype(BF16), wmix_ref[...], preferred_element_type=F32)


def mix(x, proj, prev, ln_g, ln_b, w_sp, b_sp, conv_w, wa, wb, wmix, *, tm, gate_len, seq_len, name):
    t, dm = x.shape
    aw, bw = wa.shape[0], wb.shape[0]
    groups = w_sp.shape[0]
    prompt = prev is None
    nt = t // tm
    row_spec = lambda w: pl.BlockSpec((tm, w), lambda i: (i, 0))
    in_specs = [row_spec(dm), row_spec(proj.shape[1])]
    if prompt:
        prev_rows = 16
        cg_blk = (2 * aw + bw) // bw
        in_specs += [
            pl.BlockSpec((prev_rows, bw), lambda i: (jnp.maximum(i * (tm // prev_rows) - 1, 0), cg_blk)),
            pl.BlockSpec((prev_rows, bw), lambda i: (jnp.maximum(i * (tm // prev_rows) - 1, 0), cg_blk + 1)),
        ]
        extra = (proj, proj)
    else:
        in_specs += [row_spec(bw), row_spec(bw)]
        extra = prev
    in_specs += [_resident((1, aw)), _resident((1, aw)), _resident(w_sp.shape), _resident(b_sp.shape),
                 _resident(conv_w.shape), _resident(wa.shape), _resident(wb.shape), _resident(wmix.shape)]
    out_specs = [row_spec(dm)]
    out_shape = [jax.ShapeDtypeStruct((t, dm), F32)]
    if prompt:
        out_specs.append(pl.BlockSpec((8, bw), lambda i: (i, 0)))
        out_shape.append(jax.ShapeDtypeStruct((nt * 8, bw), F32))
    else:
        out_specs += [row_spec(bw), row_spec(aw)]
        out_shape += [jax.ShapeDtypeStruct((t, bw), F32), jax.ShapeDtypeStruct((t, aw), F32)]
    kern = functools.partial(_mix_kernel, tm=tm, gate_len=gate_len, widths=(aw, bw, dm), groups=groups,
                             prompt=prompt, tiles_per_seq=max(seq_len // tm, 1))
    return pl.pallas_call(
        kern, grid=(nt,), in_specs=in_specs, out_specs=out_specs, out_shape=out_shape,
        scratch_shapes=[pltpu.VMEM((groups, tm, tm), BF16), pltpu.VMEM((groups, tm, 1), F32)],
        compiler_params=_params("arbitrary"), name=name,
    )(x, proj, *extra, ln_g.reshape(1, aw), ln_b.reshape(1, aw), w_sp, b_sp, conv_w, wa, wb, wmix)


def _attn_block_kernel(h_ref, g_ref, wq_ref, k_ref, v_ref, wxo_ref, gn_ref, out_ref, hn_ref, *, heads, head_dim):
    scale = head_dim ** -0.5
    h = h_ref[...]
    q = jnp.dot(_rms(h, g_ref[...]).astype(BF16), wq_ref[...], preferred_element_type=F32).astype(BF16)
    outs = []
    for hd in range(heads):
        cols = slice(hd * head_dim, (hd + 1) * head_dim)
        k = k_ref[0, :, cols].astype(BF16)
        s = lax.dot_general(q[:, cols], k, (((1,), (1,)), ((), ())), preferred_element_type=F32) * scale
        e = jnp.exp(s - jnp.max(s, axis=-1, keepdims=True))
        p = (e / jnp.sum(e, axis=-1, keepdims=True)).astype(BF16)
        outs.append(jnp.dot(p, v_ref[0, :, cols].astype(BF16), preferred_element_type=F32).astype(BF16))
    o = jnp.concatenate(outs, axis=1)
    _residual_and_norm(h, o, wxo_ref, gn_ref, out_ref, hn_ref)


def _residual_and_norm(h, o, wxo_ref, gn_ref, out_ref, hn_ref):
    out = h + jnp.dot(o, wxo_ref[...], preferred_element_type=F32)
    out_ref[...] = out
    hn_ref[...] = _rms(out, gn_ref[...]).astype(BF16)


def attn_block(h, g, wq, k, v, wxo, g_next, *, heads, tm, seq_len, name):
    t, dm = h.shape
    m = k.shape[1]
    per_seq = seq_len // tm
    row_spec = pl.BlockSpec((tm, dm), lambda i: (i, 0))
    kv_spec = pl.BlockSpec((1, m, dm), lambda i: (i // per_seq, 0, 0))
    kern = functools.partial(_attn_block_kernel, heads=heads, head_dim=dm // heads)
    return pl.pallas_call(
        kern, grid=(t // tm,),
        in_specs=[row_spec, _resident((1, dm)), _resident(wq.shape), kv_spec, kv_spec, _resident(wxo.shape),
                  _resident((1, dm))],
        out_specs=[row_spec, row_spec],
        out_shape=[jax.ShapeDtypeStruct((t, dm), F32), jax.ShapeDtypeStruct((t, dm), BF16)],
        compiler_params=_params("parallel"), name=name,
    )(h, g.reshape(1, dm), wq, k, v, wxo, g_next.reshape(1, dm))


LANES = 128


def _attn_rows_kernel(q_ref, k_ref, v_ref, o_ref, *, n_seq, rows, heads, head_dim, n_mem):
    scale = head_dim ** -0.5
    slabs = head_dim // LANES
    per_seq = n_mem * slabs * heads
    width = n_mem * slabs
    pairs = [(b, h) for b in range(n_seq) for h in range(heads)]
    lane_slab = lax.broadcasted_iota(jnp.int32, (rows, width), 1) & (slabs - 1)
    is_slab = [lane_slab == c for c in range(slabs)]

    parts = []
    for b, h in pairs:
        r0, c0 = b * rows, h * head_dim
        qh = jnp.concatenate(
            [q_ref[r0:r0 + rows, c0 + c * LANES:c0 + (c + 1) * LANES] for c in range(slabs)], axis=0
        ).astype(BF16)
        kh = k_ref[pl.ds(b * per_seq + h, width, stride=heads), :].astype(BF16)
        g = lax.dot_general(qh, kh, (((1,), (1,)), ((), ())), preferred_element_type=F32)
        s = g[0:rows]
        for c in range(1, slabs):
            s = jnp.where(is_slab[c], g[c * rows:(c + 1) * rows], s)
        parts.append(s)
    s = jnp.concatenate(parts, axis=0)

    lane = lax.broadcasted_iota(jnp.int32, s.shape, 1)
    step = 1
    while step < slabs:
        s = s + jnp.where((lane & step) != 0, pltpu.roll(s, step, 1), pltpu.roll(s, width - step, 1))
        step *= 2
    s = s * scale
    e = jnp.exp(s - jnp.max(s, axis=-1, keepdims=True))
    p = e / (jnp.sum(e, axis=-1, keepdims=True) * (1.0 / slabs))

    for i, (b, h) in enumerate(pairs):
        r0, c0 = b * rows, h * head_dim
        ph = p[i * rows:(i + 1) * rows]
        w = jnp.concatenate([jnp.where(is_slab[c], ph, 0.0) for c in range(slabs)], axis=0).astype(BF16)
        vh = v_ref[pl.ds(b * per_seq + h, width, stride=heads), :].astype(BF16)
        o = jnp.dot(w, vh, preferred_element_type=F32)
        for c in range(slabs):
            o_ref[r0:r0 + rows, c0 + c * LANES:c0 + (c + 1) * LANES] = o[c * rows:(c + 1) * rows]


def _rows_view(a):
    n, m, heads, hd = a.shape
    a = a.reshape(n * m, heads, hd // LANES, LANES)
    return a.transpose(0, 2, 1, 3).reshape(n * m * hd // LANES * heads, LANES)


def _attn_out_kernel(h_ref, o_ref, wxo_ref, gn_ref, out_ref, hn_ref):
    _residual_and_norm(h_ref[...], o_ref[...].astype(BF16), wxo_ref, gn_ref, out_ref, hn_ref)


def attn_out(h, o, wxo, g_next, *, tm, name):
    t, dm = h.shape
    row_spec = pl.BlockSpec((tm, dm), lambda i: (i, 0))
    return pl.pallas_call(
        _attn_out_kernel, grid=(t // tm,),
        in_specs=[row_spec, row_spec, _resident(wxo.shape), _resident((1, dm))],
        out_specs=[row_spec, row_spec],
        out_shape=[jax.ShapeDtypeStruct((t, dm), F32), jax.ShapeDtypeStruct((t, dm), BF16)],
        compiler_params=_params("parallel"), name=name,
    )(h, o, wxo, g_next.reshape(1, dm))


def _mlp_kernel(hn_ref, h_hbm, wup_ref, wdn_ref, gf_ref, *rest, tm, guest):
    if guest is None:
        y_ref, sem = rest
    else:
        q_ref, k_ref, v_ref, y_ref, o_ref, sem = rest
    i, j = pl.program_id(0), pl.program_id(1)
    nj = pl.num_programs(1)
    residual_copy = pltpu.make_async_copy(h_hbm.at[pl.ds(i * tm, tm), :], y_ref, sem.at[0])

    def up():
        a = jnp.dot(hn_ref[...], wup_ref[...], preferred_element_type=F32)
        return jnp.square(jnp.maximum(a, 0.0)).astype(BF16)

    def accumulate(a):
        y_ref[...] += jnp.dot(a, wdn_ref[...], preferred_element_type=F32)

    def hosted_attention():
        if guest is not None:
            _attn_rows_kernel(q_ref, k_ref, v_ref, o_ref, n_seq=1, **guest)

    @pl.when(j == 0)
    def _():
        residual_copy.start()
        hosted_attention()
        a = up()
        residual_copy.wait()
        accumulate(a)

    @pl.when(j > 0)
    def _():
        hosted_attention()
        accumulate(up())

    @pl.when(j == nj - 1)
    def _():
        y_ref[...] = _rms(y_ref[...], gf_ref[...])


def mlp(hn, h, w_up, w_down, g_final, *, tm, fc, name, guest_attn=None):
    t, dm = h.shape
    dff = w_up.shape[1]
    grid = (t // tm, dff // fc)
    in_specs = [pl.BlockSpec((tm, dm), lambda i, j: (i, 0)),
                pl.BlockSpec(memory_space=pl.ANY),
                pl.BlockSpec((dm, fc), lambda i, j: (0, j)),
                pl.BlockSpec((fc, dm), lambda i, j: (j, 0)),
                pl.BlockSpec((1, dm), lambda i, j: (0, 0))]
    out_specs = [pl.BlockSpec((tm, dm), lambda i, j: (i, 0))]
    out_shape = [jax.ShapeDtypeStruct((t, dm), F32)]
    scratch = [pltpu.SemaphoreType.DMA((1,))]
    args = [hn, h, w_up, w_down, g_final.reshape(1, dm)]
    guest = None
    if guest_attn is not None:
        q, k, v = guest_attn
        n, n_mem, heads, hd = k.shape
        rows = q.shape[0] // n
        per_seq = n_mem * heads * hd // LANES
        assert n == grid[0] * grid[1] and rows % 8 == 0
        guest = dict(rows=rows, heads=heads, head_dim=hd, n_mem=n_mem)
        seq_spec = pl.BlockSpec((rows, dm), lambda i, j: (i * grid[1] + j, 0))
        kv_spec = pl.BlockSpec((per_seq, LANES), lambda i, j: (i * grid[1] + j, 0))
        in_specs += [seq_spec, kv_spec, kv_spec]
        out_specs.append(seq_spec)
        out_shape.append(jax.ShapeDtypeStruct(q.shape, F32))
        args += [q, _rows_view(k), _rows_view(v)]
    outs = pl.pallas_call(
        functools.partial(_mlp_kernel, tm=tm, guest=guest), grid=grid,
        in_specs=in_specs, out_specs=out_specs, out_shape=out_shape, scratch_shapes=scratch,
        compiler_params=_params("arbitrary", "arbitrary"), name=name,
    )(*args)
    return outs if guest_attn is not None else outs[0]


MIX_TM = 256


ROW_TM = 1024
COL_TN = 1024
ATTN_TM = 512
MLP_FC = 512


def _gate_len(seq_len, w_spatial):
    gate_len = CHUNK if seq_len % CHUNK == 0 else seq_len
    assert gate_len & (gate_len - 1) == 0 and MIX_TM % gate_len == 0 and gate_len <= w_spatial.shape[1]
    return gate_len


def kernel(x_prompt, x_sample, state_conv, cache_mem_k, cache_mem_v, mem_prompt, norm_mix_g, w_in, ln_v_g, ln_v_b, w_spatial, b_spatial, conv_w, w_branch_a, w_branch_b, w_mix_out, norm_x_g, norm_mem_g, w_q, w_k, w_v, w_x_out, norm_mlp_g, w_up, w_down, norm_final_g):
    depth = w_in.shape[0]
    assert depth == 1, "the final rmsnorm is fused into the single layer's MLP kernel"
    nb, seq, dm = x_prompt.shape
    nd, dseq, _ = x_sample.shape
    n_mem, heads, hd = cache_mem_k.shape[2:]
    l = 0
    xs = x_sample.reshape(nd * dseq, dm)
    xp = x_prompt.reshape(nb * seq, dm)
    assert xs.shape[0] == ROW_TM

    proj_s, w_in_b = rms_matmul(xs, norm_mix_g[l], w_in[l], BF16, tm=ROW_TM, tn=COL_TN, emit_w=True,
                                name="in_proj_sample")
    proj_p, w_a, w_b, w_mix, w_q_b, w_xo, w_up_b, w_down_b = rms_matmul(
        xp, norm_mix_g[l], w_in_b, BF16, tm=ROW_TM, tn=COL_TN, name="in_proj_prompt",
        side_cast=(w_branch_a[l], w_branch_b[l], w_mix_out[l], w_q[l], w_x_out[l], w_up[l], w_down[l]))
    mix_w = (ln_v_g[l], ln_v_b[l], w_spatial[l], b_spatial[l], conv_w[l], w_a, w_b, w_mix)

    prev = state_conv[l]
    e1 = jnp.pad(prev[:, 1:2], ((0, 0), (0, dseq - 1), (0, 0))).reshape(nd * dseq, -1)
    e2 = jnp.pad(prev, ((0, 0), (0, dseq - 2), (0, 0))).reshape(nd * dseq, -1)
    h_s, p_s, vn_s = mix(xs, proj_s, (e1, e2), *mix_w, tm=MIX_TM, gate_len=_gate_len(dseq, w_spatial[l]),
                         seq_len=dseq, name="mix_sample")
    q_s = rms_matmul(h_s, norm_x_g[l], w_q_b, F32, tm=ROW_TM, tn=COL_TN, name="q_proj_sample")

    mem = mem_prompt.reshape(nb * n_mem, dm)
    k_p = rms_matmul(mem, norm_mem_g[l], w_k[l], F32, tm=nb * n_mem, tn=COL_TN, name="mem_k")
    v_p = rms_matmul(mem, norm_mem_g[l], w_v[l], F32, tm=nb * n_mem, tn=COL_TN, name="mem_v")
    h_p, tail_p = mix(xp, proj_p, None, *mix_w, tm=MIX_TM, gate_len=_gate_len(seq, w_spatial[l]),
                      seq_len=seq, name="mix_prompt")
    h_p, hn_p = attn_block(h_p, norm_x_g[l], w_q_b, k_p.reshape(nb, n_mem, dm), v_p.reshape(nb, n_mem, dm), w_xo,
                           norm_mlp_g[l], heads=heads, tm=ATTN_TM, seq_len=seq, name="attn_block_prompt")
    y_p, o_s = mlp(hn_p, h_p, w_up_b, w_down_b, norm_final_g, tm=ROW_TM, fc=MLP_FC, name="mlp_prompt",
                   guest_attn=(q_s, cache_mem_k[l], cache_mem_v[l]))
    h_s, hn_s = attn_out(h_s, o_s, w_xo, norm_mlp_g[l], tm=ATTN_TM, name="x_out_sample")
    y_s = mlp(hn_s, h_s, w_up_b, w_down_b, norm_final_g, tm=ROW_TM, fc=MLP_FC, name="mlp_sample")
    k_p = k_p.reshape(nb, n_mem, heads, hd)
    v_p = v_p.reshape(nb, n_mem, heads, hd)

    bw = p_s.shape[1]
    keep = conv_w.shape[1] - 1
    conv_p = tail_p.reshape(nb, seq // MIX_TM, 8, bw)[:, -1, 8 - keep:, :]
    conv_s = p_s.reshape(nd, dseq, bw)[:, dseq - keep:, :]
    return (y_p.reshape(nb, seq, dm), y_s.reshape(nd, dseq, dm), k_p[None], v_p[None],
            conv_p[None], conv_s[None], vn_s.reshape(1, nd, dseq, -1))
```

```python
import functools

import jax
import jax.numpy as jnp
from jax import lax
from jax.experimental import pallas as pl
from jax.experimental.pallas import tpu as pltpu

EPS = 1e-6
CHUNK = 128
F32 = jnp.float32
BF16 = jnp.bfloat16

V7X_VMEM_BYTES = 64 * 1024 * 1024
VMEM_LIMIT = V7X_VMEM_BYTES * 7 // 8


def _params(*sem):
    return pltpu.CompilerParams(dimension_semantics=sem, vmem_limit_bytes=VMEM_LIMIT)


def _rms(x, g):
    r = lax.rsqrt(jnp.mean(x * x, axis=-1, keepdims=True) + EPS)
    return (x * r) * g


def _resident(shape):
    return pl.BlockSpec(shape, lambda *_: (0,) * len(shape), pipeline_mode=pl.Buffered(1))


def _bf16_tile(w_ref, wb_ref):
    if wb_ref is None:
        return w_ref[...].astype(BF16)
    wb_ref[...] = w_ref[...].astype(BF16)
    return wb_ref[...]


def _rms_matmul_kernel(x_ref, g_ref, w_ref, *rest, emit_w, n_side):
    side_in, rest = rest[:n_side], rest[n_side:]
    o_ref, rest = rest[0], rest[1:]
    wb_ref, rest = (rest[0], rest[1:]) if emit_w else (None, rest)
    side_out, xn_ref = rest[:n_side], rest[n_side]

    @pl.when(pl.program_id(1) == 0)
    def _():
        xn_ref[...] = _rms(x_ref[...], g_ref[...]).astype(BF16)

    w = _bf16_tile(w_ref, wb_ref)
    o_ref[...] = jnp.dot(xn_ref[...], w, preferred_element_type=F32).astype(o_ref.dtype)
    for src, dst in zip(side_in, side_out):
        dst[...] = src[...].astype(BF16)


BF16_SUBLANES = 16


def _side_chunks(rows, steps):
    return max(n for n in range(1, steps + 1) if rows % n == 0 and (rows // n) % BF16_SUBLANES == 0)


def rms_matmul(x, g, w, out_dtype, *, tm, tn, name, emit_w=False, side_cast=()):
    t, d = x.shape
    n = w.shape[1]
    one_tile = t == tm
    assert one_tile or not emit_w
    nj = n // tn
    steps = (t // tm) * nj
    x_spec = (pl.BlockSpec((tm, d), lambda i, j: (i, 0), pipeline_mode=pl.Buffered(1)) if one_tile
              else pl.BlockSpec((tm, d), lambda i, j: (i, 0)))
    w_spec = pl.BlockSpec((d, tn), lambda i, j: (0, j))
    out_specs = [pl.BlockSpec((tm, tn), lambda i, j: (i, j))]
    out_shape = [jax.ShapeDtypeStruct((t, n), out_dtype)]
    if emit_w:
        out_specs.append(w_spec)
        out_shape.append(jax.ShapeDtypeStruct(w.shape, BF16))
    side_specs = []
    for s in side_cast:
        chunks = _side_chunks(s.shape[0], steps)
        side_specs.append(pl.BlockSpec((s.shape[0] // chunks, s.shape[1]),
                                       lambda i, j, c=chunks: (jnp.minimum(i * nj + j, c - 1), 0)))
        out_shape.append(jax.ShapeDtypeStruct(s.shape, BF16))
    outs = pl.pallas_call(
        functools.partial(_rms_matmul_kernel, emit_w=emit_w, n_side=len(side_cast)),
        grid=(t // tm, nj),
        in_specs=[x_spec, pl.BlockSpec((1, d), lambda i, j: (0, 0)), w_spec] + side_specs,
        out_specs=out_specs + side_specs, out_shape=out_shape,
        scratch_shapes=[pltpu.VMEM((tm, d), BF16)],
        compiler_params=_params("arbitrary", "arbitrary") if side_cast else _params("parallel", "arbitrary"),
        name=name,
    )(x, g.reshape(1, d), w, *side_cast)
    return outs if len(outs) > 1 else outs[0]


def _mix_kernel(*refs, tm, gate_len, widths, groups, prompt, tiles_per_seq):
    if prompt:
        (x_ref, proj_ref, cgp_ref, xinp_ref, lng_ref, lnb_ref, wsp_ref, bsp_ref, cw_ref,
         wa_ref, wb_ref, wmix_ref, h_ref, tail_ref, gate_ref, gbias_ref) = refs
    else:
        (x_ref, proj_ref, e1_ref, e2_ref, lng_ref, lnb_ref, wsp_ref, bsp_ref, cw_ref,
         wa_ref, wb_ref, wmix_ref, h_ref, p_ref, v_ref, gate_ref, gbias_ref) = refs
    aw, bw, dm = widths
    o_u, o_v, o_bg, o_cg, o_xin, o_ga, o_gb = (0, aw, 2 * aw, 2 * aw + bw, 2 * aw + 2 * bw,
                                               2 * aw + 3 * bw, 2 * aw + 3 * bw + dm)

    def col(o, w):
        return proj_ref[:, o:o + w].astype(F32)

    v = col(o_v, aw)
    mu = jnp.mean(v, axis=-1, keepdims=True)
    vc = v - mu
    var = jnp.mean(vc * vc, axis=-1, keepdims=True)
    vn = (vc * lax.rsqrt(var + EPS)) * lng_ref[...] + lnb_ref[...]
    if not prompt:
        v_ref[...] = vn
    vb = vn.astype(BF16)
    gd = aw // groups

    @pl.when(pl.program_id(0) == 0)
    def _():
        row = lax.broadcasted_iota(jnp.int32, (tm, tm), 0)
        cidx = lax.broadcasted_iota(jnp.int32, (tm, tm), 1)
        keep = ((row ^ cidx) < gate_len) & (cidx <= row)
        cs = wsp_ref.shape[1]
        pick = (lax.broadcasted_iota(jnp.int32, (tm, cs), 1)
                == (lax.broadcasted_iota(jnp.int32, (tm, cs), 0) & (gate_len - 1)))
        pick_b = jnp.where(pick, 1.0, 0.0).astype(BF16)
        for g in range(groups):
            w_rows = jnp.dot(pick_b, wsp_ref[g].astype(BF16), preferred_element_type=F32).astype(BF16)
            w_full = lax.dot_general(w_rows, pick_b, (((1,), (1,)), ((), ())), preferred_element_type=F32)
            gate_ref[g] = jnp.where(keep, w_full, 0.0).astype(BF16)
            gbias_ref[g] = jnp.sum(jnp.where(pick, bsp_ref[g:g + 1, :], 0.0), axis=1, keepdims=True)

    zs = [jnp.dot(gate_ref[g], vb[:, g * gd:(g + 1) * gd], preferred_element_type=F32) + gbias_ref[g]
          for g in range(groups)]
    z = jnp.concatenate(zs, axis=1)
    y_a = (col(o_u, aw) * z).astype(BF16)

    p = col(o_cg, bw) * col(o_xin, bw)
    trow = lax.broadcasted_iota(jnp.int32, (tm, bw), 0)
    if prompt:
        fresh = pl.program_id(0) % tiles_per_seq == 0
        pp = cgp_ref[...].astype(F32) * xinp_ref[...].astype(F32)
        pp = jnp.where(fresh, 0.0, pp)
        last1 = pp[-1:, :]
        last2 = pp[-2:-1, :]
        e1 = jnp.where(trow == 0, last1, 0.0)
        e2 = jnp.where(trow == 0, last2, jnp.where(trow == 1, last1, 0.0))
        tail_ref[...] = p[tm - 8:, :]
    else:
        trow = trow & (gate_len - 1)
        e1 = e1_ref[...]
        e2 = e2_ref[...]
        p_ref[...] = p
    s1 = jnp.where(trow >= 1, pltpu.roll(p, 1, 0), 0.0) + e1
    s2 = jnp.where(trow >= 2, pltpu.roll(p, 2, 0), 0.0) + e2
    conv = cw_ref[0:1, :] * s2 + cw_ref[1:2, :] * s1 + cw_ref[2:3, :] * p
    y_b = (col(o_bg, bw) * conv).astype(BF16)

    merged = (jax.nn.sigmoid(col(o_ga, dm)) * jnp.dot(y_a, wa_ref[...], preferred_element_type=F32)
              + jax.nn.sigmoid(col(o_gb, dm)) * jnp.dot(y_b, wb_ref[...], preferred_element_type=F32))
    h_ref[...] = x_ref[...] + jnp.dot(merged.astype(BF16), wmix_ref[...], preferred_element_type=F32)


def mix(x, proj, prev, ln_g, ln_b, w_sp, b_sp, conv_w, wa, wb, wmix, *, tm, gate_len, seq_len, name):
    t, dm = x.shape
    aw, bw = wa.shape[0], wb.shape[0]
    groups = w_sp.shape[0]
    prompt = prev is None
    nt = t // tm
    row_spec = lambda w: pl.BlockSpec((tm, w), lambda i: (i, 0))
    in_specs = [row_spec(dm), row_spec(proj.shape[1])]
    if prompt:
        prev_rows = 16
        cg_blk = (2 * aw + bw) // bw
        in_specs += [
            pl.BlockSpec((prev_rows, bw), lambda i: (jnp.maximum(i * (tm // prev_rows) - 1, 0), cg_blk)),
            pl.BlockSpec((prev_rows, bw), lambda i: (jnp.maximum(i * (tm // prev_rows) - 1, 0), cg_blk + 1)),
        ]
        extra = (proj, proj)
    else:
        in_specs += [row_spec(bw), row_spec(bw)]
        extra = prev
    in_specs += [_resident((1, aw)), _resident((1, aw)), _resident(w_sp.shape), _resident(b_sp.shape),
                 _resident(conv_w.shape), _resident(wa.shape), _resident(wb.shape), _resident(wmix.shape)]
    out_specs = [row_spec(dm)]
    out_shape = [jax.ShapeDtypeStruct((t, dm), F32)]
    if prompt:
        out_specs.append(pl.BlockSpec((8, bw), lambda i: (i, 0)))
        out_shape.append(jax.ShapeDtypeStruct((nt * 8, bw), F32))
    else:
        out_specs += [row_spec(bw), row_spec(aw)]
        out_shape += [jax.ShapeDtypeStruct((t, bw), F32), jax.ShapeDtypeStruct((t, aw), F32)]
    kern = functools.partial(_mix_kernel, tm=tm, gate_len=gate_len, widths=(aw, bw, dm), groups=groups,
                             prompt=prompt, tiles_per_seq=max(seq_len // tm, 1))
    return pl.pallas_call(
        kern, grid=(nt,), in_specs=in_specs, out_specs=out_specs, out_shape=out_shape,
        scratch_shapes=[pltpu.VMEM((groups, tm, tm), BF16), pltpu.VMEM((groups, tm, 1), F32)],
        compiler_params=_params("arbitrary"), name=name,
    )(x, proj, *extra, ln_g.reshape(1, aw), ln_b.reshape(1, aw), w_sp, b_sp, conv_w, wa, wb, wmix)


def _attn_block_kernel(h_ref, g_ref, wq_ref, k_ref, v_ref, wxo_ref, gn_ref, out_ref, hn_ref, *, heads, head_dim):
    scale = head_dim ** -0.5
    h = h_ref[...]
    q = jnp.dot(_rms(h, g_ref[...]).astype(BF16), wq_ref[...], preferred_element_type=F32).astype(BF16)
    outs = []
    for hd in range(heads):
        cols = slice(hd * head_dim, (hd + 1) * head_dim)
        k = k_ref[0, :, cols].astype(BF16)
        s = lax.dot_general(q[:, cols], k, (((1,), (1,)), ((), ())), preferred_element_type=F32) * scale
        e = jnp.exp(s - jnp.max(s, axis=-1, keepdims=True))
        p = (e / jnp.sum(e, axis=-1, keepdims=True)).astype(BF16)
        outs.append(jnp.dot(p, v_ref[0, :, cols].astype(BF16), preferred_element_type=F32).astype(BF16))
    o = jnp.concatenate(outs, axis=1)
    _residual_and_norm(h, o, wxo_ref, gn_ref, out_ref, hn_ref)


def _residual_and_norm(h, o, wxo_ref, gn_ref, out_ref, hn_ref):
    out = h + jnp.dot(o, wxo_ref[...], preferred_element_type=F32)
    out_ref[...] = out
    hn_ref[...] = _rms(out, gn_ref[...]).astype(BF16)


def attn_block(h, g, wq, k, v, wxo, g_next, *, heads, tm, seq_len, name):
    t, dm = h.shape
    m = k.shape[1]
    per_seq = seq_len // tm
    row_spec = pl.BlockSpec((tm, dm), lambda i: (i, 0))
    kv_spec = pl.BlockSpec((1, m, dm), lambda i: (i // per_seq, 0, 0))
    kern = functools.partial(_attn_block_kernel, heads=heads, head_dim=dm // heads)
    return pl.pallas_call(
        kern, grid=(t // tm,),
        in_specs=[row_spec, _resident((1, dm)), _resident(wq.shape), kv_spec, kv_spec, _resident(wxo.shape),
                  _resident((1, dm))],
        out_specs=[row_spec, row_spec],
        out_shape=[jax.ShapeDtypeStruct((t, dm), F32), jax.ShapeDtypeStruct((t, dm), BF16)],
        compiler_params=_params("parallel"), name=name,
    )(h, g.reshape(1, dm), wq, k, v, wxo, g_next.reshape(1, dm))


LANES = 128


def _attn_rows_kernel(q_ref, k_ref, v_ref, o_ref, *, n_seq, rows, heads, head_dim, n_mem):
    scale = head_dim ** -0.5
    slabs = head_dim // LANES
    per_seq = n_mem * slabs * heads
    width = n_mem * slabs
    pairs = [(b, h) for b in range(n_seq) for h in range(heads)]
    lane_slab = lax.broadcasted_iota(jnp.int32, (rows, width), 1) & (slabs - 1)
    is_slab = [lane_slab == c for c in range(slabs)]

    parts = []
    for b, h in pairs:
        r0, c0 = b * rows, h * head_dim
        qh = jnp.concatenate(
            [q_ref[r0:r0 + rows, c0 + c * LANES:c0 + (c + 1) * LANES] for c in range(slabs)], axis=0
        ).astype(BF16)
        kh = k_ref[pl.ds(b * per_seq + h, width, stride=heads), :].astype(BF16)
        g = lax.dot_general(qh, kh, (((1,), (1,)), ((), ())), preferred_element_type=F32)
        s = g[0:rows]
        for c in range(1, slabs):
            s = jnp.where(is_slab[c], g[c * rows:(c + 1) * rows], s)
        parts.append(s)
    s = jnp.concatenate(parts, axis=0)

    lane = lax.broadcasted_iota(jnp.int32, s.shape, 1)
    step = 1
    while step < slabs:
        s = s + jnp.where((lane & step) != 0, pltpu.roll(s, step, 1), pltpu.roll(s, width - step, 1))
        step *= 2
    s = s * scale
    e = jnp.exp(s - jnp.max(s, axis=-1, keepdims=True))
    p = e / (jnp.sum(e, axis=-1, keepdims=True) * (1.0 / slabs))

    for i, (b, h) in enumerate(pairs):
        r0, c0 = b * rows, h * head_dim
        ph = p[i * rows:(i + 1) * rows]
        w = jnp.concatenate([jnp.where(is_slab[c], ph, 0.0) for c in range(slabs)], axis=0).astype(BF16)
        vh = v_ref[pl.ds(b * per_seq + h, width, stride=heads), :].astype(BF16)
        o = jnp.dot(w, vh, preferred_element_type=F32)
        for c in range(slabs):
            o_ref[r0:r0 + rows, c0 + c * LANES:c0 + (c + 1) * LANES] = o[c * rows:(c + 1) * rows]


def _rows_view(a):
    n, m, heads, hd = a.shape
    a = a.reshape(n * m, heads, hd // LANES, LANES)
    return a.transpose(0, 2, 1, 3).reshape(n * m * hd // LANES * heads, LANES)


def attn_rows(q, k, v, *, n_seq, name):
    n, n_mem, heads, hd = k.shape
    t, dm = q.shape
    rows = t // n
    slabs = hd // LANES
    assert slabs & (slabs - 1) == 0 and rows % 8 == 0
    per_seq = n_mem * slabs * heads
    kv_spec = pl.BlockSpec((n_seq * per_seq, LANES), lambda i: (i, 0))
    q_spec = pl.BlockSpec((n_seq * rows, dm), lambda i: (i, 0))
    kern = functools.partial(_attn_rows_kernel, n_seq=n_seq, rows=rows, heads=heads, head_dim=hd, n_mem=n_mem)
    return pl.pallas_call(
        kern, grid=(n // n_seq,),
        in_specs=[q_spec, kv_spec, kv_spec], out_specs=q_spec,
        out_shape=jax.ShapeDtypeStruct((t, dm), F32),
        compiler_params=_params("parallel"), name=name,
    )(q, _rows_view(k), _rows_view(v))


def _attn_out_kernel(h_ref, o_ref, wxo_ref, gn_ref, out_ref, hn_ref):
    _residual_and_norm(h_ref[...], o_ref[...].astype(BF16), wxo_ref, gn_ref, out_ref, hn_ref)


def attn_out(h, o, wxo, g_next, *, tm, name):
    t, dm = h.shape
    row_spec = pl.BlockSpec((tm, dm), lambda i: (i, 0))
    return pl.pallas_call(
        _attn_out_kernel, grid=(t // tm,),
        in_specs=[row_spec, row_spec, _resident(wxo.shape), _resident((1, dm))],
        out_specs=[row_spec, row_spec],
        out_shape=[jax.ShapeDtypeStruct((t, dm), F32), jax.ShapeDtypeStruct((t, dm), BF16)],
        compiler_params=_params("parallel"), name=name,
    )(h, o, wxo, g_next.reshape(1, dm))


def _mlp_kernel(hn_ref, h_hbm, wup_ref, wdn_ref, gf_ref, y_ref, sem, *, tm):
    i, j = pl.program_id(0), pl.program_id(1)
    residual_copy = pltpu.make_async_copy(h_hbm.at[pl.ds(i * tm, tm), :], y_ref, sem.at[0])

    def up():
        a = jnp.dot(hn_ref[...], wup_ref[...], preferred_element_type=F32)
        return jnp.square(jnp.maximum(a, 0.0)).astype(BF16)

    def accumulate(a):
        y_ref[...] += jnp.dot(a, wdn_ref[...], preferred_element_type=F32)

    @pl.when(j == 0)
    def _():
        residual_copy.start()
        a = up()
        residual_copy.wait()
        accumulate(a)

    @pl.when(j > 0)
    def _():
        accumulate(up())

    @pl.when(j == pl.num_programs(1) - 1)
    def _():
        y_ref[...] = _rms(y_ref[...], gf_ref[...])


def mlp(hn, h, w_up, w_down, g_final, *, tm, fc, name):
    t, dm = h.shape
    dff = w_up.shape[1]
    return pl.pallas_call(
        functools.partial(_mlp_kernel, tm=tm), grid=(t // tm, dff // fc),
        in_specs=[pl.BlockSpec((tm, dm), lambda i, j: (i, 0)),
                  pl.BlockSpec(memory_space=pl.ANY),
                  pl.BlockSpec((dm, fc), lambda i, j: (0, j)),
                  pl.BlockSpec((fc, dm), lambda i, j: (j, 0)),
                  pl.BlockSpec((1, dm), lambda i, j: (0, 0))],
        out_specs=pl.BlockSpec((tm, dm), lambda i, j: (i, 0)),
        out_shape=jax.ShapeDtypeStruct((t, dm), F32),
        scratch_shapes=[pltpu.SemaphoreType.DMA((1,))],
        compiler_params=_params("arbitrary", "arbitrary"), name=name,
    )(hn, h, w_up, w_down, g_final.reshape(1, dm))


MIX_TM = 256


ROW_TM = 1024
COL_TN = 1024
ATTN_TM = 512
MLP_FC = 1024


def _gate_len(seq_len, w_spatial):
    gate_len = CHUNK if seq_len % CHUNK == 0 else seq_len
    assert gate_len & (gate_len - 1) == 0 and MIX_TM % gate_len == 0 and gate_len <= w_spatial.shape[1]
    return gate_len


def kernel(x_prompt, x_sample, state_conv, cache_mem_k, cache_mem_v, mem_prompt, norm_mix_g, w_in, ln_v_g, ln_v_b, w_spatial, b_spatial, conv_w, w_branch_a, w_branch_b, w_mix_out, norm_x_g, norm_mem_g, w_q, w_k, w_v, w_x_out, norm_mlp_g, w_up, w_down, norm_final_g):
    depth = w_in.shape[0]
    assert depth == 1, "the final rmsnorm is fused into the single layer's MLP kernel"
    nb, seq, dm = x_prompt.shape
    nd, dseq, _ = x_sample.shape
    n_mem, heads, hd = cache_mem_k.shape[2:]
    l = 0
    xs = x_sample.reshape(nd * dseq, dm)
    xp = x_prompt.reshape(nb * seq, dm)
    assert xs.shape[0] == ROW_TM

    proj_s, w_in_b = rms_matmul(xs, norm_mix_g[l], w_in[l], BF16, tm=ROW_TM, tn=COL_TN, emit_w=True,
                                name="in_proj_sample")
    proj_p, w_a, w_b, w_mix, w_q_b, w_xo, w_up_b, w_down_b = rms_matmul(
        xp, norm_mix_g[l], w_in_b, BF16, tm=ROW_TM, tn=COL_TN, name="in_proj_prompt",
        side_cast=(w_branch_a[l], w_branch_b[l], w_mix_out[l], w_q[l], w_x_out[l], w_up[l], w_down[l]))
    mix_w = (ln_v_g[l], ln_v_b[l], w_spatial[l], b_spatial[l], conv_w[l], w_a, w_b, w_mix)

    prev = state_conv[l]
    e1 = jnp.pad(prev[:, 1:2], ((0, 0), (0, dseq - 1), (0, 0))).reshape(nd * dseq, -1)
    e2 = jnp.pad(prev, ((0, 0), (0, dseq - 2), (0, 0))).reshape(nd * dseq, -1)
    h_s, p_s, vn_s = mix(xs, proj_s, (e1, e2), *mix_w, tm=MIX_TM, gate_len=_gate_len(dseq, w_spatial[l]),
                         seq_len=dseq, name="mix_sample")
    q_s = rms_matmul(h_s, norm_x_g[l], w_q_b, F32, tm=ROW_TM, tn=COL_TN, name="q_proj_sample")
    o_s = attn_rows(q_s, cache_mem_k[l], cache_mem_v[l], n_seq=4, name="attn_sample")
    h_s, hn_s = attn_out(h_s, o_s, w_xo, norm_mlp_g[l], tm=ATTN_TM, name="x_out_sample")
    y_s = mlp(hn_s, h_s, w_up_b, w_down_b, norm_final_g, tm=ROW_TM, fc=MLP_FC, name="mlp_sample")

    mem = mem_prompt.reshape(nb * n_mem, dm)
    k_p = rms_matmul(mem, norm_mem_g[l], w_k[l], F32, tm=nb * n_mem, tn=COL_TN, name="mem_k")
    v_p = rms_matmul(mem, norm_mem_g[l], w_v[l], F32, tm=nb * n_mem, tn=COL_TN, name="mem_v")
    h_p, tail_p = mix(xp, proj_p, None, *mix_w, tm=MIX_TM, gate_len=_gate_len(seq, w_spatial[l]),
                      seq_len=seq, name="mix_prompt")
    h_p, hn_p = attn_block(h_p, norm_x_g[l], w_q_b, k_p.reshape(nb, n_mem, dm), v_p.reshape(nb, n_mem, dm), w_xo,
                           norm_mlp_g[l], heads=heads, tm=ATTN_TM, seq_len=seq, name="attn_block_prompt")
    y_p = mlp(hn_p, h_p, w_up_b, w_down_b, norm_final_g, tm=ROW_TM, fc=MLP_FC, name="mlp_prompt")
    k_p = k_p.reshape(nb, n_mem, heads, hd)
    v_p = v_p.reshape(nb, n_mem, heads, hd)

    bw = p_s.shape[1]
    keep = conv_w.shape[1] - 1
    conv_p = tail_p.reshape(nb, seq // MIX_TM, 8, bw)[:, -1, 8 - keep:, :]
    conv_s = p_s.reshape(nd, dseq, bw)[:, dseq - keep:, :]
    return (y_p.reshape(nb, seq, dm), y_s.reshape(nd, dseq, dm), k_p[None], v_p[None],
            conv_p[None], conv_s[None], vn_s.reshape(1, nd, dseq, -1))
```

```python
import functools

import jax
import jax.numpy as jnp
from jax import lax
from jax.experimental import pallas as pl
from jax.experimental.pallas import tpu as pltpu

EPS = 1e-6
CHUNK = 128
F32 = jnp.float32
BF16 = jnp.bfloat16

V7X_VMEM_BYTES = 64 * 1024 * 1024
VMEM_LIMIT = V7X_VMEM_BYTES * 7 // 8


def _params(*sem):
    return pltpu.CompilerParams(dimension_semantics=sem, vmem_limit_bytes=VMEM_LIMIT)


def _rms(x, g):
    r = lax.rsqrt(jnp.mean(x * x, axis=-1, keepdims=True) + EPS)
    return (x * r) * g


def _resident(shape):
    return pl.BlockSpec(shape, lambda *_: (0,) * len(shape), pipeline_mode=pl.Buffered(1))


def _bf16_tile(w_ref, wb_ref):
    if wb_ref is None:
        return w_ref[...].astype(BF16)
    wb_ref[...] = w_ref[...].astype(BF16)
    return wb_ref[...]


def _rms_matmul_kernel(x_ref, g_ref, w_ref, *rest, emit_w, n_side):
    side_in, rest = rest[:n_side], rest[n_side:]
    o_ref, rest = rest[0], rest[1:]
    wb_ref, rest = (rest[0], rest[1:]) if emit_w else (None, rest)
    side_out, xn_ref = rest[:n_side], rest[n_side]

    @pl.when(pl.program_id(1) == 0)
    def _():
        xn_ref[...] = _rms(x_ref[...], g_ref[...]).astype(BF16)

    w = _bf16_tile(w_ref, wb_ref)
    o_ref[...] = jnp.dot(xn_ref[...], w, preferred_element_type=F32).astype(o_ref.dtype)
    for src, dst in zip(side_in, side_out):
        dst[...] = src[...].astype(BF16)


BF16_SUBLANES = 16


def _side_chunks(rows, steps):
    return max(n for n in range(1, steps + 1) if rows % n == 0 and (rows // n) % BF16_SUBLANES == 0)


def rms_matmul(x, g, w, out_dtype, *, tm, tn, name, emit_w=False, side_cast=()):
    t, d = x.shape
    n = w.shape[1]
    one_tile = t == tm
    assert one_tile or not emit_w
    nj = n // tn
    steps = (t // tm) * nj
    x_spec = (pl.BlockSpec((tm, d), lambda i, j: (i, 0), pipeline_mode=pl.Buffered(1)) if one_tile
              else pl.BlockSpec((tm, d), lambda i, j: (i, 0)))
    w_spec = pl.BlockSpec((d, tn), lambda i, j: (0, j))
    out_specs = [pl.BlockSpec((tm, tn), lambda i, j: (i, j))]
    out_shape = [jax.ShapeDtypeStruct((t, n), out_dtype)]
    if emit_w:
        out_specs.append(w_spec)
        out_shape.append(jax.ShapeDtypeStruct(w.shape, BF16))
    side_specs = []
    for s in side_cast:
        chunks = _side_chunks(s.shape[0], steps)
        side_specs.append(pl.BlockSpec((s.shape[0] // chunks, s.shape[1]),
                                       lambda i, j, c=chunks: (jnp.minimum(i * nj + j, c - 1), 0)))
        out_shape.append(jax.ShapeDtypeStruct(s.shape, BF16))
    outs = pl.pallas_call(
        functools.partial(_rms_matmul_kernel, emit_w=emit_w, n_side=len(side_cast)),
        grid=(t // tm, nj),
        in_specs=[x_spec, pl.BlockSpec((1, d), lambda i, j: (0, 0)), w_spec] + side_specs,
        out_specs=out_specs + side_specs, out_shape=out_shape,
        scratch_shapes=[pltpu.VMEM((tm, d), BF16)],
        compiler_params=_params("arbitrary", "arbitrary") if side_cast else _params("parallel", "arbitrary"),
        name=name,
    )(x, g.reshape(1, d), w, *side_cast)
    return outs if len(outs) > 1 else outs[0]


LANES = 128


def _mem_kv_kernel(x_ref, g_ref, wk_ref, wv_ref, k_ref, v_ref, kc_ref, vc_ref, xn_ref, *, heads):
    head = pl.program_id(0)

    @pl.when(head == 0)
    def _():
        xn_ref[...] = _rms(x_ref[...], g_ref[...]).astype(BF16)

    rows = x_ref.shape[0]
    slabs = wk_ref.shape[1] // LANES
    for w_ref, rows_ref, cache_ref in ((wk_ref, k_ref, kc_ref), (wv_ref, v_ref, vc_ref)):
        o = jnp.dot(xn_ref[...], w_ref[...].astype(BF16), preferred_element_type=F32)
        rows_ref[...] = o.astype(rows_ref.dtype)
        for c in range(slabs):
            cache_ref[pl.ds(c * heads + head, rows, stride=slabs * heads), :] = o[:, c * LANES:(c + 1) * LANES]


def mem_kv(mem, g, w_k, w_v, *, heads, name):
    t, d = mem.shape
    hd = w_k.shape[1] // heads
    once = dict(pipeline_mode=pl.Buffered(1))
    w_spec = pl.BlockSpec((d, hd), lambda j: (0, j))
    rows_spec = pl.BlockSpec((t, hd), lambda j: (0, j))
    cache_spec = pl.BlockSpec((t * d // LANES, LANES), lambda j: (0, 0), **once)
    return pl.pallas_call(
        functools.partial(_mem_kv_kernel, heads=heads), grid=(heads,),
        in_specs=[pl.BlockSpec((t, d), lambda j: (0, 0), **once), pl.BlockSpec((1, d), lambda j: (0, 0)),
                  w_spec, w_spec],
        out_specs=[rows_spec, rows_spec, cache_spec, cache_spec],
        out_shape=[jax.ShapeDtypeStruct((t, d), BF16)] * 2 + [jax.ShapeDtypeStruct((t * d // LANES, LANES), F32)] * 2,
        scratch_shapes=[pltpu.VMEM((t, d), BF16)],
        compiler_params=_params("arbitrary"), name=name,
    )(mem, g.reshape(1, d), w_k, w_v)


def _cache_view(rows, n, m, heads, hd):
    a = rows.reshape(n * m, hd // LANES, heads, LANES)
    return a.transpose(0, 2, 1, 3).reshape(n, m, heads, hd)


def _mix_kernel(*refs, tm, gate_len, widths, groups, prompt, tiles_per_seq):
    if prompt:
        (x_ref, proj_ref, cgp_ref, xinp_ref, lng_ref, lnb_ref, wsp_ref, bsp_ref, cw_ref,
         wa_ref, wb_ref, wmix_ref, h_ref, tail_ref, gate_ref, gbias_ref) = refs
    else:
        (x_ref, proj_ref, e1_ref, e2_ref, lng_ref, lnb_ref, wsp_ref, bsp_ref, cw_ref,
         wa_ref, wb_ref, wmix_ref, h_ref, p_ref, v_ref, gate_ref, gbias_ref) = refs
    aw, bw, dm = widths
    o_u, o_v, o_bg, o_cg, o_xin, o_ga, o_gb = (0, aw, 2 * aw, 2 * aw + bw, 2 * aw + 2 * bw,
                                               2 * aw + 3 * bw, 2 * aw + 3 * bw + dm)

    def col(o, w):
        return proj_ref[:, o:o + w].astype(F32)

    v = col(o_v, aw)
    mu = jnp.mean(v, axis=-1, keepdims=True)
    vc = v - mu
    var = jnp.mean(vc * vc, axis=-1, keepdims=True)
    vn = (vc * lax.rsqrt(var + EPS)) * lng_ref[...] + lnb_ref[...]
    if not prompt:
        v_ref[...] = vn
    vb = vn.astype(BF16)
    gd = aw // groups

    @pl.when(pl.program_id(0) == 0)
    def _():
        row = lax.broadcasted_iota(jnp.int32, (tm, tm), 0)
        cidx = lax.broadcasted_iota(jnp.int32, (tm, tm), 1)
        keep = ((row ^ cidx) < gate_len) & (cidx <= row)
        cs = wsp_ref.shape[1]
        pick = (lax.broadcasted_iota(jnp.int32, (tm, cs), 1)
                == (lax.broadcasted_iota(jnp.int32, (tm, cs), 0) & (gate_len - 1)))
        pick_b = jnp.where(pick, 1.0, 0.0).astype(BF16)
        for g in range(groups):
            w_rows = jnp.dot(pick_b, wsp_ref[g].astype(BF16), preferred_element_type=F32).astype(BF16)
            w_full = lax.dot_general(w_rows, pick_b, (((1,), (1,)), ((), ())), preferred_element_type=F32)
            gate_ref[g] = jnp.where(keep, w_full, 0.0).astype(BF16)
            gbias_ref[g] = jnp.sum(jnp.where(pick, bsp_ref[g:g + 1, :], 0.0), axis=1, keepdims=True)

    zs = [jnp.dot(gate_ref[g], vb[:, g * gd:(g + 1) * gd], preferred_element_type=F32) + gbias_ref[g]
          for g in range(groups)]
    z = jnp.concatenate(zs, axis=1)
    y_a = (col(o_u, aw) * z).astype(BF16)

    p = col(o_cg, bw) * col(o_xin, bw)
    trow = lax.broadcasted_iota(jnp.int32, (tm, bw), 0)
    if prompt:
        fresh = pl.program_id(0) % tiles_per_seq == 0
        pp = cgp_ref[...].astype(F32) * xinp_ref[...].astype(F32)
        pp = jnp.where(fresh, 0.0, pp)
        last1 = pp[-1:, :]
        last2 = pp[-2:-1, :]
        e1 = jnp.where(trow == 0, last1, 0.0)
        e2 = jnp.where(trow == 0, last2, jnp.where(trow == 1, last1, 0.0))
        tail_ref[...] = p[tm - 8:, :]
    else:
        trow = trow & (gate_len - 1)
        e1 = e1_ref[...]
        e2 = e2_ref[...]
        p_ref[...] = p
    s1 = jnp.where(trow >= 1, pltpu.roll(p, 1, 0), 0.0) + e1
    s2 = jnp.where(trow >= 2, pltpu.roll(p, 2, 0), 0.0) + e2
    conv = cw_ref[0:1, :] * s2 + cw_ref[1:2, :] * s1 + cw_ref[2:3, :] * p
    y_b = (col(o_bg, bw) * conv).astype(BF16)

    merged = (jax.nn.sigmoid(col(o_ga, dm)) * jnp.dot(y_a, wa_ref[...], preferred_element_type=F32)
              + jax.nn.sigmoid(col(o_gb, dm)) * jnp.dot(y_b, wb_ref[...], preferred_element_type=F32))
    h_ref[...] = x_ref[...] + jnp.dot(merged.astype(BF16), wmix_ref[...], preferred_element_type=F32)


def mix(x, proj, prev, ln_g, ln_b, w_sp, b_sp, conv_w, wa, wb, wmix, *, tm, gate_len, seq_len, name):
    t, dm = x.shape
    aw, bw = wa.shape[0], wb.shape[0]
    groups = w_sp.shape[0]
    prompt = prev is None
    nt = t // tm
    row_spec = lambda w: pl.BlockSpec((tm, w), lambda i: (i, 0))
    in_specs = [row_spec(dm), row_spec(proj.shape[1])]
    if prompt:
        prev_rows = 16
        cg_blk = (2 * aw + bw) // bw
        in_specs += [
            pl.BlockSpec((prev_rows, bw), lambda i: (jnp.maximum(i * (tm // prev_rows) - 1, 0), cg_blk)),
            pl.BlockSpec((prev_rows, bw), lambda i: (jnp.maximum(i * (tm // prev_rows) - 1, 0), cg_blk + 1)),
        ]
        extra = (proj, proj)
    else:
        in_specs += [row_spec(bw), row_spec(bw)]
        extra = prev
    in_specs += [_resident((1, aw)), _resident((1, aw)), _resident(w_sp.shape), _resident(b_sp.shape),
                 _resident(conv_w.shape), _resident(wa.shape), _resident(wb.shape), _resident(wmix.shape)]
    out_specs = [row_spec(dm)]
    out_shape = [jax.ShapeDtypeStruct((t, dm), F32)]
    if prompt:
        out_specs.append(pl.BlockSpec((8, bw), lambda i: (i, 0)))
        out_shape.append(jax.ShapeDtypeStruct((nt * 8, bw), F32))
    else:
        out_specs += [row_spec(bw), row_spec(aw)]
        out_shape += [jax.ShapeDtypeStruct((t, bw), F32), jax.ShapeDtypeStruct((t, aw), F32)]
    kern = functools.partial(_mix_kernel, tm=tm, gate_len=gate_len, widths=(aw, bw, dm), groups=groups,
                             prompt=prompt, tiles_per_seq=max(seq_len // tm, 1))
    return pl.pallas_call(
        kern, grid=(nt,), in_specs=in_specs, out_specs=out_specs, out_shape=out_shape,
        scratch_shapes=[pltpu.VMEM((groups, tm, tm), BF16), pltpu.VMEM((groups, tm, 1), F32)],
        compiler_params=_params("arbitrary"), name=name,
    )(x, proj, *extra, ln_g.reshape(1, aw), ln_b.reshape(1, aw), w_sp, b_sp, conv_w, wa, wb, wmix)


def _attn_block_kernel(h_ref, g_ref, wq_ref, k_ref, v_ref, wxo_ref, gn_ref, out_ref, hn_ref, *, heads, head_dim):
    scale = head_dim ** -0.5
    h = h_ref[...]
    q = jnp.dot(_rms(h, g_ref[...]).astype(BF16), wq_ref[...], preferred_element_type=F32).astype(BF16)
    outs = []
    for hd in range(heads):
        cols = slice(hd * head_dim, (hd + 1) * head_dim)
        k = k_ref[0, :, cols].astype(BF16)
        s = lax.dot_general(q[:, cols], k, (((1,), (1,)), ((), ())), preferred_element_type=F32) * scale
        e = jnp.exp(s - jnp.max(s, axis=-1, keepdims=True))
        p = (e / jnp.sum(e, axis=-1, keepdims=True)).astype(BF16)
        outs.append(jnp.dot(p, v_ref[0, :, cols].astype(BF16), preferred_element_type=F32).astype(BF16))
    o = jnp.concatenate(outs, axis=1)
    _residual_and_norm(h, o, wxo_ref, gn_ref, out_ref, hn_ref)


def _residual_and_norm(h, o, wxo_ref, gn_ref, out_ref, hn_ref):
    out = h + jnp.dot(o, wxo_ref[...], preferred_element_type=F32)
    out_ref[...] = out
    hn_ref[...] = _rms(out, gn_ref[...]).astype(BF16)


def attn_block(h, g, wq, k, v, wxo, g_next, *, heads, tm, seq_len, name):
    t, dm = h.shape
    m = k.shape[1]
    per_seq = seq_len // tm
    row_spec = pl.BlockSpec((tm, dm), lambda i: (i, 0))
    kv_spec = pl.BlockSpec((1, m, dm), lambda i: (i // per_seq, 0, 0))
    kern = functools.partial(_attn_block_kernel, heads=heads, head_dim=dm // heads)
    return pl.pallas_call(
        kern, grid=(t // tm,),
        in_specs=[row_spec, _resident((1, dm)), _resident(wq.shape), kv_spec, kv_spec, _resident(wxo.shape),
                  _resident((1, dm))],
        out_specs=[row_spec, row_spec],
        out_shape=[jax.ShapeDtypeStruct((t, dm), F32), jax.ShapeDtypeStruct((t, dm), BF16)],
        compiler_params=_params("parallel"), name=name,
    )(h, g.reshape(1, dm), wq, k, v, wxo, g_next.reshape(1, dm))


def _attn_rows_kernel(q_ref, k_ref, v_ref, o_ref, *, n_seq, rows, heads, head_dim, n_mem):
    scale = head_dim ** -0.5
    slabs = head_dim // LANES
    per_seq = n_mem * slabs * heads
    width = n_mem * slabs
    pairs = [(b, h) for b in range(n_seq) for h in range(heads)]
    lane_slab = lax.broadcasted_iota(jnp.int32, (rows, width), 1) & (slabs - 1)
    is_slab = [lane_slab == c for c in range(slabs)]

    parts = []
    for b, h in pairs:
        r0, c0 = b * rows, h * head_dim
        qh = jnp.concatenate(
            [q_ref[r0:r0 + rows, c0 + c * LANES:c0 + (c + 1) * LANES] for c in range(slabs)], axis=0
        ).astype(BF16)
        kh = k_ref[pl.ds(b * per_seq + h, width, stride=heads), :].astype(BF16)
        g = lax.dot_general(qh, kh, (((1,), (1,)), ((), ())), preferred_element_type=F32)
        s = g[0:rows]
        for c in range(1, slabs):
            s = jnp.where(is_slab[c], g[c * rows:(c + 1) * rows], s)
        parts.append(s)
    s = jnp.concatenate(parts, axis=0)

    lane = lax.broadcasted_iota(jnp.int32, s.shape, 1)
    step = 1
    while step < slabs:
        s = s + jnp.where((lane & step) != 0, pltpu.roll(s, step, 1), pltpu.roll(s, width - step, 1))
        step *= 2
    s = s * scale
    e = jnp.exp(s - jnp.max(s, axis=-1, keepdims=True))
    p = e / (jnp.sum(e, axis=-1, keepdims=True) * (1.0 / slabs))

    for i, (b, h) in enumerate(pairs):
        r0, c0 = b * rows, h * head_dim
        ph = p[i * rows:(i + 1) * rows]
        w = jnp.concatenate([jnp.where(is_slab[c], ph, 0.0) for c in range(slabs)], axis=0).astype(BF16)
        vh = v_ref[pl.ds(b * per_seq + h, width, stride=heads), :].astype(BF16)
        o = jnp.dot(w, vh, preferred_element_type=F32)
        for c in range(slabs):
            o_ref[r0:r0 + rows, c0 + c * LANES:c0 + (c + 1) * LANES] = o[c * rows:(c + 1) * rows]


def _rows_view(a):
    n, m, heads, hd = a.shape
    a = a.reshape(n * m, heads, hd // LANES, LANES)
    return a.transpose(0, 2, 1, 3).reshape(n * m * hd // LANES * heads, LANES)


def attn_rows(q, k, v, *, n_seq, name):
    n, n_mem, heads, hd = k.shape
    t, dm = q.shape
    rows = t // n
    slabs = hd // LANES
    assert slabs & (slabs - 1) == 0 and rows % 8 == 0
    per_seq = n_mem * slabs * heads
    kv_spec = pl.BlockSpec((n_seq * per_seq, LANES), lambda i: (i, 0))
    q_spec = pl.BlockSpec((n_seq * rows, dm), lambda i: (i, 0))
    kern = functools.partial(_attn_rows_kernel, n_seq=n_seq, rows=rows, heads=heads, head_dim=hd, n_mem=n_mem)
    return pl.pallas_call(
        kern, grid=(n // n_seq,),
        in_specs=[q_spec, kv_spec, kv_spec], out_specs=q_spec,
        out_shape=jax.ShapeDtypeStruct((t, dm), F32),
        compiler_params=_params("parallel"), name=name,
    )(q, _rows_view(k), _rows_view(v))


def _attn_out_kernel(h_ref, o_ref, wxo_ref, gn_ref, out_ref, hn_ref):
    _residual_and_norm(h_ref[...], o_ref[...].astype(BF16), wxo_ref, gn_ref, out_ref, hn_ref)


def attn_out(h, o, wxo, g_next, *, tm, name):
    t, dm = h.shape
    row_spec = pl.BlockSpec((tm, dm), lambda i: (i, 0))
    return pl.pallas_call(
        _attn_out_kernel, grid=(t // tm,),
        in_specs=[row_spec, row_spec, _resident(wxo.shape), _resident((1, dm))],
        out_specs=[row_spec, row_spec],
        out_shape=[jax.ShapeDtypeStruct((t, dm), F32), jax.ShapeDtypeStruct((t, dm), BF16)],
        compiler_params=_params("parallel"), name=name,
    )(h, o, wxo, g_next.reshape(1, dm))


def _mlp_kernel(hn_ref, h_hbm, wup_ref, wdn_ref, gf_ref, y_ref, sem, *, tm):
    i, j = pl.program_id(0), pl.program_id(1)
    residual_copy = pltpu.make_async_copy(h_hbm.at[pl.ds(i * tm, tm), :], y_ref, sem.at[0])

    def up():
        a = jnp.dot(hn_ref[...], wup_ref[...], preferred_element_type=F32)
        return jnp.square(jnp.maximum(a, 0.0)).astype(BF16)

    def accumulate(a):
        y_ref[...] += jnp.dot(a, wdn_ref[...], preferred_element_type=F32)

    @pl.when(j == 0)
    def _():
        residual_copy.start()
        a = up()
        residual_copy.wait()
        accumulate(a)

    @pl.when(j > 0)
    def _():
        accumulate(up())

    @pl.when(j == pl.num_programs(1) - 1)
    def _():
        y_ref[...] = _rms(y_ref[...], gf_ref[...])


def mlp(hn, h, w_up, w_down, g_final, *, tm, fc, name):
    t, dm = h.shape
    dff = w_up.shape[1]
    return pl.pallas_call(
        functools.partial(_mlp_kernel, tm=tm), grid=(t // tm, dff // fc),
        in_specs=[pl.BlockSpec((tm, dm), lambda i, j: (i, 0)),
                  pl.BlockSpec(memory_space=pl.ANY),
                  pl.BlockSpec((dm, fc), lambda i, j: (0, j)),
                  pl.BlockSpec((fc, dm), lambda i, j: (j, 0)),
                  pl.BlockSpec((1, dm), lambda i, j: (0, 0))],
        out_specs=pl.BlockSpec((tm, dm), lambda i, j: (i, 0)),
        out_shape=jax.ShapeDtypeStruct((t, dm), F32),
        scratch_shapes=[pltpu.SemaphoreType.DMA((1,))],
        compiler_params=_params("arbitrary", "arbitrary"), name=name,
    )(hn, h, w_up, w_down, g_final.reshape(1, dm))


MIX_TM = 256


ROW_TM = 1024
COL_TN = 1024
ATTN_TM = 512
MLP_FC = 1024


def _gate_len(seq_len, w_spatial):
    gate_len = CHUNK if seq_len % CHUNK == 0 else seq_len
    assert gate_len & (gate_len - 1) == 0 and MIX_TM % gate_len == 0 and gate_len <= w_spatial.shape[1]
    return gate_len


def kernel(x_prompt, x_sample, state_conv, cache_mem_k, cache_mem_v, mem_prompt, norm_mix_g, w_in, ln_v_g, ln_v_b, w_spatial, b_spatial, conv_w, w_branch_a, w_branch_b, w_mix_out, norm_x_g, norm_mem_g, w_q, w_k, w_v, w_x_out, norm_mlp_g, w_up, w_down, norm_final_g):
    depth = w_in.shape[0]
    assert depth == 1, "the final rmsnorm is fused into the single layer's MLP kernel"
    nb, seq, dm = x_prompt.shape
    nd, dseq, _ = x_sample.shape
    n_mem, heads, hd = cache_mem_k.shape[2:]
    l = 0
    xs = x_sample.reshape(nd * dseq, dm)
    xp = x_prompt.reshape(nb * seq, dm)
    assert xs.shape[0] == ROW_TM

    proj_s, w_in_b = rms_matmul(xs, norm_mix_g[l], w_in[l], BF16, tm=ROW_TM, tn=COL_TN, emit_w=True,
                                name="in_proj_sample")
    proj_p, w_a, w_b, w_mix, w_q_b, w_xo, w_up_b, w_down_b = rms_matmul(
        xp, norm_mix_g[l], w_in_b, BF16, tm=ROW_TM, tn=COL_TN, name="in_proj_prompt",
        side_cast=(w_branch_a[l], w_branch_b[l], w_mix_out[l], w_q[l], w_x_out[l], w_up[l], w_down[l]))
    mix_w = (ln_v_g[l], ln_v_b[l], w_spatial[l], b_spatial[l], conv_w[l], w_a, w_b, w_mix)

    prev = state_conv[l]
    e1 = jnp.pad(prev[:, 1:2], ((0, 0), (0, dseq - 1), (0, 0))).reshape(nd * dseq, -1)
    e2 = jnp.pad(prev, ((0, 0), (0, dseq - 2), (0, 0))).reshape(nd * dseq, -1)
    h_s, p_s, vn_s = mix(xs, proj_s, (e1, e2), *mix_w, tm=MIX_TM, gate_len=_gate_len(dseq, w_spatial[l]),
                         seq_len=dseq, name="mix_sample")
    q_s = rms_matmul(h_s, norm_x_g[l], w_q_b, F32, tm=ROW_TM, tn=COL_TN, name="q_proj_sample")
    o_s = attn_rows(q_s, cache_mem_k[l], cache_mem_v[l], n_seq=4, name="attn_sample")
    h_s, hn_s = attn_out(h_s, o_s, w_xo, norm_mlp_g[l], tm=ATTN_TM, name="x_out_sample")
    y_s = mlp(hn_s, h_s, w_up_b, w_down_b, norm_final_g, tm=ROW_TM, fc=MLP_FC, name="mlp_sample")

    mem = mem_prompt.reshape(nb * n_mem, dm)
    k_b, v_b, k_rows, v_rows = mem_kv(mem, norm_mem_g[l], w_k[l], w_v[l], heads=heads, name="mem_kv")
    h_p, tail_p = mix(xp, proj_p, None, *mix_w, tm=MIX_TM, gate_len=_gate_len(seq, w_spatial[l]),
                      seq_len=seq, name="mix_prompt")
    h_p, hn_p = attn_block(h_p, norm_x_g[l], w_q_b, k_b.reshape(nb, n_mem, dm), v_b.reshape(nb, n_mem, dm), w_xo,
                           norm_mlp_g[l], heads=heads, tm=ATTN_TM, seq_len=seq, name="attn_block_prompt")
    y_p = mlp(hn_p, h_p, w_up_b, w_down_b, norm_final_g, tm=ROW_TM, fc=MLP_FC, name="mlp_prompt")
    k_p = _cache_view(k_rows, nb, n_mem, heads, hd)
    v_p = _cache_view(v_rows, nb, n_mem, heads, hd)

    bw = p_s.shape[1]
    keep = conv_w.shape[1] - 1
    conv_p = tail_p.reshape(nb, seq // MIX_TM, 8, bw)[:, -1, 8 - keep:, :]
    conv_s = p_s.reshape(nd, dseq, bw)[:, dseq - keep:, :]
    return (y_p.reshape(nb, seq, dm), y_s.reshape(nd, dseq, dm), k_p[None], v_p[None],
            conv_p[None], conv_s[None], vn_s.reshape(1, nd, dseq, -1))
```

```python
import functools

import jax
import jax.numpy as jnp
from jax import lax
from jax.experimental import pallas as pl
from jax.experimental.pallas import tpu as pltpu

EPS = 1e-6
CHUNK = 128
F32 = jnp.float32
BF16 = jnp.bfloat16

V7X_VMEM_BYTES = 64 * 1024 * 1024
VMEM_LIMIT = V7X_VMEM_BYTES * 7 // 8


def _params(*sem):
    return pltpu.CompilerParams(dimension_semantics=sem, vmem_limit_bytes=VMEM_LIMIT)


def _rms(x, g):
    r = lax.rsqrt(jnp.mean(x * x, axis=-1, keepdims=True) + EPS)
    return (x * r) * g


def _resident(shape):
    return pl.BlockSpec(shape, lambda *_: (0,) * len(shape), pipeline_mode=pl.Buffered(1))


def _bf16_tile(w_ref, wb_ref):
    if wb_ref is None:
        return w_ref[...].astype(BF16)
    wb_ref[...] = w_ref[...].astype(BF16)
    return wb_ref[...]


def _rms_matmul_kernel(x_ref, g_ref, w_ref, *rest, emit_w, n_side):
    side_in, rest = rest[:n_side], rest[n_side:]
    o_ref, rest = rest[0], rest[1:]
    wb_ref, rest = (rest[0], rest[1:]) if emit_w else (None, rest)
    side_out, xn_ref = rest[:n_side], rest[n_side]

    @pl.when(pl.program_id(1) == 0)
    def _():
        xn_ref[...] = _rms(x_ref[...], g_ref[...]).astype(BF16)

    w = _bf16_tile(w_ref, wb_ref)
    o_ref[...] = jnp.dot(xn_ref[...], w, preferred_element_type=F32).astype(o_ref.dtype)
    for src, dst in zip(side_in, side_out):
        dst[...] = src[...].astype(BF16)


BF16_SUBLANES = 16


def _side_chunks(rows, steps):
    return max(n for n in range(1, steps + 1) if rows % n == 0 and (rows // n) % BF16_SUBLANES == 0)


def rms_matmul(x, g, w, out_dtype, *, tm, tn, name, emit_w=False, side_cast=()):
    t, d = x.shape
    n = w.shape[1]
    one_tile = t == tm
    assert one_tile or not emit_w
    nj = n // tn
    steps = (t // tm) * nj
    x_spec = (pl.BlockSpec((tm, d), lambda i, j: (i, 0), pipeline_mode=pl.Buffered(1)) if one_tile
              else pl.BlockSpec((tm, d), lambda i, j: (i, 0)))
    w_spec = pl.BlockSpec((d, tn), lambda i, j: (0, j))
    out_specs = [pl.BlockSpec((tm, tn), lambda i, j: (i, j))]
    out_shape = [jax.ShapeDtypeStruct((t, n), out_dtype)]
    if emit_w:
        out_specs.append(w_spec)
        out_shape.append(jax.ShapeDtypeStruct(w.shape, BF16))
    side_specs = []
    for s in side_cast:
        chunks = _side_chunks(s.shape[0], steps)
        side_specs.append(pl.BlockSpec((s.shape[0] // chunks, s.shape[1]),
                                       lambda i, j, c=chunks: (jnp.minimum(i * nj + j, c - 1), 0)))
        out_shape.append(jax.ShapeDtypeStruct(s.shape, BF16))
    outs = pl.pallas_call(
        functools.partial(_rms_matmul_kernel, emit_w=emit_w, n_side=len(side_cast)),
        grid=(t // tm, nj),
        in_specs=[x_spec, pl.BlockSpec((1, d), lambda i, j: (0, 0)), w_spec] + side_specs,
        out_specs=out_specs + side_specs, out_shape=out_shape,
        scratch_shapes=[pltpu.VMEM((tm, d), BF16)],
        compiler_params=_params("arbitrary", "arbitrary") if side_cast else _params("parallel", "arbitrary"),
        name=name,
    )(x, g.reshape(1, d), w, *side_cast)
    return outs if len(outs) > 1 else outs[0]


LANES = 128


def _mem_kv_kernel(x_ref, g_ref, wk_ref, wv_ref, k_ref, v_ref, kc_ref, vc_ref, xn_ref, *, heads):
    head = pl.program_id(0)

    @pl.when(head == 0)
    def _():
        xn_ref[...] = _rms(x_ref[...], g_ref[...]).astype(BF16)

    rows = x_ref.shape[0]
    slabs = wk_ref.shape[1] // LANES
    for w_ref, rows_ref, cache_ref in ((wk_ref, k_ref, kc_ref), (wv_ref, v_ref, vc_ref)):
        o = jnp.dot(xn_ref[...], w_ref[...].astype(BF16), preferred_element_type=F32)
        rows_ref[...] = o.astype(rows_ref.dtype)
        for c in range(slabs):
            cache_ref[pl.ds(c * heads + head, rows, stride=slabs * heads), :] = o[:, c * LANES:(c + 1) * LANES]


def mem_kv(mem, g, w_k, w_v, *, heads, name):
    t, d = mem.shape
    hd = w_k.shape[1] // heads
    once = dict(pipeline_mode=pl.Buffered(1))
    w_spec = pl.BlockSpec((d, hd), lambda j: (0, j))
    rows_spec = pl.BlockSpec((t, hd), lambda j: (0, j))
    cache_spec = pl.BlockSpec((t * d // LANES, LANES), lambda j: (0, 0), **once)
    return pl.pallas_call(
        functools.partial(_mem_kv_kernel, heads=heads), grid=(heads,),
        in_specs=[pl.BlockSpec((t, d), lambda j: (0, 0), **once), pl.BlockSpec((1, d), lambda j: (0, 0)),
                  w_spec, w_spec],
        out_specs=[rows_spec, rows_spec, cache_spec, cache_spec],
        out_shape=[jax.ShapeDtypeStruct((t, d), BF16)] * 2 + [jax.ShapeDtypeStruct((t * d // LANES, LANES), F32)] * 2,
        scratch_shapes=[pltpu.VMEM((t, d), BF16)],
        compiler_params=_params("arbitrary"), name=name,
    )(mem, g.reshape(1, d), w_k, w_v)


def _cache_view(rows, n, m, heads, hd):
    a = rows.reshape(n * m, hd // LANES, heads, LANES)
    return a.transpose(0, 2, 1, 3).reshape(n, m, heads, hd)


def _mix_kernel(*refs, tm, gate_len, widths, groups, prompt, tiles_per_seq):
    if prompt:
        (x_ref, proj_ref, cgp_ref, xinp_ref, lng_ref, lnb_ref, wsp_ref, bsp_ref, cw_ref,
         wa_ref, wb_ref, wmix_ref, h_ref, tail_ref, gate_ref, gbias_ref) = refs
    else:
        (x_ref, proj_ref, e1_ref, e2_ref, lng_ref, lnb_ref, wsp_ref, bsp_ref, cw_ref,
         wa_ref, wb_ref, wmix_ref, h_ref, p_ref, v_ref, gate_ref, gbias_ref) = refs
    aw, bw, dm = widths
    o_u, o_v, o_bg, o_cg, o_xin, o_ga, o_gb = (0, aw, 2 * aw, 2 * aw + bw, 2 * aw + 2 * bw,
                                               2 * aw + 3 * bw, 2 * aw + 3 * bw + dm)

    def col(o, w):
        return proj_ref[:, o:o + w].astype(F32)

    v = col(o_v, aw)
    mu = jnp.mean(v, axis=-1, keepdims=True)
    vc = v - mu
    var = jnp.mean(vc * vc, axis=-1, keepdims=True)
    vn = (vc * lax.rsqrt(var + EPS)) * lng_ref[...] + lnb_ref[...]
    if not prompt:
        v_ref[...] = vn
    vb = vn.astype(BF16)
    gd = aw // groups

    @pl.when(pl.program_id(0) == 0)
    def _():
        row = lax.broadcasted_iota(jnp.int32, (tm, tm), 0)
        cidx = lax.broadcasted_iota(jnp.int32, (tm, tm), 1)
        keep = ((row ^ cidx) < gate_len) & (cidx <= row)
        cs = wsp_ref.shape[1]
        pick = (lax.broadcasted_iota(jnp.int32, (tm, cs), 1)
                == (lax.broadcasted_iota(jnp.int32, (tm, cs), 0) & (gate_len - 1)))
        pick_b = jnp.where(pick, 1.0, 0.0).astype(BF16)
        for g in range(groups):
            w_rows = jnp.dot(pick_b, wsp_ref[g].astype(BF16), preferred_element_type=F32).astype(BF16)
            w_full = lax.dot_general(w_rows, pick_b, (((1,), (1,)), ((), ())), preferred_element_type=F32)
            gate_ref[g] = jnp.where(keep, w_full, 0.0).astype(BF16)
            gbias_ref[g] = jnp.sum(jnp.where(pick, bsp_ref[g:g + 1, :], 0.0), axis=1, keepdims=True)

    zs = [jnp.dot(gate_ref[g], vb[:, g * gd:(g + 1) * gd], preferred_element_type=F32) + gbias_ref[g]
          for g in range(groups)]
    z = jnp.concatenate(zs, axis=1)
    y_a = (col(o_u, aw) * z).astype(BF16)

    p = col(o_cg, bw) * col(o_xin, bw)
    trow = lax.broadcasted_iota(jnp.int32, (tm, bw), 0)
    if prompt:
        fresh = pl.program_id(0) % tiles_per_seq == 0
        pp = cgp_ref[...].astype(F32) * xinp_ref[...].astype(F32)
        pp = jnp.where(fresh, 0.0, pp)
        last1 = pp[-1:, :]
        last2 = pp[-2:-1, :]
        e1 = jnp.where(trow == 0, last1, 0.0)
        e2 = jnp.where(trow == 0, last2, jnp.where(trow == 1, last1, 0.0))
        tail_ref[...] = p[tm - 8:, :]
    else:
        trow = trow & (gate_len - 1)
        e1 = e1_ref[...]
        e2 = e2_ref[...]
        p_ref[...] = p
    s1 = jnp.where(trow >= 1, pltpu.roll(p, 1, 0), 0.0) + e1
    s2 = jnp.where(trow >= 2, pltpu.roll(p, 2, 0), 0.0) + e2
    conv = cw_ref[0:1, :] * s2 + cw_ref[1:2, :] * s1 + cw_ref[2:3, :] * p
    y_b = (col(o_bg, bw) * conv).astype(BF16)

    merged = (jax.nn.sigmoid(col(o_ga, dm)) * jnp.dot(y_a, wa_ref[...], preferred_element_type=F32)
              + jax.nn.sigmoid(col(o_gb, dm)) * jnp.dot(y_b, wb_ref[...], preferred_element_type=F32))
    h_ref[...] = x_ref[...] + jnp.dot(merged.astype(BF16), wmix_ref[...], preferred_element_type=F32)


def mix(x, proj, prev, ln_g, ln_b, w_sp, b_sp, conv_w, wa, wb, wmix, *, tm, gate_len, seq_len, name):
    t, dm = x.shape
    aw, bw = wa.shape[0], wb.shape[0]
    groups = w_sp.shape[0]
    prompt = prev is None
    nt = t // tm
    row_spec = lambda w: pl.BlockSpec((tm, w), lambda i: (i, 0))
    in_specs = [row_spec(dm), row_spec(proj.shape[1])]
    if prompt:
        prev_rows = 16
        cg_blk = (2 * aw + bw) // bw
        in_specs += [
            pl.BlockSpec((prev_rows, bw), lambda i: (jnp.maximum(i * (tm // prev_rows) - 1, 0), cg_blk)),
            pl.BlockSpec((prev_rows, bw), lambda i: (jnp.maximum(i * (tm // prev_rows) - 1, 0), cg_blk + 1)),
        ]
        extra = (proj, proj)
    else:
        in_specs += [row_spec(bw), row_spec(bw)]
        extra = prev
    in_specs += [_resident((1, aw)), _resident((1, aw)), _resident(w_sp.shape), _resident(b_sp.shape),
                 _resident(conv_w.shape), _resident(wa.shape), _resident(wb.shape), _resident(wmix.shape)]
    out_specs = [row_spec(dm)]
    out_shape = [jax.ShapeDtypeStruct((t, dm), F32)]
    if prompt:
        out_specs.append(pl.BlockSpec((8, bw), lambda i: (i, 0)))
        out_shape.append(jax.ShapeDtypeStruct((nt * 8, bw), F32))
    else:
        out_specs += [row_spec(bw), row_spec(aw)]
        out_shape += [jax.ShapeDtypeStruct((t, bw), F32), jax.ShapeDtypeStruct((t, aw), F32)]
    kern = functools.partial(_mix_kernel, tm=tm, gate_len=gate_len, widths=(aw, bw, dm), groups=groups,
                             prompt=prompt, tiles_per_seq=max(seq_len // tm, 1))
    return pl.pallas_call(
        kern, grid=(nt,), in_specs=in_specs, out_specs=out_specs, out_shape=out_shape,
        scratch_shapes=[pltpu.VMEM((groups, tm, tm), BF16), pltpu.VMEM((groups, tm, 1), F32)],
        compiler_params=_params("arbitrary"), name=name,
    )(x, proj, *extra, ln_g.reshape(1, aw), ln_b.reshape(1, aw), w_sp, b_sp, conv_w, wa, wb, wmix)


def _attn_block_kernel(h_ref, g_ref, wq_ref, k_ref, v_ref, wxo_ref, gn_ref, out_ref, hn_ref, *, heads, head_dim):
    scale = head_dim ** -0.5
    h = h_ref[...]
    q = jnp.dot(_rms(h, g_ref[...]).astype(BF16), wq_ref[...], preferred_element_type=F32).astype(BF16)
    outs = []
    for hd in range(heads):
        cols = slice(hd * head_dim, (hd + 1) * head_dim)
        k = k_ref[0, :, cols].astype(BF16)
        s = lax.dot_general(q[:, cols], k, (((1,), (1,)), ((), ())), preferred_element_type=F32) * scale
        e = jnp.exp(s - jnp.max(s, axis=-1, keepdims=True))
        p = (e / jnp.sum(e, axis=-1, keepdims=True)).astype(BF16)
        outs.append(jnp.dot(p, v_ref[0, :, cols].astype(BF16), preferred_element_type=F32).astype(BF16))
    o = jnp.concatenate(outs, axis=1)
    _residual_and_norm(h, o, wxo_ref, gn_ref, out_ref, hn_ref)


def _residual_and_norm(h, o, wxo_ref, gn_ref, out_ref, hn_ref):
    out = h + jnp.dot(o, wxo_ref[...], preferred_element_type=F32)
    out_ref[...] = out
    hn_ref[...] = _rms(out, gn_ref[...]).astype(BF16)


def attn_block(h, g, wq, k, v, wxo, g_next, *, heads, tm, seq_len, name):
    t, dm = h.shape
    m = k.shape[1]
    per_seq = seq_len // tm
    row_spec = pl.BlockSpec((tm, dm), lambda i: (i, 0))
    kv_spec = pl.BlockSpec((1, m, dm), lambda i: (i // per_seq, 0, 0))
    kern = functools.partial(_attn_block_kernel, heads=heads, head_dim=dm // heads)
    return pl.pallas_call(
        kern, grid=(t // tm,),
        in_specs=[row_spec, _resident((1, dm)), _resident(wq.shape), kv_spec, kv_spec, _resident(wxo.shape),
                  _resident((1, dm))],
        out_specs=[row_spec, row_spec],
        out_shape=[jax.ShapeDtypeStruct((t, dm), F32), jax.ShapeDtypeStruct((t, dm), BF16)],
        compiler_params=_params("parallel"), name=name,
    )(h, g.reshape(1, dm), wq, k, v, wxo, g_next.reshape(1, dm))


def _attn_rows_kernel(q_ref, k_ref, v_ref, o_ref, *, n_seq, rows, heads, head_dim, n_mem):
    scale = head_dim ** -0.5
    slabs = head_dim // LANES
    per_seq = n_mem * slabs * heads
    width = n_mem * slabs
    pairs = [(b, h) for b in range(n_seq) for h in range(heads)]
    lane_slab = lax.broadcasted_iota(jnp.int32, (rows, width), 1) & (slabs - 1)
    is_slab = [lane_slab == c for c in range(slabs)]

    parts = []
    for b, h in pairs:
        r0, c0 = b * rows, h * head_dim
        qh = jnp.concatenate(
            [q_ref[r0:r0 + rows, c0 + c * LANES:c0 + (c + 1) * LANES] for c in range(slabs)], axis=0
        ).astype(BF16)
        kh = k_ref[pl.ds(b * per_seq + h, width, stride=heads), :].astype(BF16)
        g = lax.dot_general(qh, kh, (((1,), (1,)), ((), ())), preferred_element_type=F32)
        s = g[0:rows]
        for c in range(1, slabs):
            s = jnp.where(is_slab[c], g[c * rows:(c + 1) * rows], s)
        parts.append(s)
    s = jnp.concatenate(parts, axis=0)

    lane = lax.broadcasted_iota(jnp.int32, s.shape, 1)
    step = 1
    while step < slabs:
        s = s + jnp.where((lane & step) != 0, pltpu.roll(s, step, 1), pltpu.roll(s, width - step, 1))
        step *= 2
    s = s * scale
    e = jnp.exp(s - jnp.max(s, axis=-1, keepdims=True))
    p = e / (jnp.sum(e, axis=-1, keepdims=True) * (1.0 / slabs))

    for i, (b, h) in enumerate(pairs):
        r0, c0 = b * rows, h * head_dim
        ph = p[i * rows:(i + 1) * rows]
        w = jnp.concatenate([jnp.where(is_slab[c], ph, 0.0) for c in range(slabs)], axis=0).astype(BF16)
        vh = v_ref[pl.ds(b * per_seq + h, width, stride=heads), :].astype(BF16)
        o = jnp.dot(w, vh, preferred_element_type=F32)
        for c in range(slabs):
            o_ref[r0:r0 + rows, c0 + c * LANES:c0 + (c + 1) * LANES] = o[c * rows:(c + 1) * rows]


def _rows_view(a):
    n, m, heads, hd = a.shape
    a = a.reshape(n * m, heads, hd // LANES, LANES)
    return a.transpose(0, 2, 1, 3).reshape(n * m * hd // LANES * heads, LANES)


def attn_rows(q, k, v, *, n_seq, name):
    n, n_mem, heads, hd = k.shape
    t, dm = q.shape
    rows = t // n
    slabs = hd // LANES
    assert slabs & (slabs - 1) == 0 and rows % 8 == 0
    per_seq = n_mem * slabs * heads
    kv_spec = pl.BlockSpec((n_seq * per_seq, LANES), lambda i: (i, 0))
    q_spec = pl.BlockSpec((n_seq * rows, dm), lambda i: (i, 0))
    kern = functools.partial(_attn_rows_kernel, n_seq=n_seq, rows=rows, heads=heads, head_dim=hd, n_mem=n_mem)
    return pl.pallas_call(
        kern, grid=(n // n_seq,),
        in_specs=[q_spec, kv_spec, kv_spec], out_specs=q_spec,
        out_shape=jax.ShapeDtypeStruct((t, dm), F32),
        compiler_params=_params("parallel"), name=name,
    )(q, _rows_view(k), _rows_view(v))


def _attn_out_kernel(h_ref, o_ref, wxo_ref, gn_ref, out_ref, hn_ref):
    _residual_and_norm(h_ref[...], o_ref[...].astype(BF16), wxo_ref, gn_ref, out_ref, hn_ref)


def attn_out(h, o, wxo, g_next, *, tm, name):
    t, dm = h.shape
    row_spec = pl.BlockSpec((tm, dm), lambda i: (i, 0))
    return pl.pallas_call(
        _attn_out_kernel, grid=(t // tm,),
        in_specs=[row_spec, row_spec, _resident(wxo.shape), _resident((1, dm))],
        out_specs=[row_spec, row_spec],
        out_shape=[jax.ShapeDtypeStruct((t, dm), F32), jax.ShapeDtypeStruct((t, dm), BF16)],
        compiler_params=_params("parallel"), name=name,
    )(h, o, wxo, g_next.reshape(1, dm))


def _mlp_kernel(hn_ref, h_hbm, wup_ref, wdn_ref, gf_ref, y_ref, hbuf, sem, *, tm):
    i, j = pl.program_id(0), pl.program_id(1)

    def residual_rows(tile):
        return pltpu.make_async_copy(h_hbm.at[pl.ds(tile * tm, tm), :], hbuf, sem.at[0])

    def up():
        a = jnp.dot(hn_ref[...], wup_ref[...], preferred_element_type=F32)
        return jnp.square(jnp.maximum(a, 0.0)).astype(BF16)

    @pl.when((i == 0) & (j == 0))
    def _():
        residual_rows(0).start()

    @pl.when(j == 0)
    def _():
        a = up()
        residual_rows(i).wait()
        y_ref[...] = hbuf[...] + jnp.dot(a, wdn_ref[...], preferred_element_type=F32)

        @pl.when(i + 1 < pl.num_programs(0))
        def _():
            residual_rows(i + 1).start()

    @pl.when(j > 0)
    def _():
        y_ref[...] += jnp.dot(up(), wdn_ref[...], preferred_element_type=F32)

    @pl.when(j == pl.num_programs(1) - 1)
    def _():
        y_ref[...] = _rms(y_ref[...], gf_ref[...])


def mlp(hn, h, w_up, w_down, g_final, *, tm, fc, name):
    t, dm = h.shape
    dff = w_up.shape[1]
    return pl.pallas_call(
        functools.partial(_mlp_kernel, tm=tm), grid=(t // tm, dff // fc),
        in_specs=[pl.BlockSpec((tm, dm), lambda i, j: (i, 0)),
                  pl.BlockSpec(memory_space=pl.ANY),
                  pl.BlockSpec((dm, fc), lambda i, j: (0, j)),
                  pl.BlockSpec((fc, dm), lambda i, j: (j, 0)),
                  pl.BlockSpec((1, dm), lambda i, j: (0, 0))],
        out_specs=pl.BlockSpec((tm, dm), lambda i, j: (i, 0)),
        out_shape=jax.ShapeDtypeStruct((t, dm), F32),
        scratch_shapes=[pltpu.VMEM((tm, dm), F32), pltpu.SemaphoreType.DMA((1,))],
        compiler_params=_params("arbitrary", "arbitrary"), name=name,
    )(hn, h, w_up, w_down, g_final.reshape(1, dm))


MIX_TM = 256
ROW_TM = 1024
COL_TN = 1024
ATTN_TM = 512
MLP_FC = 1024


def _gate_len(seq_len, w_spatial):
    gate_len = CHUNK if seq_len % CHUNK == 0 else seq_len
    assert gate_len & (gate_len - 1) == 0 and MIX_TM % gate_len == 0 and gate_len <= w_spatial.shape[1]
    return gate_len


def kernel(x_prompt, x_sample, state_conv, cache_mem_k, cache_mem_v, mem_prompt, norm_mix_g, w_in, ln_v_g, ln_v_b, w_spatial, b_spatial, conv_w, w_branch_a, w_branch_b, w_mix_out, norm_x_g, norm_mem_g, w_q, w_k, w_v, w_x_out, norm_mlp_g, w_up, w_down, norm_final_g):
    depth = w_in.shape[0]
    assert depth == 1, "the final rmsnorm is fused into the single layer's MLP kernel"
    nb, seq, dm = x_prompt.shape
    nd, dseq, _ = x_sample.shape
    n_mem, heads, hd = cache_mem_k.shape[2:]
    l = 0
    xs = x_sample.reshape(nd * dseq, dm)
    xp = x_prompt.reshape(nb * seq, dm)
    assert xs.shape[0] == ROW_TM

    proj_s, w_in_b = rms_matmul(xs, norm_mix_g[l], w_in[l], BF16, tm=ROW_TM, tn=COL_TN, emit_w=True,
                                name="in_proj_sample")
    proj_p, w_a, w_b, w_mix, w_q_b, w_xo, w_up_b, w_down_b = rms_matmul(
        xp, norm_mix_g[l], w_in_b, BF16, tm=ROW_TM, tn=COL_TN, name="in_proj_prompt",
        side_cast=(w_branch_a[l], w_branch_b[l], w_mix_out[l], w_q[l], w_x_out[l], w_up[l], w_down[l]))
    mix_w = (ln_v_g[l], ln_v_b[l], w_spatial[l], b_spatial[l], conv_w[l], w_a, w_b, w_mix)

    prev = state_conv[l]
    e1 = jnp.pad(prev[:, 1:2], ((0, 0), (0, dseq - 1), (0, 0))).reshape(nd * dseq, -1)
    e2 = jnp.pad(prev, ((0, 0), (0, dseq - 2), (0, 0))).reshape(nd * dseq, -1)
    h_s, p_s, vn_s = mix(xs, proj_s, (e1, e2), *mix_w, tm=MIX_TM, gate_len=_gate_len(dseq, w_spatial[l]),
                         seq_len=dseq, name="mix_sample")
    q_s = rms_matmul(h_s, norm_x_g[l], w_q_b, F32, tm=ROW_TM, tn=COL_TN, name="q_proj_sample")
    o_s = attn_rows(q_s, cache_mem_k[l], cache_mem_v[l], n_seq=4, name="attn_sample")
    h_s, hn_s = attn_out(h_s, o_s, w_xo, norm_mlp_g[l], tm=ATTN_TM, name="x_out_sample")
    y_s = mlp(hn_s, h_s, w_up_b, w_down_b, norm_final_g, tm=ROW_TM, fc=MLP_FC, name="mlp_sample")

    mem = mem_prompt.reshape(nb * n_mem, dm)
    k_b, v_b, k_rows, v_rows = mem_kv(mem, norm_mem_g[l], w_k[l], w_v[l], heads=heads, name="mem_kv")
    h_p, tail_p = mix(xp, proj_p, None, *mix_w, tm=MIX_TM, gate_len=_gate_len(seq, w_spatial[l]),
                      seq_len=seq, name="mix_prompt")
    h_p, hn_p = attn_block(h_p, norm_x_g[l], w_q_b, k_b.reshape(nb, n_mem, dm), v_b.reshape(nb, n_mem, dm), w_xo,
                           norm_mlp_g[l], heads=heads, tm=ATTN_TM, seq_len=seq, name="attn_block_prompt")
    y_p = mlp(hn_p, h_p, w_up_b, w_down_b, norm_final_g, tm=ROW_TM, fc=MLP_FC, name="mlp_prompt")
    k_p = _cache_view(k_rows, nb, n_mem, heads, hd)
    v_p = _cache_view(v_rows, nb, n_mem, heads, hd)

    bw = p_s.shape[1]
    keep = conv_w.shape[1] - 1
    conv_p = tail_p.reshape(nb, seq // MIX_TM, 8, bw)[:, -1, 8 - keep:, :]
    conv_s = p_s.reshape(nd, dseq, bw)[:, dseq - keep:, :]
    return (y_p.reshape(nb, seq, dm), y_s.reshape(nd, dseq, dm), k_p[None], v_p[None],
            conv_p[None], conv_s[None], vn_s.reshape(1, nd, dseq, -1))
```

```python
import functools

import jax
import jax.numpy as jnp
from jax import lax
from jax.experimental import pallas as pl
from jax.experimental.pallas import tpu as pltpu

EPS = 1e-6
CHUNK = 128
F32 = jnp.float32
BF16 = jnp.bfloat16

V7X_VMEM_BYTES = 64 * 1024 * 1024
VMEM_LIMIT = V7X_VMEM_BYTES * 7 // 8


def _params(*sem):
    return pltpu.CompilerParams(dimension_semantics=sem, vmem_limit_bytes=VMEM_LIMIT)


def _rms(x, g):
    r = lax.rsqrt(jnp.mean(x * x, axis=-1, keepdims=True) + EPS)
    return (x * r) * g


def _resident(shape):
    return pl.BlockSpec(shape, lambda *_: (0,) * len(shape), pipeline_mode=pl.Buffered(1))


def _bf16_tile(w_ref, wb_ref):
    if wb_ref is None:
        return w_ref[...].astype(BF16)
    wb_ref[...] = w_ref[...].astype(BF16)
    return wb_ref[...]


def _rms_matmul_kernel(x_hbm, g_ref, w_ref, *rest, emit_w, n_side):
    side_in, rest = rest[:n_side], rest[n_side:]
    o_ref, rest = rest[0], rest[1:]
    wb_ref, rest = (rest[0], rest[1:]) if emit_w else (None, rest)
    side_out, (xn_ref, xbuf, sem) = rest[:n_side], rest[n_side:]
    i, j = pl.program_id(0), pl.program_id(1)
    tm = xbuf.shape[0]

    def x_rows(tile):
        return pltpu.make_async_copy(x_hbm.at[pl.ds(tile * tm, tm), :], xbuf, sem.at[0])

    @pl.when((i == 0) & (j == 0))
    def _():
        x_rows(0).start()

    @pl.when(j == 0)
    def _():
        x_rows(i).wait()
        xn_ref[...] = _rms(xbuf[...], g_ref[...]).astype(BF16)

        @pl.when(i + 1 < pl.num_programs(0))
        def _():
            x_rows(i + 1).start()

    w = _bf16_tile(w_ref, wb_ref)
    o_ref[...] = jnp.dot(xn_ref[...], w, preferred_element_type=F32).astype(o_ref.dtype)
    for src, dst in zip(side_in, side_out):
        dst[...] = src[...].astype(BF16)


BF16_SUBLANES = 16


def _side_chunks(rows, steps):
    return max(n for n in range(1, steps + 1) if rows % n == 0 and (rows // n) % BF16_SUBLANES == 0)


def rms_matmul(x, g, w, out_dtype, *, tm, tn, name, emit_w=False, side_cast=()):
    t, d = x.shape
    n = w.shape[1]
    assert t == tm or not emit_w
    nj = n // tn
    steps = (t // tm) * nj
    w_spec = pl.BlockSpec((d, tn), lambda i, j: (0, j))
    out_specs = [pl.BlockSpec((tm, tn), lambda i, j: (i, j))]
    out_shape = [jax.ShapeDtypeStruct((t, n), out_dtype)]
    if emit_w:
        out_specs.append(w_spec)
        out_shape.append(jax.ShapeDtypeStruct(w.shape, BF16))
    side_specs = []
    for s in side_cast:
        chunks = _side_chunks(s.shape[0], steps)
        side_specs.append(pl.BlockSpec((s.shape[0] // chunks, s.shape[1]),
                                       lambda i, j, c=chunks: (jnp.minimum(i * nj + j, c - 1), 0)))
        out_shape.append(jax.ShapeDtypeStruct(s.shape, BF16))
    outs = pl.pallas_call(
        functools.partial(_rms_matmul_kernel, emit_w=emit_w, n_side=len(side_cast)),
        grid=(t // tm, nj),
        in_specs=[pl.BlockSpec(memory_space=pl.ANY), pl.BlockSpec((1, d), lambda i, j: (0, 0)), w_spec] + side_specs,
        out_specs=out_specs + side_specs, out_shape=out_shape,
        scratch_shapes=[pltpu.VMEM((tm, d), BF16), pltpu.VMEM((tm, d), F32), pltpu.SemaphoreType.DMA((1,))],
        compiler_params=_params("arbitrary", "arbitrary"),
        name=name,
    )(x, g.reshape(1, d), w, *side_cast)
    return outs if len(outs) > 1 else outs[0]


LANES = 128


def _mem_kv_kernel(x_ref, g_ref, wk_ref, wv_ref, k_ref, v_ref, kc_ref, vc_ref, xn_ref, *, heads):
    head = pl.program_id(0)

    @pl.when(head == 0)
    def _():
        xn_ref[...] = _rms(x_ref[...], g_ref[...]).astype(BF16)

    rows = x_ref.shape[0]
    slabs = wk_ref.shape[1] // LANES
    for w_ref, rows_ref, cache_ref in ((wk_ref, k_ref, kc_ref), (wv_ref, v_ref, vc_ref)):
        o = jnp.dot(xn_ref[...], w_ref[...].astype(BF16), preferred_element_type=F32)
        rows_ref[...] = o.astype(rows_ref.dtype)
        for c in range(slabs):
            cache_ref[pl.ds(c * heads + head, rows, stride=slabs * heads), :] = o[:, c * LANES:(c + 1) * LANES]


def mem_kv(mem, g, w_k, w_v, *, heads, name):
    t, d = mem.shape
    hd = w_k.shape[1] // heads
    once = dict(pipeline_mode=pl.Buffered(1))
    w_spec = pl.BlockSpec((d, hd), lambda j: (0, j))
    rows_spec = pl.BlockSpec((t, hd), lambda j: (0, j))
    cache_spec = pl.BlockSpec((t * d // LANES, LANES), lambda j: (0, 0), **once)
    return pl.pallas_call(
        functools.partial(_mem_kv_kernel, heads=heads), grid=(heads,),
        in_specs=[pl.BlockSpec((t, d), lambda j: (0, 0), **once), pl.BlockSpec((1, d), lambda j: (0, 0)),
                  w_spec, w_spec],
        out_specs=[rows_spec, rows_spec, cache_spec, cache_spec],
        out_shape=[jax.ShapeDtypeStruct((t, d), BF16)] * 2 + [jax.ShapeDtypeStruct((t * d // LANES, LANES), F32)] * 2,
        scratch_shapes=[pltpu.VMEM((t, d), BF16)],
        compiler_params=_params("arbitrary"), name=name,
    )(mem, g.reshape(1, d), w_k, w_v)


def _cache_view(rows, n, m, heads, hd):
    a = rows.reshape(n * m, hd // LANES, heads, LANES)
    return a.transpose(0, 2, 1, 3).reshape(n, m, heads, hd)


def _mix_kernel(*refs, tm, gate_len, widths, groups, prompt, tiles_per_seq):
    if prompt:
        (x_ref, proj_ref, cgp_ref, xinp_ref, lng_ref, lnb_ref, wsp_ref, bsp_ref, cw_ref,
         wa_ref, wb_ref, wmix_ref, h_ref, tail_ref, gate_ref, gbias_ref) = refs
    else:
        (x_ref, proj_ref, e1_ref, e2_ref, lng_ref, lnb_ref, wsp_ref, bsp_ref, cw_ref,
         wa_ref, wb_ref, wmix_ref, h_ref, p_ref, v_ref, gate_ref, gbias_ref) = refs
    aw, bw, dm = widths
    o_u, o_v, o_bg, o_cg, o_xin, o_ga, o_gb = (0, aw, 2 * aw, 2 * aw + bw, 2 * aw + 2 * bw,
                                               2 * aw + 3 * bw, 2 * aw + 3 * bw + dm)

    def col(o, w):
        return proj_ref[:, o:o + w].astype(F32)

    v = col(o_v, aw)
    mu = jnp.mean(v, axis=-1, keepdims=True)
    vc = v - mu
    var = jnp.mean(vc * vc, axis=-1, keepdims=True)
    vn = (vc * lax.rsqrt(var + EPS)) * lng_ref[...] + lnb_ref[...]
    if not prompt:
        v_ref[...] = vn
    vb = vn.astype(BF16)
    gd = aw // groups

    @pl.when(pl.program_id(0) == 0)
    def _():
        row = lax.broadcasted_iota(jnp.int32, (tm, tm), 0)
        cidx = lax.broadcasted_iota(jnp.int32, (tm, tm), 1)
        keep = ((row ^ cidx) < gate_len) & (cidx <= row)
        cs = wsp_ref.shape[1]
        pick = (lax.broadcasted_iota(jnp.int32, (tm, cs), 1)
                == (lax.broadcasted_iota(jnp.int32, (tm, cs), 0) & (gate_len - 1)))
        pick_b = jnp.where(pick, 1.0, 0.0).astype(BF16)
        for g in range(groups):
            w_rows = jnp.dot(pick_b, wsp_ref[g].astype(BF16), preferred_element_type=F32).astype(BF16)
            w_full = lax.dot_general(w_rows, pick_b, (((1,), (1,)), ((), ())), preferred_element_type=F32)
            gate_ref[g] = jnp.where(keep, w_full, 0.0).astype(BF16)
            gbias_ref[g] = jnp.sum(jnp.where(pick, bsp_ref[g:g + 1, :], 0.0), axis=1, keepdims=True)

    zs = [jnp.dot(gate_ref[g], vb[:, g * gd:(g + 1) * gd], preferred_element_type=F32) + gbias_ref[g]
          for g in range(groups)]
    z = jnp.concatenate(zs, axis=1)
    y_a = (col(o_u, aw) * z).astype(BF16)

    p = col(o_cg, bw) * col(o_xin, bw)
    trow = lax.broadcasted_iota(jnp.int32, (tm, bw), 0)
    if prompt:
        fresh = pl.program_id(0) % tiles_per_seq == 0
        pp = cgp_ref[...].astype(F32) * xinp_ref[...].astype(F32)
        pp = jnp.where(fresh, 0.0, pp)
        last1 = pp[-1:, :]
        last2 = pp[-2:-1, :]
        e1 = jnp.where(trow == 0, last1, 0.0)
        e2 = jnp.where(trow == 0, last2, jnp.where(trow == 1, last1, 0.0))
        tail_ref[...] = p[tm - 8:, :]
    else:
        trow = trow & (gate_len - 1)
        e1 = e1_ref[...]
        e2 = e2_ref[...]
        p_ref[...] = p
    s1 = jnp.where(trow >= 1, pltpu.roll(p, 1, 0), 0.0) + e1
    s2 = jnp.where(trow >= 2, pltpu.roll(p, 2, 0), 0.0) + e2
    conv = cw_ref[0:1, :] * s2 + cw_ref[1:2, :] * s1 + cw_ref[2:3, :] * p
    y_b = (col(o_bg, bw) * conv).astype(BF16)

    merged = (jax.nn.sigmoid(col(o_ga, dm)) * jnp.dot(y_a, wa_ref[...], preferred_element_type=F32)
              + jax.nn.sigmoid(col(o_gb, dm)) * jnp.dot(y_b, wb_ref[...], preferred_element_type=F32))
    h_ref[...] = x_ref[...] + jnp.dot(merged.astype(BF16), wmix_ref[...], preferred_element_type=F32)


def mix(x, proj, prev, ln_g, ln_b, w_sp, b_sp, conv_w, wa, wb, wmix, *, tm, gate_len, seq_len, name):
    t, dm = x.shape
    aw, bw = wa.shape[0], wb.shape[0]
    groups = w_sp.shape[0]
    prompt = prev is None
    nt = t // tm
    row_spec = lambda w: pl.BlockSpec((tm, w), lambda i: (i, 0))
    in_specs = [row_spec(dm), row_spec(proj.shape[1])]
    if prompt:
        prev_rows = 16
        cg_blk = (2 * aw + bw) // bw
        in_specs += [
            pl.BlockSpec((prev_rows, bw), lambda i: (jnp.maximum(i * (tm // prev_rows) - 1, 0), cg_blk)),
            pl.BlockSpec((prev_rows, bw), lambda i: (jnp.maximum(i * (tm // prev_rows) - 1, 0), cg_blk + 1)),
        ]
        extra = (proj, proj)
    else:
        in_specs += [row_spec(bw), row_spec(bw)]
        extra = prev
    in_specs += [_resident((1, aw)), _resident((1, aw)), _resident(w_sp.shape), _resident(b_sp.shape),
                 _resident(conv_w.shape), _resident(wa.shape), _resident(wb.shape), _resident(wmix.shape)]
    out_specs = [row_spec(dm)]
    out_shape = [jax.ShapeDtypeStruct((t, dm), F32)]
    if prompt:
        out_specs.append(pl.BlockSpec((8, bw), lambda i: (i, 0)))
        out_shape.append(jax.ShapeDtypeStruct((nt * 8, bw), F32))
    else:
        out_specs += [row_spec(bw), row_spec(aw)]
        out_shape += [jax.ShapeDtypeStruct((t, bw), F32), jax.ShapeDtypeStruct((t, aw), F32)]
    kern = functools.partial(_mix_kernel, tm=tm, gate_len=gate_len, widths=(aw, bw, dm), groups=groups,
                             prompt=prompt, tiles_per_seq=max(seq_len // tm, 1))
    return pl.pallas_call(
        kern, grid=(nt,), in_specs=in_specs, out_specs=out_specs, out_shape=out_shape,
        scratch_shapes=[pltpu.VMEM((groups, tm, tm), BF16), pltpu.VMEM((groups, tm, 1), F32)],
        compiler_params=_params("arbitrary"), name=name,
    )(x, proj, *extra, ln_g.reshape(1, aw), ln_b.reshape(1, aw), w_sp, b_sp, conv_w, wa, wb, wmix)


def _attn_block_kernel(h_ref, g_ref, wq_ref, k_ref, v_ref, wxo_ref, gn_ref, out_ref, hn_ref, *, heads, head_dim):
    scale = head_dim ** -0.5
    h = h_ref[...]
    q = jnp.dot(_rms(h, g_ref[...]).astype(BF16), wq_ref[...], preferred_element_type=F32).astype(BF16)
    outs = []
    for hd in range(heads):
        cols = slice(hd * head_dim, (hd + 1) * head_dim)
        k = k_ref[0, :, cols].astype(BF16)
        s = lax.dot_general(q[:, cols], k, (((1,), (1,)), ((), ())), preferred_element_type=F32) * scale
        e = jnp.exp(s - jnp.max(s, axis=-1, keepdims=True))
        p = (e / jnp.sum(e, axis=-1, keepdims=True)).astype(BF16)
        outs.append(jnp.dot(p, v_ref[0, :, cols].astype(BF16), preferred_element_type=F32).astype(BF16))
    o = jnp.concatenate(outs, axis=1)
    _residual_and_norm(h, o, wxo_ref, gn_ref, out_ref, hn_ref)


def _residual_and_norm(h, o, wxo_ref, gn_ref, out_ref, hn_ref):
    out = h + jnp.dot(o, wxo_ref[...], preferred_element_type=F32)
    out_ref[...] = out
    hn_ref[...] = _rms(out, gn_ref[...]).astype(BF16)


def attn_block(h, g, wq, k, v, wxo, g_next, *, heads, tm, seq_len, name):
    t, dm = h.shape
    m = k.shape[1]
    per_seq = seq_len // tm
    row_spec = pl.BlockSpec((tm, dm), lambda i: (i, 0))
    kv_spec = pl.BlockSpec((1, m, dm), lambda i: (i // per_seq, 0, 0))
    kern = functools.partial(_attn_block_kernel, heads=heads, head_dim=dm // heads)
    return pl.pallas_call(
        kern, grid=(t // tm,),
        in_specs=[row_spec, _resident((1, dm)), _resident(wq.shape), kv_spec, kv_spec, _resident(wxo.shape),
                  _resident((1, dm))],
        out_specs=[row_spec, row_spec],
        out_shape=[jax.ShapeDtypeStruct((t, dm), F32), jax.ShapeDtypeStruct((t, dm), BF16)],
        compiler_params=_params("parallel"), name=name,
    )(h, g.reshape(1, dm), wq, k, v, wxo, g_next.reshape(1, dm))


def _attn_rows_kernel(q_ref, k_ref, v_ref, o_ref, *, n_seq, rows, heads, head_dim, n_mem):
    scale = head_dim ** -0.5
    slabs = head_dim // LANES
    per_seq = n_mem * slabs * heads
    width = n_mem * slabs
    pairs = [(b, h) for b in range(n_seq) for h in range(heads)]
    lane_slab = lax.broadcasted_iota(jnp.int32, (rows, width), 1) & (slabs - 1)
    is_slab = [lane_slab == c for c in range(slabs)]

    parts = []
    for b, h in pairs:
        r0, c0 = b * rows, h * head_dim
        qh = jnp.concatenate(
            [q_ref[r0:r0 + rows, c0 + c * LANES:c0 + (c + 1) * LANES] for c in range(slabs)], axis=0
        ).astype(BF16)
        kh = k_ref[pl.ds(b * per_seq + h, width, stride=heads), :].astype(BF16)
        g = lax.dot_general(qh, kh, (((1,), (1,)), ((), ())), preferred_element_type=F32)
        s = g[0:rows]
        for c in range(1, slabs):
            s = jnp.where(is_slab[c], g[c * rows:(c + 1) * rows], s)
        parts.append(s)
    s = jnp.concatenate(parts, axis=0)

    lane = lax.broadcasted_iota(jnp.int32, s.shape, 1)
    step = 1
    while step < slabs:
        s = s + jnp.where((lane & step) != 0, pltpu.roll(s, step, 1), pltpu.roll(s, width - step, 1))
        step *= 2
    s = s * scale
    e = jnp.exp(s - jnp.max(s, axis=-1, keepdims=True))
    p = e / (jnp.sum(e, axis=-1, keepdims=True) * (1.0 / slabs))

    for i, (b, h) in enumerate(pairs):
        r0, c0 = b * rows, h * head_dim
        ph = p[i * rows:(i + 1) * rows]
        w = jnp.concatenate([jnp.where(is_slab[c], ph, 0.0) for c in range(slabs)], axis=0).astype(BF16)
        vh = v_ref[pl.ds(b * per_seq + h, width, stride=heads), :].astype(BF16)
        o = jnp.dot(w, vh, preferred_element_type=F32)
        for c in range(slabs):
            o_ref[r0:r0 + rows, c0 + c * LANES:c0 + (c + 1) * LANES] = o[c * rows:(c + 1) * rows]


def _rows_view(a):
    n, m, heads, hd = a.shape
    a = a.reshape(n * m, heads, hd // LANES, LANES)
    return a.transpose(0, 2, 1, 3).reshape(n * m * hd // LANES * heads, LANES)


def attn_rows(q, k, v, *, n_seq, name):
    n, n_mem, heads, hd = k.shape
    t, dm = q.shape
    rows = t // n
    slabs = hd // LANES
    assert slabs & (slabs - 1) == 0 and rows % 8 == 0
    per_seq = n_mem * slabs * heads
    kv_spec = pl.BlockSpec((n_seq * per_seq, LANES), lambda i: (i, 0))
    q_spec = pl.BlockSpec((n_seq * rows, dm), lambda i: (i, 0))
    kern = functools.partial(_attn_rows_kernel, n_seq=n_seq, rows=rows, heads=heads, head_dim=hd, n_mem=n_mem)
    return pl.pallas_call(
        kern, grid=(n // n_seq,),
        in_specs=[q_spec, kv_spec, kv_spec], out_specs=q_spec,
        out_shape=jax.ShapeDtypeStruct((t, dm), F32),
        compiler_params=_params("parallel"), name=name,
    )(q, _rows_view(k), _rows_view(v))


def _attn_out_kernel(h_ref, o_ref, wxo_ref, gn_ref, out_ref, hn_ref):
    _residual_and_norm(h_ref[...], o_ref[...].astype(BF16), wxo_ref, gn_ref, out_ref, hn_ref)


def attn_out(h, o, wxo, g_next, *, tm, name):
    t, dm = h.shape
    row_spec = pl.BlockSpec((tm, dm), lambda i: (i, 0))
    return pl.pallas_call(
        _attn_out_kernel, grid=(t // tm,),
        in_specs=[row_spec, row_spec, _resident(wxo.shape), _resident((1, dm))],
        out_specs=[row_spec, row_spec],
        out_shape=[jax.ShapeDtypeStruct((t, dm), F32), jax.ShapeDtypeStruct((t, dm), BF16)],
        compiler_params=_params("parallel"), name=name,
    )(h, o, wxo, g_next.reshape(1, dm))


def _mlp_kernel(hn_ref, h_hbm, wup_ref, wdn_ref, gf_ref, y_ref, hbuf, sem, *, tm):
    i, j = pl.program_id(0), pl.program_id(1)

    def residual_rows(tile):
        return pltpu.make_async_copy(h_hbm.at[pl.ds(tile * tm, tm), :], hbuf, sem.at[0])

    def up():
        a = jnp.dot(hn_ref[...], wup_ref[...], preferred_element_type=F32)
        return jnp.square(jnp.maximum(a, 0.0)).astype(BF16)

    @pl.when((i == 0) & (j == 0))
    def _():
        residual_rows(0).start()

    @pl.when(j == 0)
    def _():
        a = up()
        residual_rows(i).wait()
        y_ref[...] = hbuf[...] + jnp.dot(a, wdn_ref[...], preferred_element_type=F32)

        @pl.when(i + 1 < pl.num_programs(0))
        def _():
            residual_rows(i + 1).start()

    @pl.when(j > 0)
    def _():
        y_ref[...] += jnp.dot(up(), wdn_ref[...], preferred_element_type=F32)

    @pl.when(j == pl.num_programs(1) - 1)
    def _():
        y_ref[...] = _rms(y_ref[...], gf_ref[...])


def mlp(hn, h, w_up, w_down, g_final, *, tm, fc, name):
    t, dm = h.shape
    dff = w_up.shape[1]
    return pl.pallas_call(
        functools.partial(_mlp_kernel, tm=tm), grid=(t // tm, dff // fc),
        in_specs=[pl.BlockSpec((tm, dm), lambda i, j: (i, 0)),
                  pl.BlockSpec(memory_space=pl.ANY),
                  pl.BlockSpec((dm, fc), lambda i, j: (0, j)),
                  pl.BlockSpec((fc, dm), lambda i, j: (j, 0)),
                  pl.BlockSpec((1, dm), lambda i, j: (0, 0))],
        out_specs=pl.BlockSpec((tm, dm), lambda i, j: (i, 0)),
        out_shape=jax.ShapeDtypeStruct((t, dm), F32),
        scratch_shapes=[pltpu.VMEM((tm, dm), F32), pltpu.SemaphoreType.DMA((1,))],
        compiler_params=_params("arbitrary", "arbitrary"), name=name,
    )(hn, h, w_up, w_down, g_final.reshape(1, dm))


MIX_TM = 256
ROW_TM = 1024
COL_TN = 1024
WIDE_TN = 1536
ATTN_TM = 512
MLP_FC = 1024


def _gate_len(seq_len, w_spatial):
    gate_len = CHUNK if seq_len % CHUNK == 0 else seq_len
    assert gate_len & (gate_len - 1) == 0 and MIX_TM % gate_len == 0 and gate_len <= w_spatial.shape[1]
    return gate_len


def kernel(x_prompt, x_sample, state_conv, cache_mem_k, cache_mem_v, mem_prompt, norm_mix_g, w_in, ln_v_g, ln_v_b, w_spatial, b_spatial, conv_w, w_branch_a, w_branch_b, w_mix_out, norm_x_g, norm_mem_g, w_q, w_k, w_v, w_x_out, norm_mlp_g, w_up, w_down, norm_final_g):
    depth = w_in.shape[0]
    assert depth == 1, "the final rmsnorm is fused into the single layer's MLP kernel"
    nb, seq, dm = x_prompt.shape
    nd, dseq, _ = x_sample.shape
    n_mem, heads, hd = cache_mem_k.shape[2:]
    l = 0
    xs = x_sample.reshape(nd * dseq, dm)
    xp = x_prompt.reshape(nb * seq, dm)
    assert xs.shape[0] == ROW_TM

    proj_s, w_in_b = rms_matmul(xs, norm_mix_g[l], w_in[l], BF16, tm=ROW_TM, tn=COL_TN, emit_w=True,
                                name="in_proj_sample")
    proj_p, w_a, w_b, w_mix, w_q_b, w_xo, w_up_b, w_down_b = rms_matmul(
        xp, norm_mix_g[l], w_in_b, BF16, tm=ROW_TM, tn=WIDE_TN, name="in_proj_prompt",
        side_cast=(w_branch_a[l], w_branch_b[l], w_mix_out[l], w_q[l], w_x_out[l], w_up[l], w_down[l]))
    mix_w = (ln_v_g[l], ln_v_b[l], w_spatial[l], b_spatial[l], conv_w[l], w_a, w_b, w_mix)

    prev = state_conv[l]
    e1 = jnp.pad(prev[:, 1:2], ((0, 0), (0, dseq - 1), (0, 0))).reshape(nd * dseq, -1)
    e2 = jnp.pad(prev, ((0, 0), (0, dseq - 2), (0, 0))).reshape(nd * dseq, -1)
    h_s, p_s, vn_s = mix(xs, proj_s, (e1, e2), *mix_w, tm=MIX_TM, gate_len=_gate_len(dseq, w_spatial[l]),
                         seq_len=dseq, name="mix_sample")
    q_s = rms_matmul(h_s, norm_x_g[l], w_q_b, F32, tm=ROW_TM, tn=COL_TN, name="q_proj_sample")
    o_s = attn_rows(q_s, cache_mem_k[l], cache_mem_v[l], n_seq=4, name="attn_sample")
    h_s, hn_s = attn_out(h_s, o_s, w_xo, norm_mlp_g[l], tm=ATTN_TM, name="x_out_sample")
    y_s = mlp(hn_s, h_s, w_up_b, w_down_b, norm_final_g, tm=ROW_TM, fc=MLP_FC, name="mlp_sample")

    mem = mem_prompt.reshape(nb * n_mem, dm)
    k_b, v_b, k_rows, v_rows = mem_kv(mem, norm_mem_g[l], w_k[l], w_v[l], heads=heads, name="mem_kv")
    h_p, tail_p = mix(xp, proj_p, None, *mix_w, tm=MIX_TM, gate_len=_gate_len(seq, w_spatial[l]),
                      seq_len=seq, name="mix_prompt")
    h_p, hn_p = attn_block(h_p, norm_x_g[l], w_q_b, k_b.reshape(nb, n_mem, dm), v_b.reshape(nb, n_mem, dm), w_xo,
                           norm_mlp_g[l], heads=heads, tm=ATTN_TM, seq_len=seq, name="attn_block_prompt")
    y_p = mlp(hn_p, h_p, w_up_b, w_down_b, norm_final_g, tm=ROW_TM, fc=MLP_FC, name="mlp_prompt")
    k_p = _cache_view(k_rows, nb, n_mem, heads, hd)
    v_p = _cache_view(v_rows, nb, n_mem, heads, hd)

    bw = p_s.shape[1]
    keep = conv_w.shape[1] - 1
    conv_p = tail_p.reshape(nb, seq // MIX_TM, 8, bw)[:, -1, 8 - keep:, :]
    conv_s = p_s.reshape(nd, dseq, bw)[:, dseq - keep:, :]
    return (y_p.reshape(nb, seq, dm), y_s.reshape(nd, dseq, dm), k_p[None], v_p[None],
            conv_p[None], conv_s[None], vn_s.reshape(1, nd, dseq, -1))
```

```python
import functools

import jax
import jax.numpy as jnp
from jax import lax
from jax.experimental import pallas as pl
from jax.experimental.pallas import tpu as pltpu

EPS = 1e-6
CHUNK = 128
F32 = jnp.float32
BF16 = jnp.bfloat16

V7X_VMEM_BYTES = 64 * 1024 * 1024
VMEM_LIMIT = V7X_VMEM_BYTES * 7 // 8


def _params(*sem):
    return pltpu.CompilerParams(dimension_semantics=sem, vmem_limit_bytes=VMEM_LIMIT)


def _rms(x, g):
    r = lax.rsqrt(jnp.mean(x * x, axis=-1, keepdims=True) + EPS)
    return (x * r) * g


def _resident(shape):
    return pl.BlockSpec(shape, lambda *_: (0,) * len(shape), pipeline_mode=pl.Buffered(1))


def _bf16_tile(w_ref, wb_ref):
    if wb_ref is None:
        return w_ref[...].astype(BF16)
    wb_ref[...] = w_ref[...].astype(BF16)
    return wb_ref[...]


def _rms_matmul_kernel(x_hbm, g_ref, w_ref, *rest, emit_w, n_side):
    side_in, rest = rest[:n_side], rest[n_side:]
    o_ref, rest = rest[0], rest[1:]
    wb_ref, rest = (rest[0], rest[1:]) if emit_w else (None, rest)
    side_out, (xn_ref, xbuf, sem) = rest[:n_side], rest[n_side:]
    i, j = pl.program_id(0), pl.program_id(1)
    tm = xbuf.shape[0]

    def x_rows(tile):
        return pltpu.make_async_copy(x_hbm.at[pl.ds(tile * tm, tm), :], xbuf, sem.at[0])

    @pl.when((i == 0) & (j == 0))
    def _():
        x_rows(0).start()

    @pl.when(j == 0)
    def _():
        x_rows(i).wait()
        xn_ref[...] = _rms(xbuf[...], g_ref[...]).astype(BF16)

        @pl.when(i + 1 < pl.num_programs(0))
        def _():
            x_rows(i + 1).start()

    w = _bf16_tile(w_ref, wb_ref)
    o_ref[...] = jnp.dot(xn_ref[...], w, preferred_element_type=F32).astype(o_ref.dtype)
    for src, dst in zip(side_in, side_out):
        dst[...] = src[...].astype(BF16)


BF16_SUBLANES = 16


def _side_chunks(rows, steps):
    return max(n for n in range(1, steps + 1) if rows % n == 0 and (rows // n) % BF16_SUBLANES == 0)


def rms_matmul(x, g, w, out_dtype, *, tm, tn, name, emit_w=False, side_cast=()):
    t, d = x.shape
    n = w.shape[1]
    assert t == tm or not emit_w
    nj = n // tn
    steps = (t // tm) * nj
    w_spec = pl.BlockSpec((d, tn), lambda i, j: (0, j))
    out_specs = [pl.BlockSpec((tm, tn), lambda i, j: (i, j))]
    out_shape = [jax.ShapeDtypeStruct((t, n), out_dtype)]
    if emit_w:
        out_specs.append(w_spec)
        out_shape.append(jax.ShapeDtypeStruct(w.shape, BF16))
    side_specs = []
    for s in side_cast:
        chunks = _side_chunks(s.shape[0], steps)
        side_specs.append(pl.BlockSpec((s.shape[0] // chunks, s.shape[1]),
                                       lambda i, j, c=chunks: (jnp.minimum(i * nj + j, c - 1), 0)))
        out_shape.append(jax.ShapeDtypeStruct(s.shape, BF16))
    outs = pl.pallas_call(
        functools.partial(_rms_matmul_kernel, emit_w=emit_w, n_side=len(side_cast)),
        grid=(t // tm, nj),
        in_specs=[pl.BlockSpec(memory_space=pl.ANY), pl.BlockSpec((1, d), lambda i, j: (0, 0)), w_spec] + side_specs,
        out_specs=out_specs + side_specs, out_shape=out_shape,
        scratch_shapes=[pltpu.VMEM((tm, d), BF16), pltpu.VMEM((tm, d), F32), pltpu.SemaphoreType.DMA((1,))],
        compiler_params=_params("arbitrary", "arbitrary"),
        name=name,
    )(x, g.reshape(1, d), w, *side_cast)
    return outs if len(outs) > 1 else outs[0]


LANES = 128


def _mem_kv_kernel(x_ref, g_ref, wk_ref, wv_ref, k_ref, v_ref, kc_ref, vc_ref, xn_ref, *, heads):
    head = pl.program_id(0)

    @pl.when(head == 0)
    def _():
        xn_ref[...] = _rms(x_ref[...], g_ref[...]).astype(BF16)

    rows = x_ref.shape[0]
    slabs = wk_ref.shape[1] // LANES
    for w_ref, rows_ref, cache_ref in ((wk_ref, k_ref, kc_ref), (wv_ref, v_ref, vc_ref)):
        o = jnp.dot(xn_ref[...], w_ref[...].astype(BF16), preferred_element_type=F32)
        rows_ref[...] = o.astype(rows_ref.dtype)
        for c in range(slabs):
            cache_ref[pl.ds(c * heads + head, rows, stride=slabs * heads), :] = o[:, c * LANES:(c + 1) * LANES]


def mem_kv(mem, g, w_k, w_v, *, heads, name):
    t, d = mem.shape
    hd = w_k.shape[1] // heads
    once = dict(pipeline_mode=pl.Buffered(1))
    w_spec = pl.BlockSpec((d, hd), lambda j: (0, j))
    rows_spec = pl.BlockSpec((t, hd), lambda j: (0, j))
    cache_spec = pl.BlockSpec((t * d // LANES, LANES), lambda j: (0, 0), **once)
    return pl.pallas_call(
        functools.partial(_mem_kv_kernel, heads=heads), grid=(heads,),
        in_specs=[pl.BlockSpec((t, d), lambda j: (0, 0), **once), pl.BlockSpec((1, d), lambda j: (0, 0)),
                  w_spec, w_spec],
        out_specs=[rows_spec, rows_spec, cache_spec, cache_spec],
        out_shape=[jax.ShapeDtypeStruct((t, d), BF16)] * 2 + [jax.ShapeDtypeStruct((t * d // LANES, LANES), F32)] * 2,
        scratch_shapes=[pltpu.VMEM((t, d), BF16)],
        compiler_params=_params("arbitrary"), name=name,
    )(mem, g.reshape(1, d), w_k, w_v)


def _cache_view(rows, n, m, heads, hd):
    a = rows.reshape(n * m, hd // LANES, heads, LANES)
    return a.transpose(0, 2, 1, 3).reshape(n, m, heads, hd)


def _mix_kernel(*refs, tm, gate_len, widths, groups, prompt, tiles_per_seq):
    if prompt:
        (x_ref, proj_ref, cgp_ref, xinp_ref, lng_ref, lnb_ref, wsp_ref, bsp_ref, cw_ref,
         wa_ref, wb_ref, wmix_ref, h_ref, tail_ref, gate_ref, gbias_ref) = refs
    else:
        (x_ref, proj_ref, e1_ref, e2_ref, lng_ref, lnb_ref, wsp_ref, bsp_ref, cw_ref,
         wa_ref, wb_ref, wmix_ref, h_ref, p_ref, v_ref, gate_ref, gbias_ref) = refs
    aw, bw, dm = widths
    o_u, o_v, o_bg, o_cg, o_xin, o_ga, o_gb = (0, aw, 2 * aw, 2 * aw + bw, 2 * aw + 2 * bw,
                                               2 * aw + 3 * bw, 2 * aw + 3 * bw + dm)

    def col(o, w):
        return proj_ref[:, o:o + w].astype(F32)

    gd = aw // groups

    @pl.when(pl.program_id(0) == 0)
    def _():
        row = lax.broadcasted_iota(jnp.int32, (tm, tm), 0)
        cidx = lax.broadcasted_iota(jnp.int32, (tm, tm), 1)
        keep = ((row ^ cidx) < gate_len) & (cidx <= row)
        cs = wsp_ref.shape[1]
        pick = (lax.broadcasted_iota(jnp.int32, (tm, cs), 1)
                == (lax.broadcasted_iota(jnp.int32, (tm, cs), 0) & (gate_len - 1)))
        pick_b = jnp.where(pick, 1.0, 0.0).astype(BF16)
        for g in range(groups):
            w_rows = jnp.dot(pick_b, wsp_ref[g].astype(BF16), preferred_element_type=F32).astype(BF16)
            w_full = lax.dot_general(w_rows, pick_b, (((1,), (1,)), ((), ())), preferred_element_type=F32)
            gate_ref[g] = jnp.where(keep, w_full, 0.0).astype(BF16)
            gbias_ref[g] = jnp.sum(jnp.where(pick, bsp_ref[g:g + 1, :], 0.0), axis=1, keepdims=True)

    p = col(o_cg, bw) * col(o_xin, bw)
    trow = lax.broadcasted_iota(jnp.int32, (tm, bw), 0)
    if prompt:
        fresh = pl.program_id(0) % tiles_per_seq == 0
        pp = cgp_ref[...].astype(F32) * xinp_ref[...].astype(F32)
        pp = jnp.where(fresh, 0.0, pp)
        last1 = pp[-1:, :]
        last2 = pp[-2:-1, :]
        e1 = jnp.where(trow == 0, last1, 0.0)
        e2 = jnp.where(trow == 0, last2, jnp.where(trow == 1, last1, 0.0))
        tail_ref[...] = p[tm - 8:, :]
    else:
        trow = trow & (gate_len - 1)
        e1 = e1_ref[...]
        e2 = e2_ref[...]
        p_ref[...] = p
    s1 = jnp.where(trow >= 1, pltpu.roll(p, 1, 0), 0.0) + e1
    s2 = jnp.where(trow >= 2, pltpu.roll(p, 2, 0), 0.0) + e2
    conv = cw_ref[0:1, :] * s2 + cw_ref[1:2, :] * s1 + cw_ref[2:3, :] * p
    y_b = (col(o_bg, bw) * conv).astype(BF16)
    branch_b = jax.nn.sigmoid(col(o_gb, dm)) * jnp.dot(y_b, wb_ref[...], preferred_element_type=F32)

    v = col(o_v, aw)
    mu = jnp.mean(v, axis=-1, keepdims=True)
    vc = v - mu
    var = jnp.mean(vc * vc, axis=-1, keepdims=True)
    vn = (vc * lax.rsqrt(var + EPS)) * lng_ref[...] + lnb_ref[...]
    if not prompt:
        v_ref[...] = vn
    vb = vn.astype(BF16)
    zs = [jnp.dot(gate_ref[g], vb[:, g * gd:(g + 1) * gd], preferred_element_type=F32) + gbias_ref[g]
          for g in range(groups)]
    y_a = (col(o_u, aw) * jnp.concatenate(zs, axis=1)).astype(BF16)
    branch_a = jax.nn.sigmoid(col(o_ga, dm)) * jnp.dot(y_a, wa_ref[...], preferred_element_type=F32)

    h_ref[...] = x_ref[...] + jnp.dot((branch_a + branch_b).astype(BF16), wmix_ref[...],
                                      preferred_element_type=F32)


def mix(x, proj, prev, ln_g, ln_b, w_sp, b_sp, conv_w, wa, wb, wmix, *, tm, gate_len, seq_len, name):
    t, dm = x.shape
    aw, bw = wa.shape[0], wb.shape[0]
    groups = w_sp.shape[0]
    prompt = prev is None
    nt = t // tm
    row_spec = lambda w: pl.BlockSpec((tm, w), lambda i: (i, 0))
    in_specs = [row_spec(dm), row_spec(proj.shape[1])]
    if prompt:
        prev_rows = 16
        cg_blk = (2 * aw + bw) // bw
        in_specs += [
            pl.BlockSpec((prev_rows, bw), lambda i: (jnp.maximum(i * (tm // prev_rows) - 1, 0), cg_blk)),
            pl.BlockSpec((prev_rows, bw), lambda i: (jnp.maximum(i * (tm // prev_rows) - 1, 0), cg_blk + 1)),
        ]
        extra = (proj, proj)
    else:
        in_specs += [row_spec(bw), row_spec(bw)]
        extra = prev
    in_specs += [_resident((1, aw)), _resident((1, aw)), _resident(w_sp.shape), _resident(b_sp.shape),
                 _resident(conv_w.shape), _resident(wa.shape), _resident(wb.shape), _resident(wmix.shape)]
    out_specs = [row_spec(dm)]
    out_shape = [jax.ShapeDtypeStruct((t, dm), F32)]
    if prompt:
        out_specs.append(pl.BlockSpec((8, bw), lambda i: (i, 0)))
        out_shape.append(jax.ShapeDtypeStruct((nt * 8, bw), F32))
    else:
        out_specs += [row_spec(bw), row_spec(aw)]
        out_shape += [jax.ShapeDtypeStruct((t, bw), F32), jax.ShapeDtypeStruct((t, aw), F32)]
    kern = functools.partial(_mix_kernel, tm=tm, gate_len=gate_len, widths=(aw, bw, dm), groups=groups,
                             prompt=prompt, tiles_per_seq=max(seq_len // tm, 1))
    return pl.pallas_call(
        kern, grid=(nt,), in_specs=in_specs, out_specs=out_specs, out_shape=out_shape,
        scratch_shapes=[pltpu.VMEM((groups, tm, tm), BF16), pltpu.VMEM((groups, tm, 1), F32)],
        compiler_params=_params("arbitrary"), name=name,
    )(x, proj, *extra, ln_g.reshape(1, aw), ln_b.reshape(1, aw), w_sp, b_sp, conv_w, wa, wb, wmix)


def _attn_block_kernel(h_ref, g_ref, wq_ref, k_ref, v_ref, wxo_ref, gn_ref, out_ref, hn_ref, *, heads, head_dim):
    scale = head_dim ** -0.5
    h = h_ref[...]
    q = jnp.dot(_rms(h, g_ref[...]).astype(BF16), wq_ref[...], preferred_element_type=F32).astype(BF16)
    outs = []
    for hd in range(heads):
        cols = slice(hd * head_dim, (hd + 1) * head_dim)
        k = k_ref[0, :, cols].astype(BF16)
        s = lax.dot_general(q[:, cols], k, (((1,), (1,)), ((), ())), preferred_element_type=F32) * scale
        e = jnp.exp(s - jnp.max(s, axis=-1, keepdims=True))
        p = (e / jnp.sum(e, axis=-1, keepdims=True)).astype(BF16)
        outs.append(jnp.dot(p, v_ref[0, :, cols].astype(BF16), preferred_element_type=F32).astype(BF16))
    o = jnp.concatenate(outs, axis=1)
    _residual_and_norm(h, o, wxo_ref, gn_ref, out_ref, hn_ref)


def _residual_and_norm(h, o, wxo_ref, gn_ref, out_ref, hn_ref):
    out = h + jnp.dot(o, wxo_ref[...], preferred_element_type=F32)
    out_ref[...] = out
    hn_ref[...] = _rms(out, gn_ref[...]).astype(BF16)


def attn_block(h, g, wq, k, v, wxo, g_next, *, heads, tm, seq_len, name):
    t, dm = h.shape
    m = k.shape[1]
    per_seq = seq_len // tm
    row_spec = pl.BlockSpec((tm, dm), lambda i: (i, 0))
    kv_spec = pl.BlockSpec((1, m, dm), lambda i: (i // per_seq, 0, 0))
    kern = functools.partial(_attn_block_kernel, heads=heads, head_dim=dm // heads)
    return pl.pallas_call(
        kern, grid=(t // tm,),
        in_specs=[row_spec, _resident((1, dm)), _resident(wq.shape), kv_spec, kv_spec, _resident(wxo.shape),
                  _resident((1, dm))],
        out_specs=[row_spec, row_spec],
        out_shape=[jax.ShapeDtypeStruct((t, dm), F32), jax.ShapeDtypeStruct((t, dm), BF16)],
        compiler_params=_params("parallel"), name=name,
    )(h, g.reshape(1, dm), wq, k, v, wxo, g_next.reshape(1, dm))


def _attn_rows_kernel(q_ref, k_ref, v_ref, o_ref, *, n_seq, rows, heads, head_dim, n_mem):
    scale = head_dim ** -0.5
    slabs = head_dim // LANES
    per_seq = n_mem * slabs * heads
    width = n_mem * slabs
    pairs = [(b, h) for b in range(n_seq) for h in range(heads)]
    lane_slab = lax.broadcasted_iota(jnp.int32, (rows, width), 1) & (slabs - 1)
    is_slab = [lane_slab == c for c in range(slabs)]

    parts = []
    for b, h in pairs:
        r0, c0 = b * rows, h * head_dim
        qh = jnp.concatenate(
            [q_ref[r0:r0 + rows, c0 + c * LANES:c0 + (c + 1) * LANES] for c in range(slabs)], axis=0
        ).astype(BF16)
        kh = k_ref[pl.ds(b * per_seq + h, width, stride=heads), :].astype(BF16)
        g = lax.dot_general(qh, kh, (((1,), (1,)), ((), ())), preferred_element_type=F32)
        s = g[0:rows]
        for c in range(1, slabs):
            s = jnp.where(is_slab[c], g[c * rows:(c + 1) * rows], s)
        parts.append(s)
    s = jnp.concatenate(parts, axis=0)

    lane = lax.broadcasted_iota(jnp.int32, s.shape, 1)
    step = 1
    while step < slabs:
        s = s + jnp.where((lane & step) != 0, pltpu.roll(s, step, 1), pltpu.roll(s, width - step, 1))
        step *= 2
    s = s * scale
    e = jnp.exp(s - jnp.max(s, axis=-1, keepdims=True))
    p = e / (jnp.sum(e, axis=-1, keepdims=True) * (1.0 / slabs))

    for i, (b, h) in enumerate(pairs):
        r0, c0 = b * rows, h * head_dim
        ph = p[i * rows:(i + 1) * rows]
        w = jnp.concatenate([jnp.where(is_slab[c], ph, 0.0) for c in range(slabs)], axis=0).astype(BF16)
        vh = v_ref[pl.ds(b * per_seq + h, width, stride=heads), :].astype(BF16)
        o = jnp.dot(w, vh, preferred_element_type=F32)
        for c in range(slabs):
            o_ref[r0:r0 + rows, c0 + c * LANES:c0 + (c + 1) * LANES] = o[c * rows:(c + 1) * rows]


def _rows_view(a):
    n, m, heads, hd = a.shape
    a = a.reshape(n * m, heads, hd // LANES, LANES)
    return a.transpose(0, 2, 1, 3).reshape(n * m * hd // LANES * heads, LANES)


def attn_rows(q, k, v, *, n_seq, name):
    n, n_mem, heads, hd = k.shape
    t, dm = q.shape
    rows = t // n
    slabs = hd // LANES
    assert slabs & (slabs - 1) == 0 and rows % 8 == 0
    per_seq = n_mem * slabs * heads
    kv_spec = pl.BlockSpec((n_seq * per_seq, LANES), lambda i: (i, 0))
    q_spec = pl.BlockSpec((n_seq * rows, dm), lambda i: (i, 0))
    kern = functools.partial(_attn_rows_kernel, n_seq=n_seq, rows=rows, heads=heads, head_dim=hd, n_mem=n_mem)
    return pl.pallas_call(
        kern, grid=(n // n_seq,),
        in_specs=[q_spec, kv_spec, kv_spec], out_specs=q_spec,
        out_shape=jax.ShapeDtypeStruct((t, dm), F32),
        compiler_params=_params("parallel"), name=name,
    )(q, _rows_view(k), _rows_view(v))


def _attn_out_kernel(h_ref, o_ref, wxo_ref, gn_ref, out_ref, hn_ref):
    _residual_and_norm(h_ref[...], o_ref[...].astype(BF16), wxo_ref, gn_ref, out_ref, hn_ref)


def attn_out(h, o, wxo, g_next, *, tm, name):
    t, dm = h.shape
    row_spec = pl.BlockSpec((tm, dm), lambda i: (i, 0))
    return pl.pallas_call(
        _attn_out_kernel, grid=(t // tm,),
        in_specs=[row_spec, row_spec, _resident(wxo.shape), _resident((1, dm))],
        out_specs=[row_spec, row_spec],
        out_shape=[jax.ShapeDtypeStruct((t, dm), F32), jax.ShapeDtypeStruct((t, dm), BF16)],
        compiler_params=_params("parallel"), name=name,
    )(h, o, wxo, g_next.reshape(1, dm))


def _mlp_kernel(hn_ref, h_hbm, wup_ref, wdn_ref, gf_ref, y_ref, hbuf, sem, *, tm):
    i, j = pl.program_id(0), pl.program_id(1)

    def residual_rows(tile):
        return pltpu.make_async_copy(h_hbm.at[pl.ds(tile * tm, tm), :], hbuf, sem.at[0])

    def up():
        a = jnp.dot(hn_ref[...], wup_ref[...], preferred_element_type=F32)
        return jnp.square(jnp.maximum(a, 0.0)).astype(BF16)

    @pl.when((i == 0) & (j == 0))
    def _():
        residual_rows(0).start()

    @pl.when(j == 0)
    def _():
        a = up()
        residual_rows(i).wait()
        y_ref[...] = hbuf[...] + jnp.dot(a, wdn_ref[...], preferred_element_type=F32)

        @pl.when(i + 1 < pl.num_programs(0))
        def _():
            residual_rows(i + 1).start()

    @pl.when(j > 0)
    def _():
        y_ref[...] += jnp.dot(up(), wdn_ref[...], preferred_element_type=F32)

    @pl.when(j == pl.num_programs(1) - 1)
    def _():
        y_ref[...] = _rms(y_ref[...], gf_ref[...])


def mlp(hn, h, w_up, w_down, g_final, *, tm, fc, name):
    t, dm = h.shape
    dff = w_up.shape[1]
    return pl.pallas_call(
        functools.partial(_mlp_kernel, tm=tm), grid=(t // tm, dff // fc),
        in_specs=[pl.BlockSpec((tm, dm), lambda i, j: (i, 0)),
                  pl.BlockSpec(memory_space=pl.ANY),
                  pl.BlockSpec((dm, fc), lambda i, j: (0, j)),
                  pl.BlockSpec((fc, dm), lambda i, j: (j, 0)),
                  pl.BlockSpec((1, dm), lambda i, j: (0, 0))],
        out_specs=pl.BlockSpec((tm, dm), lambda i, j: (i, 0)),
        out_shape=jax.ShapeDtypeStruct((t, dm), F32),
        scratch_shapes=[pltpu.VMEM((tm, dm), F32), pltpu.SemaphoreType.DMA((1,))],
        compiler_params=_params("arbitrary", "arbitrary"), name=name,
    )(hn, h, w_up, w_down, g_final.reshape(1, dm))


MIX_TM = 256
ROW_TM = 1024
COL_TN = 1024
WIDE_TN = 1536
ATTN_TM = 512
MLP_FC = 1024


def _gate_len(seq_len, w_spatial):
    gate_len = CHUNK if seq_len % CHUNK == 0 else seq_len
    assert gate_len & (gate_len - 1) == 0 and MIX_TM % gate_len == 0 and gate_len <= w_spatial.shape[1]
    return gate_len


def kernel(x_prompt, x_sample, state_conv, cache_mem_k, cache_mem_v, mem_prompt, norm_mix_g, w_in, ln_v_g, ln_v_b, w_spatial, b_spatial, conv_w, w_branch_a, w_branch_b, w_mix_out, norm_x_g, norm_mem_g, w_q, w_k, w_v, w_x_out, norm_mlp_g, w_up, w_down, norm_final_g):
    depth = w_in.shape[0]
    assert depth == 1, "the final rmsnorm is fused into the single layer's MLP kernel"
    nb, seq, dm = x_prompt.shape
    nd, dseq, _ = x_sample.shape
    n_mem, heads, hd = cache_mem_k.shape[2:]
    l = 0
    xs = x_sample.reshape(nd * dseq, dm)
    xp = x_prompt.reshape(nb * seq, dm)
    assert xs.shape[0] == ROW_TM

    proj_s, w_in_b = rms_matmul(xs, norm_mix_g[l], w_in[l], BF16, tm=ROW_TM, tn=COL_TN, emit_w=True,
                                name="in_proj_sample")
    proj_p, w_a, w_b, w_mix, w_q_b, w_xo, w_up_b, w_down_b = rms_matmul(
        xp, norm_mix_g[l], w_in_b, BF16, tm=ROW_TM, tn=WIDE_TN, name="in_proj_prompt",
        side_cast=(w_branch_a[l], w_branch_b[l], w_mix_out[l], w_q[l], w_x_out[l], w_up[l], w_down[l]))
    mix_w = (ln_v_g[l], ln_v_b[l], w_spatial[l], b_spatial[l], conv_w[l], w_a, w_b, w_mix)

    prev = state_conv[l]
    e1 = jnp.pad(prev[:, 1:2], ((0, 0), (0, dseq - 1), (0, 0))).reshape(nd * dseq, -1)
    e2 = jnp.pad(prev, ((0, 0), (0, dseq - 2), (0, 0))).reshape(nd * dseq, -1)
    h_s, p_s, vn_s = mix(xs, proj_s, (e1, e2), *mix_w, tm=MIX_TM, gate_len=_gate_len(dseq, w_spatial[l]),
                         seq_len=dseq, name="mix_sample")
    q_s = rms_matmul(h_s, norm_x_g[l], w_q_b, F32, tm=ROW_TM, tn=COL_TN, name="q_proj_sample")
    o_s = attn_rows(q_s, cache_mem_k[l], cache_mem_v[l], n_seq=4, name="attn_sample")
    h_s, hn_s = attn_out(h_s, o_s, w_xo, norm_mlp_g[l], tm=ATTN_TM, name="x_out_sample")
    y_s = mlp(hn_s, h_s, w_up_b, w_down_b, norm_final_g, tm=ROW_TM, fc=MLP_FC, name="mlp_sample")

    mem = mem_prompt.reshape(nb * n_mem, dm)
    k_b, v_b, k_rows, v_rows = mem_kv(mem, norm_mem_g[l], w_k[l], w_v[l], heads=heads, name="mem_kv")
    h_p, tail_p = mix(xp, proj_p, None, *mix_w, tm=MIX_TM, gate_len=_gate_len(seq, w_spatial[l]),
                      seq_len=seq, name="mix_prompt")
    h_p, hn_p = attn_block(h_p, norm_x_g[l], w_q_b, k_b.reshape(nb, n_mem, dm), v_b.reshape(nb, n_mem, dm), w_xo,
                           norm_mlp_g[l], heads=heads, tm=ATTN_TM, seq_len=seq, name="attn_block_prompt")
    y_p = mlp(hn_p, h_p, w_up_b, w_down_b, norm_final_g, tm=ROW_TM, fc=MLP_FC, name="mlp_prompt")
    k_p = _cache_view(k_rows, nb, n_mem, heads, hd)
    v_p = _cache_view(v_rows, nb, n_mem, heads, hd)

    bw = p_s.shape[1]
    keep = conv_w.shape[1] - 1
    conv_p = tail_p.reshape(nb, seq // MIX_TM, 8, bw)[:, -1, 8 - keep:, :]
    conv_s = p_s.reshape(nd, dseq, bw)[:, dseq - keep:, :]
    return (y_p.reshape(nb, seq, dm), y_s.reshape(nd, dseq, dm), k_p[None], v_p[None],
            conv_p[None], conv_s[None], vn_s.reshape(1, nd, dseq, -1))
```

```python
import functools

import jax
import jax.numpy as jnp
from jax import lax
from jax.experimental import pallas as pl
from jax.experimental.pallas import tpu as pltpu

EPS = 1e-6
CHUNK = 128
F32 = jnp.float32
BF16 = jnp.bfloat16

V7X_VMEM_BYTES = 64 * 1024 * 1024
VMEM_LIMIT = V7X_VMEM_BYTES * 7 // 8


def _params(*sem):
    return pltpu.CompilerParams(dimension_semantics=sem, vmem_limit_bytes=VMEM_LIMIT)


def _rms(x, g):
    r = lax.rsqrt(jnp.mean(x * x, axis=-1, keepdims=True) + EPS)
    return (x * r) * g


def _resident(shape):
    return pl.BlockSpec(shape, lambda *_: (0,) * len(shape), pipeline_mode=pl.Buffered(1))


def _bf16_tile(w_ref, wb_ref):
    if wb_ref is None:
        return w_ref[...].astype(BF16)
    wb_ref[...] = w_ref[...].astype(BF16)
    return wb_ref[...]


def _rms_matmul_kernel(x_hbm, g_ref, w_ref, *rest, emit_w, n_side):
    side_in, rest = rest[:n_side], rest[n_side:]
    o_ref, rest = rest[0], rest[1:]
    wb_ref, rest = (rest[0], rest[1:]) if emit_w else (None, rest)
    side_out, (xn_ref, xbuf, sem) = rest[:n_side], rest[n_side:]
    i, j = pl.program_id(0), pl.program_id(1)
    tm = xbuf.shape[0]

    def x_rows(tile):
        return pltpu.make_async_copy(x_hbm.at[pl.ds(tile * tm, tm), :], xbuf, sem.at[0])

    @pl.when((i == 0) & (j == 0))
    def _():
        x_rows(0).start()

    @pl.when(j == 0)
    def _():
        x_rows(i).wait()
        xn_ref[...] = _rms(xbuf[...], g_ref[...]).astype(BF16)

        @pl.when(i + 1 < pl.num_programs(0))
        def _():
            x_rows(i + 1).start()

    w = _bf16_tile(w_ref, wb_ref)
    o_ref[...] = jnp.dot(xn_ref[...], w, preferred_element_type=F32).astype(o_ref.dtype)
    for src, dst in zip(side_in, side_out):
        dst[...] = src[...].astype(BF16)


BF16_SUBLANES = 16


def _side_chunks(rows, steps):
    return max(n for n in range(1, steps + 1) if rows % n == 0 and (rows // n) % BF16_SUBLANES == 0)


def rms_matmul(x, g, w, out_dtype, *, tm, tn, name, emit_w=False, side_cast=()):
    t, d = x.shape
    n = w.shape[1]
    assert t == tm or not emit_w
    nj = n // tn
    steps = (t // tm) * nj
    w_spec = pl.BlockSpec((d, tn), lambda i, j: (0, j))
    out_specs = [pl.BlockSpec((tm, tn), lambda i, j: (i, j))]
    out_shape = [jax.ShapeDtypeStruct((t, n), out_dtype)]
    if emit_w:
        out_specs.append(w_spec)
        out_shape.append(jax.ShapeDtypeStruct(w.shape, BF16))
    side_specs = []
    for s in side_cast:
        chunks = _side_chunks(s.shape[0], steps)
        side_specs.append(pl.BlockSpec((s.shape[0] // chunks, s.shape[1]),
                                       lambda i, j, c=chunks: (jnp.minimum(i * nj + j, c - 1), 0)))
        out_shape.append(jax.ShapeDtypeStruct(s.shape, BF16))
    outs = pl.pallas_call(
        functools.partial(_rms_matmul_kernel, emit_w=emit_w, n_side=len(side_cast)),
        grid=(t // tm, nj),
        in_specs=[pl.BlockSpec(memory_space=pl.ANY), pl.BlockSpec((1, d), lambda i, j: (0, 0)), w_spec] + side_specs,
        out_specs=out_specs + side_specs, out_shape=out_shape,
        scratch_shapes=[pltpu.VMEM((tm, d), BF16), pltpu.VMEM((tm, d), F32), pltpu.SemaphoreType.DMA((1,))],
        compiler_params=_params("arbitrary", "arbitrary"),
        name=name,
    )(x, g.reshape(1, d), w, *side_cast)
    return outs if len(outs) > 1 else outs[0]


LANES = 128


def _mem_kv_kernel(x_ref, g_ref, wk_ref, wv_ref, k_ref, v_ref, kc_ref, vc_ref, xn_ref, *, heads):
    head = pl.program_id(0)

    @pl.when(head == 0)
    def _():
        xn_ref[...] = _rms(x_ref[...], g_ref[...]).astype(BF16)

    rows = x_ref.shape[0]
    slabs = wk_ref.shape[1] // LANES
    for w_ref, rows_ref, cache_ref in ((wk_ref, k_ref, kc_ref), (wv_ref, v_ref, vc_ref)):
        o = jnp.dot(xn_ref[...], w_ref[...].astype(BF16), preferred_element_type=F32)
        rows_ref[...] = o.astype(rows_ref.dtype)
        for c in range(slabs):
            cache_ref[pl.ds(c * heads + head, rows, stride=slabs * heads), :] = o[:, c * LANES:(c + 1) * LANES]


def mem_kv(mem, g, w_k, w_v, *, heads, name):
    t, d = mem.shape
    hd = w_k.shape[1] // heads
    once = dict(pipeline_mode=pl.Buffered(1))
    w_spec = pl.BlockSpec((d, hd), lambda j: (0, j))
    rows_spec = pl.BlockSpec((t, hd), lambda j: (0, j))
    cache_spec = pl.BlockSpec((t * d // LANES, LANES), lambda j: (0, 0), **once)
    return pl.pallas_call(
        functools.partial(_mem_kv_kernel, heads=heads), grid=(heads,),
        in_specs=[pl.BlockSpec((t, d), lambda j: (0, 0), **once), pl.BlockSpec((1, d), lambda j: (0, 0)),
                  w_spec, w_spec],
        out_specs=[rows_spec, rows_spec, cache_spec, cache_spec],
        out_shape=[jax.ShapeDtypeStruct((t, d), BF16)] * 2 + [jax.ShapeDtypeStruct((t * d // LANES, LANES), F32)] * 2,
        scratch_shapes=[pltpu.VMEM((t, d), BF16)],
        compiler_params=_params("arbitrary"), name=name,
    )(mem, g.reshape(1, d), w_k, w_v)


def _cache_view(rows, n, m, heads, hd):
    a = rows.reshape(n * m, hd // LANES, heads, LANES)
    return a.transpose(0, 2, 1, 3).reshape(n, m, heads, hd)


def _mix_kernel(*refs, tm, gate_len, widths, groups, prompt, tiles_per_seq):
    if prompt:
        (x_ref, proj_ref, cgp_ref, xinp_ref, lng_ref, lnb_ref, wsp_ref, bsp_ref, cw_ref,
         wa_ref, wb_ref, wmix_ref, h_ref, tail_ref, gate_ref, gbias_ref) = refs
    else:
        (x_ref, proj_ref, e1_ref, e2_ref, lng_ref, lnb_ref, wsp_ref, bsp_ref, cw_ref,
         wa_ref, wb_ref, wmix_ref, h_ref, p_ref, v_ref, gate_ref, gbias_ref) = refs
    aw, bw, dm = widths
    o_u, o_v, o_bg, o_cg, o_xin, o_ga, o_gb = (0, aw, 2 * aw, 2 * aw + bw, 2 * aw + 2 * bw,
                                               2 * aw + 3 * bw, 2 * aw + 3 * bw + dm)

    def col(o, w):
        return proj_ref[:, o:o + w].astype(F32)

    gd = aw // groups

    @pl.when(pl.program_id(0) == 0)
    def _():
        row = lax.broadcasted_iota(jnp.int32, (tm, tm), 0)
        cidx = lax.broadcasted_iota(jnp.int32, (tm, tm), 1)
        keep = ((row ^ cidx) < gate_len) & (cidx <= row)
        cs = wsp_ref.shape[1]
        pick = (lax.broadcasted_iota(jnp.int32, (tm, cs), 1)
                == (lax.broadcasted_iota(jnp.int32, (tm, cs), 0) & (gate_len - 1)))
        pick_b = jnp.where(pick, 1.0, 0.0).astype(BF16)
        for g in range(groups):
            w_rows = jnp.dot(pick_b, wsp_ref[g].astype(BF16), preferred_element_type=F32).astype(BF16)
            w_full = lax.dot_general(w_rows, pick_b, (((1,), (1,)), ((), ())), preferred_element_type=F32)
            gate_ref[g] = jnp.where(keep, w_full, 0.0).astype(BF16)
            gbias_ref[g] = jnp.sum(jnp.where(pick, bsp_ref[g:g + 1, :], 0.0), axis=1, keepdims=True)

    p = col(o_cg, bw) * col(o_xin, bw)
    trow = lax.broadcasted_iota(jnp.int32, (tm, bw), 0)
    if prompt:
        fresh = pl.program_id(0) % tiles_per_seq == 0
        pp = cgp_ref[...].astype(F32) * xinp_ref[...].astype(F32)
        pp = jnp.where(fresh, 0.0, pp)
        last1 = pp[-1:, :]
        last2 = pp[-2:-1, :]
        e1 = jnp.where(trow == 0, last1, 0.0)
        e2 = jnp.where(trow == 0, last2, jnp.where(trow == 1, last1, 0.0))
        tail_ref[...] = p[tm - 8:, :]
    else:
        trow = trow & (gate_len - 1)
        e1 = e1_ref[...]
        e2 = e2_ref[...]
        p_ref[...] = p
    s1 = jnp.where(trow >= 1, pltpu.roll(p, 1, 0), 0.0) + e1
    s2 = jnp.where(trow >= 2, pltpu.roll(p, 2, 0), 0.0) + e2
    conv = cw_ref[0:1, :] * s2 + cw_ref[1:2, :] * s1 + cw_ref[2:3, :] * p
    y_b = (col(o_bg, bw) * conv).astype(BF16)
    branch_b = jax.nn.sigmoid(col(o_gb, dm)) * jnp.dot(y_b, wb_ref[...], preferred_element_type=F32)

    v = col(o_v, aw)
    mu = jnp.mean(v, axis=-1, keepdims=True)
    vc = v - mu
    var = jnp.mean(vc * vc, axis=-1, keepdims=True)
    vn = (vc * lax.rsqrt(var + EPS)) * lng_ref[...] + lnb_ref[...]
    if not prompt:
        v_ref[...] = vn
    vb = vn.astype(BF16)
    zs = [jnp.dot(gate_ref[g], vb[:, g * gd:(g + 1) * gd], preferred_element_type=F32) + gbias_ref[g]
          for g in range(groups)]
    y_a = (col(o_u, aw) * jnp.concatenate(zs, axis=1)).astype(BF16)
    branch_a = jax.nn.sigmoid(col(o_ga, dm)) * jnp.dot(y_a, wa_ref[...], preferred_element_type=F32)

    h_ref[...] = x_ref[...] + jnp.dot((branch_a + branch_b).astype(BF16), wmix_ref[...],
                                      preferred_element_type=F32)


def mix(x, proj, prev, ln_g, ln_b, w_sp, b_sp, conv_w, wa, wb, wmix, *, tm, gate_len, seq_len, name):
    t, dm = x.shape
    aw, bw = wa.shape[0], wb.shape[0]
    groups = w_sp.shape[0]
    prompt = prev is None
    nt = t // tm
    row_spec = lambda w: pl.BlockSpec((tm, w), lambda i: (i, 0))
    in_specs = [row_spec(dm), row_spec(proj.shape[1])]
    if prompt:
        prev_rows = 16
        cg_blk = (2 * aw + bw) // bw
        in_specs += [
            pl.BlockSpec((prev_rows, bw), lambda i: (jnp.maximum(i * (tm // prev_rows) - 1, 0), cg_blk)),
            pl.BlockSpec((prev_rows, bw), lambda i: (jnp.maximum(i * (tm // prev_rows) - 1, 0), cg_blk + 1)),
        ]
        extra = (proj, proj)
    else:
        in_specs += [row_spec(bw), row_spec(bw)]
        extra = prev
    in_specs += [_resident((1, aw)), _resident((1, aw)), _resident(w_sp.shape), _resident(b_sp.shape),
                 _resident(conv_w.shape), _resident(wa.shape), _resident(wb.shape), _resident(wmix.shape)]
    out_specs = [row_spec(dm)]
    out_shape = [jax.ShapeDtypeStruct((t, dm), F32)]
    if prompt:
        out_specs.append(pl.BlockSpec((8, bw), lambda i: (i, 0)))
        out_shape.append(jax.ShapeDtypeStruct((nt * 8, bw), F32))
    else:
        out_specs += [row_spec(bw), row_spec(aw)]
        out_shape += [jax.ShapeDtypeStruct((t, bw), F32), jax.ShapeDtypeStruct((t, aw), F32)]
    kern = functools.partial(_mix_kernel, tm=tm, gate_len=gate_len, widths=(aw, bw, dm), groups=groups,
                             prompt=prompt, tiles_per_seq=max(seq_len // tm, 1))
    return pl.pallas_call(
        kern, grid=(nt,), in_specs=in_specs, out_specs=out_specs, out_shape=out_shape,
        scratch_shapes=[pltpu.VMEM((groups, tm, tm), BF16), pltpu.VMEM((groups, tm, 1), F32)],
        compiler_params=_params("arbitrary"), name=name,
    )(x, proj, *extra, ln_g.reshape(1, aw), ln_b.reshape(1, aw), w_sp, b_sp, conv_w, wa, wb, wmix)


def _attn_block_kernel(h_ref, g_ref, wq_ref, k_ref, v_ref, wxo_ref, gn_ref, out_ref, hn_ref, *, heads, head_dim):
    scale = head_dim ** -0.5
    h = h_ref[...]
    q = jnp.dot(_rms(h, g_ref[...]).astype(BF16), wq_ref[...], preferred_element_type=F32).astype(BF16)
    head_cols = [slice(hd * head_dim, (hd + 1) * head_dim) for hd in range(heads)]
    scores = [lax.dot_general(q[:, cols], k_ref[0, :, cols].astype(BF16), (((1,), (1,)), ((), ())),
                              preferred_element_type=F32) * scale for cols in head_cols]
    probs = []
    for s in scores:
        e = jnp.exp(s - jnp.max(s, axis=-1, keepdims=True))
        probs.append((e / jnp.sum(e, axis=-1, keepdims=True)).astype(BF16))
    outs = [jnp.dot(p, v_ref[0, :, cols].astype(BF16), preferred_element_type=F32).astype(BF16)
            for p, cols in zip(probs, head_cols)]
    o = jnp.concatenate(outs, axis=1)
    _residual_and_norm(h, o, wxo_ref, gn_ref, out_ref, hn_ref)


def _residual_and_norm(h, o, wxo_ref, gn_ref, out_ref, hn_ref):
    out = h + jnp.dot(o, wxo_ref[...], preferred_element_type=F32)
    out_ref[...] = out
    hn_ref[...] = _rms(out, gn_ref[...]).astype(BF16)


def attn_block(h, g, wq, k, v, wxo, g_next, *, heads, tm, seq_len, name):
    t, dm = h.shape
    m = k.shape[1]
    per_seq = seq_len // tm
    row_spec = pl.BlockSpec((tm, dm), lambda i: (i, 0))
    kv_spec = pl.BlockSpec((1, m, dm), lambda i: (i // per_seq, 0, 0))
    kern = functools.partial(_attn_block_kernel, heads=heads, head_dim=dm // heads)
    return pl.pallas_call(
        kern, grid=(t // tm,),
        in_specs=[row_spec, _resident((1, dm)), _resident(wq.shape), kv_spec, kv_spec, _resident(wxo.shape),
                  _resident((1, dm))],
        out_specs=[row_spec, row_spec],
        out_shape=[jax.ShapeDtypeStruct((t, dm), F32), jax.ShapeDtypeStruct((t, dm), BF16)],
        compiler_params=_params("parallel"), name=name,
    )(h, g.reshape(1, dm), wq, k, v, wxo, g_next.reshape(1, dm))


def _attn_rows_kernel(q_ref, k_ref, v_ref, o_ref, *, n_seq, rows, heads, head_dim, n_mem):
    scale = head_dim ** -0.5
    slabs = head_dim // LANES
    per_seq = n_mem * slabs * heads
    width = n_mem * slabs
    pairs = [(b, h) for b in range(n_seq) for h in range(heads)]
    lane_slab = lax.broadcasted_iota(jnp.int32, (rows, width), 1) & (slabs - 1)
    is_slab = [lane_slab == c for c in range(slabs)]

    parts = []
    for b, h in pairs:
        r0, c0 = b * rows, h * head_dim
        qh = jnp.concatenate(
            [q_ref[r0:r0 + rows, c0 + c * LANES:c0 + (c + 1) * LANES] for c in range(slabs)], axis=0
        ).astype(BF16)
        kh = k_ref[pl.ds(b * per_seq + h, width, stride=heads), :].astype(BF16)
        g = lax.dot_general(qh, kh, (((1,), (1,)), ((), ())), preferred_element_type=F32)
        s = g[0:rows]
        for c in range(1, slabs):
            s = jnp.where(is_slab[c], g[c * rows:(c + 1) * rows], s)
        parts.append(s)
    s = jnp.concatenate(parts, axis=0)

    lane = lax.broadcasted_iota(jnp.int32, s.shape, 1)
    step = 1
    while step < slabs:
        s = s + jnp.where((lane & step) != 0, pltpu.roll(s, step, 1), pltpu.roll(s, width - step, 1))
        step *= 2
    s = s * scale
    e = jnp.exp(s - jnp.max(s, axis=-1, keepdims=True))
    p = e / (jnp.sum(e, axis=-1, keepdims=True) * (1.0 / slabs))

    for i, (b, h) in enumerate(pairs):
        r0, c0 = b * rows, h * head_dim
        ph = p[i * rows:(i + 1) * rows]
        w = jnp.concatenate([jnp.where(is_slab[c], ph, 0.0) for c in range(slabs)], axis=0).astype(BF16)
        vh = v_ref[pl.ds(b * per_seq + h, width, stride=heads), :].astype(BF16)
        o = jnp.dot(w, vh, preferred_element_type=F32)
        for c in range(slabs):
            o_ref[r0:r0 + rows, c0 + c * LANES:c0 + (c + 1) * LANES] = o[c * rows:(c + 1) * rows]


def _rows_view(a):
    n, m, heads, hd = a.shape
    a = a.reshape(n * m, heads, hd // LANES, LANES)
    return a.transpose(0, 2, 1, 3).reshape(n * m * hd // LANES * heads, LANES)


def attn_rows(q, k, v, *, n_seq, name):
    n, n_mem, heads, hd = k.shape
    t, dm = q.shape
    rows = t // n
    slabs = hd // LANES
    assert slabs & (slabs - 1) == 0 and rows % 8 == 0
    per_seq = n_mem * slabs * heads
    kv_spec = pl.BlockSpec((n_seq * per_seq, LANES), lambda i: (i, 0))
    q_spec = pl.BlockSpec((n_seq * rows, dm), lambda i: (i, 0))
    kern = functools.partial(_attn_rows_kernel, n_seq=n_seq, rows=rows, heads=heads, head_dim=hd, n_mem=n_mem)
    return pl.pallas_call(
        kern, grid=(n // n_seq,),
        in_specs=[q_spec, kv_spec, kv_spec], out_specs=q_spec,
        out_shape=jax.ShapeDtypeStruct((t, dm), F32),
        compiler_params=_params("parallel"), name=name,
    )(q, _rows_view(k), _rows_view(v))


def _attn_out_kernel(h_ref, o_ref, wxo_ref, gn_ref, out_ref, hn_ref):
    _residual_and_norm(h_ref[...], o_ref[...].astype(BF16), wxo_ref, gn_ref, out_ref, hn_ref)


def attn_out(h, o, wxo, g_next, *, tm, name):
    t, dm = h.shape
    row_spec = pl.BlockSpec((tm, dm), lambda i: (i, 0))
    return pl.pallas_call(
        _attn_out_kernel, grid=(t // tm,),
        in_specs=[row_spec, row_spec, _resident(wxo.shape), _resident((1, dm))],
        out_specs=[row_spec, row_spec],
        out_shape=[jax.ShapeDtypeStruct((t, dm), F32), jax.ShapeDtypeStruct((t, dm), BF16)],
        compiler_params=_params("parallel"), name=name,
    )(h, o, wxo, g_next.reshape(1, dm))


def _mlp_kernel(hn_ref, h_hbm, wup_ref, wdn_ref, gf_ref, y_ref, hbuf, sem, *, tm):
    i, j = pl.program_id(0), pl.program_id(1)

    def residual_rows(tile):
        return pltpu.make_async_copy(h_hbm.at[pl.ds(tile * tm, tm), :], hbuf, sem.at[0])

    def up():
        a = jnp.dot(hn_ref[...], wup_ref[...], preferred_element_type=F32)
        return jnp.square(jnp.maximum(a, 0.0)).astype(BF16)

    @pl.when((i == 0) & (j == 0))
    def _():
        residual_rows(0).start()

    @pl.when(j == 0)
    def _():
        a = up()
        residual_rows(i).wait()
        y_ref[...] = hbuf[...] + jnp.dot(a, wdn_ref[...], preferred_element_type=F32)

        @pl.when(i + 1 < pl.num_programs(0))
        def _():
            residual_rows(i + 1).start()

    @pl.when(j > 0)
    def _():
        y_ref[...] += jnp.dot(up(), wdn_ref[...], preferred_element_type=F32)

    @pl.when(j == pl.num_programs(1) - 1)
    def _():
        y_ref[...] = _rms(y_ref[...], gf_ref[...])


def mlp(hn, h, w_up, w_down, g_final, *, tm, fc, name):
    t, dm = h.shape
    dff = w_up.shape[1]
    return pl.pallas_call(
        functools.partial(_mlp_kernel, tm=tm), grid=(t // tm, dff // fc),
        in_specs=[pl.BlockSpec((tm, dm), lambda i, j: (i, 0)),
                  pl.BlockSpec(memory_space=pl.ANY),
                  pl.BlockSpec((dm, fc), lambda i, j: (0, j)),
                  pl.BlockSpec((fc, dm), lambda i, j: (j, 0)),
                  pl.BlockSpec((1, dm), lambda i, j: (0, 0))],
        out_specs=pl.BlockSpec((tm, dm), lambda i, j: (i, 0)),
        out_shape=jax.ShapeDtypeStruct((t, dm), F32),
        scratch_shapes=[pltpu.VMEM((tm, dm), F32), pltpu.SemaphoreType.DMA((1,))],
        compiler_params=_params("arbitrary", "arbitrary"), name=name,
    )(hn, h, w_up, w_down, g_final.reshape(1, dm))


MIX_TM = 256
ROW_TM = 1024
COL_TN = 1024
WIDE_TN = 1536
ATTN_TM = 512
MLP_FC = 1024


def _gate_len(seq_len, w_spatial):
    gate_len = CHUNK if seq_len % CHUNK == 0 else seq_len
    assert gate_len & (gate_len - 1) == 0 and MIX_TM % gate_len == 0 and gate_len <= w_spatial.shape[1]
    return gate_len


def kernel(x_prompt, x_sample, state_conv, cache_mem_k, cache_mem_v, mem_prompt, norm_mix_g, w_in, ln_v_g, ln_v_b, w_spatial, b_spatial, conv_w, w_branch_a, w_branch_b, w_mix_out, norm_x_g, norm_mem_g, w_q, w_k, w_v, w_x_out, norm_mlp_g, w_up, w_down, norm_final_g):
    depth = w_in.shape[0]
    assert depth == 1, "the final rmsnorm is fused into the single layer's MLP kernel"
    nb, seq, dm = x_prompt.shape
    nd, dseq, _ = x_sample.shape
    n_mem, heads, hd = cache_mem_k.shape[2:]
    l = 0
    xs = x_sample.reshape(nd * dseq, dm)
    xp = x_prompt.reshape(nb * seq, dm)
    assert xs.shape[0] == ROW_TM

    proj_s, w_in_b = rms_matmul(xs, norm_mix_g[l], w_in[l], BF16, tm=ROW_TM, tn=COL_TN, emit_w=True,
                                name="in_proj_sample")
    proj_p, w_a, w_b, w_mix, w_q_b, w_xo, w_up_b, w_down_b = rms_matmul(
        xp, norm_mix_g[l], w_in_b, BF16, tm=ROW_TM, tn=WIDE_TN, name="in_proj_prompt",
        side_cast=(w_branch_a[l], w_branch_b[l], w_mix_out[l], w_q[l], w_x_out[l], w_up[l], w_down[l]))
    mix_w = (ln_v_g[l], ln_v_b[l], w_spatial[l], b_spatial[l], conv_w[l], w_a, w_b, w_mix)

    prev = state_conv[l]
    e1 = jnp.pad(prev[:, 1:2], ((0, 0), (0, dseq - 1), (0, 0))).reshape(nd * dseq, -1)
    e2 = jnp.pad(prev, ((0, 0), (0, dseq - 2), (0, 0))).reshape(nd * dseq, -1)
    h_s, p_s, vn_s = mix(xs, proj_s, (e1, e2), *mix_w, tm=MIX_TM, gate_len=_gate_len(dseq, w_spatial[l]),
                         seq_len=dseq, name="mix_sample")
    q_s = rms_matmul(h_s, norm_x_g[l], w_q_b, F32, tm=ROW_TM, tn=COL_TN, name="q_proj_sample")
    o_s = attn_rows(q_s, cache_mem_k[l], cache_mem_v[l], n_seq=4, name="attn_sample")
    h_s, hn_s = attn_out(h_s, o_s, w_xo, norm_mlp_g[l], tm=ATTN_TM, name="x_out_sample")
    y_s = mlp(hn_s, h_s, w_up_b, w_down_b, norm_final_g, tm=ROW_TM, fc=MLP_FC, name="mlp_sample")

    mem = mem_prompt.reshape(nb * n_mem, dm)
    k_b, v_b, k_rows, v_rows = mem_kv(mem, norm_mem_g[l], w_k[l], w_v[l], heads=heads, name="mem_kv")
    h_p, tail_p = mix(xp, proj_p, None, *mix_w, tm=MIX_TM, gate_len=_gate_len(seq, w_spatial[l]),
                      seq_len=seq, name="mix_prompt")
    h_p, hn_p = attn_block(h_p, norm_x_g[l], w_q_b, k_b.reshape(nb, n_mem, dm), v_b.reshape(nb, n_mem, dm), w_xo,
                           norm_mlp_g[l], heads=heads, tm=ATTN_TM, seq_len=seq, name="attn_block_prompt")
    y_p = mlp(hn_p, h_p, w_up_b, w_down_b, norm_final_g, tm=ROW_TM, fc=MLP_FC, name="mlp_prompt")
    k_p = _cache_view(k_rows, nb, n_mem, heads, hd)
    v_p = _cache_view(v_rows, nb, n_mem, heads, hd)

    bw = p_s.shape[1]
    keep = conv_w.shape[1] - 1
    conv_p = tail_p.reshape(nb, seq // MIX_TM, 8, bw)[:, -1, 8 - keep:, :]
    conv_s = p_s.reshape(nd, dseq, bw)[:, dseq - keep:, :]
    return (y_p.reshape(nb, seq, dm), y_s.reshape(nd, dseq, dm), k_p[None], v_p[None],
            conv_p[None], conv_s[None], vn_s.reshape(1, nd, dseq, -1))
```

```python
import functools

import jax
import jax.numpy as jnp
from jax import lax
from jax.experimental import pallas as pl
from jax.experimental.pallas import tpu as pltpu

EPS = 1e-6
CHUNK = 128
F32 = jnp.float32
BF16 = jnp.bfloat16

V7X_VMEM_BYTES = 64 * 1024 * 1024
VMEM_LIMIT = V7X_VMEM_BYTES * 7 // 8


def _params(*sem):
    return pltpu.CompilerParams(dimension_semantics=sem, vmem_limit_bytes=VMEM_LIMIT)


def _rms(x, g):
    r = lax.rsqrt(jnp.mean(x * x, axis=-1, keepdims=True) + EPS)
    return (x * r) * g


def _resident(shape):
    return pl.BlockSpec(shape, lambda *_: (0,) * len(shape), pipeline_mode=pl.Buffered(1))


def _bf16_tile(w_ref, wb_ref):
    if wb_ref is None:
        return w_ref[...].astype(BF16)
    wb_ref[...] = w_ref[...].astype(BF16)
    return wb_ref[...]


def _rms_matmul_kernel(x_hbm, g_ref, w_ref, *rest, emit_w, n_side):
    side_in, rest = rest[:n_side], rest[n_side:]
    o_ref, rest = rest[0], rest[1:]
    wb_ref, rest = (rest[0], rest[1:]) if emit_w else (None, rest)
    side_out, (xn_ref, xbuf, sem) = rest[:n_side], rest[n_side:]
    i, j = pl.program_id(0), pl.program_id(1)
    tm = xbuf.shape[0]

    def x_rows(tile):
        return pltpu.make_async_copy(x_hbm.at[pl.ds(tile * tm, tm), :], xbuf, sem.at[0])

    @pl.when((i == 0) & (j == 0))
    def _():
        x_rows(0).start()

    @pl.when(j == 0)
    def _():
        x_rows(i).wait()
        xn_ref[...] = _rms(xbuf[...], g_ref[...]).astype(BF16)

        @pl.when(i + 1 < pl.num_programs(0))
        def _():
            x_rows(i + 1).start()

    w = _bf16_tile(w_ref, wb_ref)
    o_ref[...] = jnp.dot(xn_ref[...], w, preferred_element_type=F32).astype(o_ref.dtype)
    for src, dst in zip(side_in, side_out):
        dst[...] = src[...].astype(BF16)


BF16_SUBLANES = 16


def _side_chunks(rows, steps):
    return max(n for n in range(1, steps + 1) if rows % n == 0 and (rows // n) % BF16_SUBLANES == 0)


def rms_matmul(x, g, w, out_dtype, *, tm, tn, name, emit_w=False, side_cast=()):
    t, d = x.shape
    n = w.shape[1]
    assert t == tm or not emit_w
    nj = n // tn
    steps = (t // tm) * nj
    w_spec = pl.BlockSpec((d, tn), lambda i, j: (0, j))
    out_specs = [pl.BlockSpec((tm, tn), lambda i, j: (i, j))]
    out_shape = [jax.ShapeDtypeStruct((t, n), out_dtype)]
    if emit_w:
        out_specs.append(w_spec)
        out_shape.append(jax.ShapeDtypeStruct(w.shape, BF16))
    side_specs = []
    for s in side_cast:
        chunks = _side_chunks(s.shape[0], steps)
        side_specs.append(pl.BlockSpec((s.shape[0] // chunks, s.shape[1]),
                                       lambda i, j, c=chunks: (jnp.minimum(i * nj + j, c - 1), 0)))
        out_shape.append(jax.ShapeDtypeStruct(s.shape, BF16))
    outs = pl.pallas_call(
        functools.partial(_rms_matmul_kernel, emit_w=emit_w, n_side=len(side_cast)),
        grid=(t // tm, nj),
        in_specs=[pl.BlockSpec(memory_space=pl.ANY), pl.BlockSpec((1, d), lambda i, j: (0, 0)), w_spec] + side_specs,
        out_specs=out_specs + side_specs, out_shape=out_shape,
        scratch_shapes=[pltpu.VMEM((tm, d), BF16), pltpu.VMEM((tm, d), F32), pltpu.SemaphoreType.DMA((1,))],
        compiler_params=_params("arbitrary", "arbitrary"),
        name=name,
    )(x, g.reshape(1, d), w, *side_cast)
    return outs if len(outs) > 1 else outs[0]


LANES = 128


def _mem_kv_kernel(x_ref, g_ref, wk_ref, wv_ref, k_ref, v_ref, kc_ref, vc_ref, xn_ref, *, heads):
    head = pl.program_id(0)

    @pl.when(head == 0)
    def _():
        xn_ref[...] = _rms(x_ref[...], g_ref[...]).astype(BF16)

    rows = x_ref.shape[0]
    slabs = wk_ref.shape[1] // LANES
    for w_ref, rows_ref, cache_ref in ((wk_ref, k_ref, kc_ref), (wv_ref, v_ref, vc_ref)):
        o = jnp.dot(xn_ref[...], w_ref[...].astype(BF16), preferred_element_type=F32)
        rows_ref[...] = o.astype(rows_ref.dtype)
        for c in range(slabs):
            cache_ref[pl.ds(c * heads + head, rows, stride=slabs * heads), :] = o[:, c * LANES:(c + 1) * LANES]


def mem_kv(mem, g, w_k, w_v, *, heads, name):
    t, d = mem.shape
    hd = w_k.shape[1] // heads
    once = dict(pipeline_mode=pl.Buffered(1))
    w_spec = pl.BlockSpec((d, hd), lambda j: (0, j))
    rows_spec = pl.BlockSpec((t, hd), lambda j: (0, j))
    cache_spec = pl.BlockSpec((t * d // LANES, LANES), lambda j: (0, 0), **once)
    return pl.pallas_call(
        functools.partial(_mem_kv_kernel, heads=heads), grid=(heads,),
        in_specs=[pl.BlockSpec((t, d), lambda j: (0, 0), **once), pl.BlockSpec((1, d), lambda j: (0, 0)),
                  w_spec, w_spec],
        out_specs=[rows_spec, rows_spec, cache_spec, cache_spec],
        out_shape=[jax.ShapeDtypeStruct((t, d), BF16)] * 2 + [jax.ShapeDtypeStruct((t * d // LANES, LANES), F32)] * 2,
        scratch_shapes=[pltpu.VMEM((t, d), BF16)],
        compiler_params=_params("arbitrary"), name=name,
    )(mem, g.reshape(1, d), w_k, w_v)


def _cache_view(rows, n, m, heads, hd):
    a = rows.reshape(n * m, hd // LANES, heads, LANES)
    return a.transpose(0, 2, 1, 3).reshape(n, m, heads, hd)


def _mix_kernel(*refs, tm, gate_len, widths, groups, prompt, tiles_per_seq):
    if prompt:
        (x_ref, proj_ref, cgp_ref, xinp_ref, lng_ref, lnb_ref, wsp_ref, bsp_ref, cw_ref,
         wa_ref, wb_ref, wmix_ref, h_ref, tail_ref, gate_ref, gbias_ref) = refs
    else:
        (x_ref, proj_ref, e1_ref, e2_ref, lng_ref, lnb_ref, wsp_ref, bsp_ref, cw_ref,
         wa_ref, wb_ref, wmix_ref, h_ref, p_ref, v_ref, gate_ref, gbias_ref) = refs
    aw, bw, dm = widths
    o_u, o_v, o_bg, o_cg, o_xin, o_ga, o_gb = (0, aw, 2 * aw, 2 * aw + bw, 2 * aw + 2 * bw,
                                               2 * aw + 3 * bw, 2 * aw + 3 * bw + dm)

    def col(o, w):
        return proj_ref[:, o:o + w].astype(F32)

    gd = aw // groups

    @pl.when(pl.program_id(0) == 0)
    def _():
        row = lax.broadcasted_iota(jnp.int32, (tm, tm), 0)
        cidx = lax.broadcasted_iota(jnp.int32, (tm, tm), 1)
        keep = ((row ^ cidx) < gate_len) & (cidx <= row)
        cs = wsp_ref.shape[1]
        pick = (lax.broadcasted_iota(jnp.int32, (tm, cs), 1)
                == (lax.broadcasted_iota(jnp.int32, (tm, cs), 0) & (gate_len - 1)))
        pick_b = jnp.where(pick, 1.0, 0.0).astype(BF16)
        for g in range(groups):
            w_rows = jnp.dot(pick_b, wsp_ref[g].astype(BF16), preferred_element_type=F32).astype(BF16)
            w_full = lax.dot_general(w_rows, pick_b, (((1,), (1,)), ((), ())), preferred_element_type=F32)
            gate_ref[g] = jnp.where(keep, w_full, 0.0).astype(BF16)
            gbias_ref[g] = jnp.sum(jnp.where(pick, bsp_ref[g:g + 1, :], 0.0), axis=1, keepdims=True)

    p = col(o_cg, bw) * col(o_xin, bw)
    trow = lax.broadcasted_iota(jnp.int32, (tm, bw), 0)
    if prompt:
        fresh = pl.program_id(0) % tiles_per_seq == 0
        pp = cgp_ref[...].astype(F32) * xinp_ref[...].astype(F32)
        pp = jnp.where(fresh, 0.0, pp)
        last1 = pp[-1:, :]
        last2 = pp[-2:-1, :]
        e1 = jnp.where(trow == 0, last1, 0.0)
        e2 = jnp.where(trow == 0, last2, jnp.where(trow == 1, last1, 0.0))
        tail_ref[...] = p[tm - 8:, :]
    else:
        trow = trow & (gate_len - 1)
        e1 = e1_ref[...]
        e2 = e2_ref[...]
        p_ref[...] = p
    s1 = jnp.where(trow >= 1, pltpu.roll(p, 1, 0), 0.0) + e1
    s2 = jnp.where(trow >= 2, pltpu.roll(p, 2, 0), 0.0) + e2
    conv = cw_ref[0:1, :] * s2 + cw_ref[1:2, :] * s1 + cw_ref[2:3, :] * p
    y_b = (col(o_bg, bw) * conv).astype(BF16)
    branch_b = jax.nn.sigmoid(col(o_gb, dm)) * jnp.dot(y_b, wb_ref[...], preferred_element_type=F32)

    v = col(o_v, aw)
    mu = jnp.mean(v, axis=-1, keepdims=True)
    vc = v - mu
    var = jnp.mean(vc * vc, axis=-1, keepdims=True)
    vn = (vc * lax.rsqrt(var + EPS)) * lng_ref[...] + lnb_ref[...]
    if not prompt:
        v_ref[...] = vn
    vb = vn.astype(BF16)
    zs = [jnp.dot(gate_ref[g], vb[:, g * gd:(g + 1) * gd], preferred_element_type=F32) + gbias_ref[g]
          for g in range(groups)]
    y_a = (col(o_u, aw) * jnp.concatenate(zs, axis=1)).astype(BF16)
    branch_a = jax.nn.sigmoid(col(o_ga, dm)) * jnp.dot(y_a, wa_ref[...], preferred_element_type=F32)

    h_ref[...] = x_ref[...] + jnp.dot((branch_a + branch_b).astype(BF16), wmix_ref[...],
                                      preferred_element_type=F32)


def mix(x, proj, prev, ln_g, ln_b, w_sp, b_sp, conv_w, wa, wb, wmix, *, tm, gate_len, seq_len, name):
    t, dm = x.shape
    aw, bw = wa.shape[0], wb.shape[0]
    groups = w_sp.shape[0]
    prompt = prev is None
    nt = t // tm
    row_spec = lambda w: pl.BlockSpec((tm, w), lambda i: (i, 0))
    in_specs = [row_spec(dm), row_spec(proj.shape[1])]
    if prompt:
        prev_rows = 16
        cg_blk = (2 * aw + bw) // bw
        in_specs += [
            pl.BlockSpec((prev_rows, bw), lambda i: (jnp.maximum(i * (tm // prev_rows) - 1, 0), cg_blk)),
            pl.BlockSpec((prev_rows, bw), lambda i: (jnp.maximum(i * (tm // prev_rows) - 1, 0), cg_blk + 1)),
        ]
        extra = (proj, proj)
    else:
        in_specs += [row_spec(bw), row_spec(bw)]
        extra = prev
    in_specs += [_resident((1, aw)), _resident((1, aw)), _resident(w_sp.shape), _resident(b_sp.shape),
                 _resident(conv_w.shape), _resident(wa.shape), _resident(wb.shape), _resident(wmix.shape)]
    out_specs = [row_spec(dm)]
    out_shape = [jax.ShapeDtypeStruct((t, dm), F32)]
    if prompt:
        out_specs.append(pl.BlockSpec((8, bw), lambda i: (i, 0)))
        out_shape.append(jax.ShapeDtypeStruct((nt * 8, bw), F32))
    else:
        out_specs += [row_spec(bw), row_spec(aw)]
        out_shape += [jax.ShapeDtypeStruct((t, bw), F32), jax.ShapeDtypeStruct((t, aw), F32)]
    kern = functools.partial(_mix_kernel, tm=tm, gate_len=gate_len, widths=(aw, bw, dm), groups=groups,
                             prompt=prompt, tiles_per_seq=max(seq_len // tm, 1))
    return pl.pallas_call(
        kern, grid=(nt,), in_specs=in_specs, out_specs=out_specs, out_shape=out_shape,
        scratch_shapes=[pltpu.VMEM((groups, tm, tm), BF16), pltpu.VMEM((groups, tm, 1), F32)],
        compiler_params=_params("arbitrary"), name=name,
    )(x, proj, *extra, ln_g.reshape(1, aw), ln_b.reshape(1, aw), w_sp, b_sp, conv_w, wa, wb, wmix)


def _attn_block_kernel(h_ref, g_ref, wq_ref, k_ref, v_ref, wxo_ref, gn_ref, out_ref, hn_ref, *, heads, head_dim):
    scale = head_dim ** -0.5
    h = h_ref[...]
    q = jnp.dot(_rms(h, g_ref[...]).astype(BF16), wq_ref[...], preferred_element_type=F32).astype(BF16)
    head_cols = [slice(hd * head_dim, (hd + 1) * head_dim) for hd in range(heads)]
    scores = [lax.dot_general(q[:, cols], k_ref[0, :, cols].astype(BF16), (((1,), (1,)), ((), ())),
                              preferred_element_type=F32) * scale for cols in head_cols]
    probs = []
    for s in scores:
        e = jnp.exp(s - jnp.max(s, axis=-1, keepdims=True))
        probs.append((e / jnp.sum(e, axis=-1, keepdims=True)).astype(BF16))
    outs = [jnp.dot(p, v_ref[0, :, cols].astype(BF16), preferred_element_type=F32).astype(BF16)
            for p, cols in zip(probs, head_cols)]
    o = jnp.concatenate(outs, axis=1)
    _residual_and_norm(h, o, wxo_ref, gn_ref, out_ref, hn_ref)


def _residual_and_norm(h, o, wxo_ref, gn_ref, out_ref, hn_ref):
    out = h + jnp.dot(o, wxo_ref[...], preferred_element_type=F32)
    out_ref[...] = out
    hn_ref[...] = _rms(out, gn_ref[...]).astype(BF16)


def attn_block(h, g, wq, k, v, wxo, g_next, *, heads, tm, seq_len, name):
    t, dm = h.shape
    m = k.shape[1]
    per_seq = seq_len // tm
    row_spec = pl.BlockSpec((tm, dm), lambda i: (i, 0))
    kv_spec = pl.BlockSpec((1, m, dm), lambda i: (i // per_seq, 0, 0))
    kern = functools.partial(_attn_block_kernel, heads=heads, head_dim=dm // heads)
    return pl.pallas_call(
        kern, grid=(t // tm,),
        in_specs=[row_spec, _resident((1, dm)), _resident(wq.shape), kv_spec, kv_spec, _resident(wxo.shape),
                  _resident((1, dm))],
        out_specs=[row_spec, row_spec],
        out_shape=[jax.ShapeDtypeStruct((t, dm), F32), jax.ShapeDtypeStruct((t, dm), BF16)],
        compiler_params=_params("parallel"), name=name,
    )(h, g.reshape(1, dm), wq, k, v, wxo, g_next.reshape(1, dm))


def _attn_rows_kernel(q_ref, k_ref, v_ref, o_ref, *, n_seq, rows, heads, head_dim, n_mem):
    scale = head_dim ** -0.5
    slabs = head_dim // LANES
    per_seq = n_mem * slabs * heads
    width = n_mem * slabs
    pairs = [(b, h) for b in range(n_seq) for h in range(heads)]
    lane_slab = lax.broadcasted_iota(jnp.int32, (rows, width), 1) & (slabs - 1)
    is_slab = [lane_slab == c for c in range(slabs)]

    parts = []
    for b, h in pairs:
        r0, c0 = b * rows, h * head_dim
        qh = jnp.concatenate(
            [q_ref[r0:r0 + rows, c0 + c * LANES:c0 + (c + 1) * LANES] for c in range(slabs)], axis=0
        ).astype(BF16)
        kh = k_ref[pl.ds(b * per_seq + h, width, stride=heads), :].astype(BF16)
        g = lax.dot_general(qh, kh, (((1,), (1,)), ((), ())), preferred_element_type=F32)
        s = g[0:rows]
        for c in range(1, slabs):
            s = jnp.where(is_slab[c], g[c * rows:(c + 1) * rows], s)
        parts.append(s)
    s = jnp.concatenate(parts, axis=0)

    lane = lax.broadcasted_iota(jnp.int32, s.shape, 1)
    step = 1
    while step < slabs:
        s = s + jnp.where((lane & step) != 0, pltpu.roll(s, step, 1), pltpu.roll(s, width - step, 1))
        step *= 2
    s = s * scale
    e = jnp.exp(s - jnp.max(s, axis=-1, keepdims=True))
    p = e / (jnp.sum(e, axis=-1, keepdims=True) * (1.0 / slabs))

    for i, (b, h) in enumerate(pairs):
        r0, c0 = b * rows, h * head_dim
        ph = p[i * rows:(i + 1) * rows]
        w = jnp.concatenate([jnp.where(is_slab[c], ph, 0.0) for c in range(slabs)], axis=0).astype(BF16)
        vh = v_ref[pl.ds(b * per_seq + h, width, stride=heads), :].astype(BF16)
        o = jnp.dot(w, vh, preferred_element_type=F32)
        for c in range(slabs):
            o_ref[r0:r0 + rows, c0 + c * LANES:c0 + (c + 1) * LANES] = o[c * rows:(c + 1) * rows]


def _rows_view(a):
    n, m, heads, hd = a.shape
    a = a.reshape(n * m, heads, hd // LANES, LANES)
    return a.transpose(0, 2, 1, 3).reshape(n * m * hd // LANES * heads, LANES)


def attn_rows(q, k, v, *, n_seq, name):
    n, n_mem, heads, hd = k.shape
    t, dm = q.shape
    rows = t // n
    slabs = hd // LANES
    assert slabs & (slabs - 1) == 0 and rows % 8 == 0
    per_seq = n_mem * slabs * heads
    kv_spec = pl.BlockSpec((n_seq * per_seq, LANES), lambda i: (i, 0))
    q_spec = pl.BlockSpec((n_seq * rows, dm), lambda i: (i, 0))
    kern = functools.partial(_attn_rows_kernel, n_seq=n_seq, rows=rows, heads=heads, head_dim=hd, n_mem=n_mem)
    return pl.pallas_call(
        kern, grid=(n // n_seq,),
        in_specs=[q_spec, kv_spec, kv_spec], out_specs=q_spec,
        out_shape=jax.ShapeDtypeStruct((t, dm), F32),
        compiler_params=_params("parallel"), name=name,
    )(q, _rows_view(k), _rows_view(v))


def _attn_out_kernel(h_ref, o_ref, wxo_ref, gn_ref, out_ref, hn_ref):
    _residual_and_norm(h_ref[...], o_ref[...].astype(BF16), wxo_ref, gn_ref, out_ref, hn_ref)


def attn_out(h, o, wxo, g_next, *, tm, name):
    t, dm = h.shape
    row_spec = pl.BlockSpec((tm, dm), lambda i: (i, 0))
    return pl.pallas_call(
        _attn_out_kernel, grid=(t // tm,),
        in_specs=[row_spec, row_spec, _resident(wxo.shape), _resident((1, dm))],
        out_specs=[row_spec, row_spec],
        out_shape=[jax.ShapeDtypeStruct((t, dm), F32), jax.ShapeDtypeStruct((t, dm), BF16)],
        compiler_params=_params("parallel"), name=name,
    )(h, o, wxo, g_next.reshape(1, dm))


def _mlp_kernel(hn_ref, h_hbm, wup_hbm, wdn_hbm, gf_ref, y_ref, hbuf, wup_buf, wdn_buf, sem, *, tm, fc):
    i, n = pl.program_id(0), pl.num_programs(0)
    chunks = wup_hbm.shape[1] // fc
    assert chunks % 2 == 0
    following = jnp.minimum(i + 1, n - 1)

    def residual_rows(tile):
        return pltpu.make_async_copy(h_hbm.at[pl.ds(tile * tm, tm), :], hbuf, sem.at[0])

    def weight_tiles(c):
        slot = c % 2
        return (pltpu.make_async_copy(wup_hbm.at[:, pl.ds(c * fc, fc)], wup_buf.at[slot], sem.at[1 + slot]),
                pltpu.make_async_copy(wdn_hbm.at[pl.ds(c * fc, fc), :], wdn_buf.at[slot], sem.at[3 + slot]))

    @pl.when(i == 0)
    def _():
        residual_rows(0).start()
        for copy in weight_tiles(0):
            copy.start()

    for c in range(chunks):
        for copy in weight_tiles((c + 1) % chunks):
            copy.start()
        for copy in weight_tiles(c):
            copy.wait()
        a = jnp.dot(hn_ref[...], wup_buf[c % 2], preferred_element_type=F32)
        a = jnp.square(jnp.maximum(a, 0.0)).astype(BF16)
        down = jnp.dot(a, wdn_buf[c % 2], preferred_element_type=F32)
        if c == 0:
            residual_rows(i).wait()
            y_ref[...] = hbuf[...] + down
            residual_rows(following).start()
        else:
            y_ref[...] += down
    y_ref[...] = _rms(y_ref[...], gf_ref[...])

    @pl.when(i == n - 1)
    def _():
        residual_rows(following).wait()
        for copy in weight_tiles(0):
            copy.wait()


def mlp(hn, h, w_up, w_down, g_final, *, tm, fc, name):
    t, dm = h.shape
    return pl.pallas_call(
        functools.partial(_mlp_kernel, tm=tm, fc=fc), grid=(t // tm,),
        in_specs=[pl.BlockSpec((tm, dm), lambda i: (i, 0)),
                  pl.BlockSpec(memory_space=pl.ANY), pl.BlockSpec(memory_space=pl.ANY),
                  pl.BlockSpec(memory_space=pl.ANY),
                  pl.BlockSpec((1, dm), lambda i: (0, 0))],
        out_specs=pl.BlockSpec((tm, dm), lambda i: (i, 0)),
        out_shape=jax.ShapeDtypeStruct((t, dm), F32),
        scratch_shapes=[pltpu.VMEM((tm, dm), F32), pltpu.VMEM((2, dm, fc), BF16), pltpu.VMEM((2, fc, dm), BF16),
                        pltpu.SemaphoreType.DMA((5,))],
        compiler_params=_params("arbitrary"), name=name,
    )(hn, h, w_up, w_down, g_final.reshape(1, dm))


MIX_TM = 256
ROW_TM = 1024
COL_TN = 1024
WIDE_TN = 1536
ATTN_TM = 512
MLP_FC = 512


def _gate_len(seq_len, w_spatial):
    gate_len = CHUNK if seq_len % CHUNK == 0 else seq_len
    assert gate_len & (gate_len - 1) == 0 and MIX_TM % gate_len == 0 and gate_len <= w_spatial.shape[1]
    return gate_len


def kernel(x_prompt, x_sample, state_conv, cache_mem_k, cache_mem_v, mem_prompt, norm_mix_g, w_in, ln_v_g, ln_v_b, w_spatial, b_spatial, conv_w, w_branch_a, w_branch_b, w_mix_out, norm_x_g, norm_mem_g, w_q, w_k, w_v, w_x_out, norm_mlp_g, w_up, w_down, norm_final_g):
    depth = w_in.shape[0]
    assert depth == 1, "the final rmsnorm is fused into the single layer's MLP kernel"
    nb, seq, dm = x_prompt.shape
    nd, dseq, _ = x_sample.shape
    n_mem, heads, hd = cache_mem_k.shape[2:]
    l = 0
    xs = x_sample.reshape(nd * dseq, dm)
    xp = x_prompt.reshape(nb * seq, dm)
    assert xs.shape[0] == ROW_TM

    proj_s, w_in_b = rms_matmul(xs, norm_mix_g[l], w_in[l], BF16, tm=ROW_TM, tn=COL_TN, emit_w=True,
                                name="in_proj_sample")
    proj_p, w_a, w_b, w_mix, w_q_b, w_xo, w_up_b, w_down_b = rms_matmul(
        xp, norm_mix_g[l], w_in_b, BF16, tm=ROW_TM, tn=WIDE_TN, name="in_proj_prompt",
        side_cast=(w_branch_a[l], w_branch_b[l], w_mix_out[l], w_q[l], w_x_out[l], w_up[l], w_down[l]))
    mix_w = (ln_v_g[l], ln_v_b[l], w_spatial[l], b_spatial[l], conv_w[l], w_a, w_b, w_mix)

    prev = state_conv[l]
    e1 = jnp.pad(prev[:, 1:2], ((0, 0), (0, dseq - 1), (0, 0))).reshape(nd * dseq, -1)
    e2 = jnp.pad(prev, ((0, 0), (0, dseq - 2), (0, 0))).reshape(nd * dseq, -1)
    h_s, p_s, vn_s = mix(xs, proj_s, (e1, e2), *mix_w, tm=MIX_TM, gate_len=_gate_len(dseq, w_spatial[l]),
                         seq_len=dseq, name="mix_sample")
    q_s = rms_matmul(h_s, norm_x_g[l], w_q_b, F32, tm=ROW_TM, tn=COL_TN, name="q_proj_sample")
    o_s = attn_rows(q_s, cache_mem_k[l], cache_mem_v[l], n_seq=4, name="attn_sample")
    h_s, hn_s = attn_out(h_s, o_s, w_xo, norm_mlp_g[l], tm=ATTN_TM, name="x_out_sample")
    y_s = mlp(hn_s, h_s, w_up_b, w_down_b, norm_final_g, tm=ROW_TM, fc=MLP_FC, name="mlp_sample")

    mem = mem_prompt.reshape(nb * n_mem, dm)
    k_b, v_b, k_rows, v_rows = mem_kv(mem, norm_mem_g[l], w_k[l], w_v[l], heads=heads, name="mem_kv")
    h_p, tail_p = mix(xp, proj_p, None, *mix_w, tm=MIX_TM, gate_len=_gate_len(seq, w_spatial[l]),
                      seq_len=seq, name="mix_prompt")
    h_p, hn_p = attn_block(h_p, norm_x_g[l], w_q_b, k_b.reshape(nb, n_mem, dm), v_b.reshape(nb, n_mem, dm), w_xo,
                           norm_mlp_g[l], heads=heads, tm=ATTN_TM, seq_len=seq, name="attn_block_prompt")
    y_p = mlp(hn_p, h_p, w_up_b, w_down_b, norm_final_g, tm=ROW_TM, fc=MLP_FC, name="mlp_prompt")
    k_p = _cache_view(k_rows, nb, n_mem, heads, hd)
    v_p = _cache_view(v_rows, nb, n_mem, heads, hd)

    bw = p_s.shape[1]
    keep = conv_w.shape[1] - 1
    conv_p = tail_p.reshape(nb, seq // MIX_TM, 8, bw)[:, -1, 8 - keep:, :]
    conv_s = p_s.reshape(nd, dseq, bw)[:, dseq - keep:, :]
    return (y_p.reshape(nb, seq, dm), y_s.reshape(nd, dseq, dm), k_p[None], v_p[None],
            conv_p[None], conv_s[None], vn_s.reshape(1, nd, dseq, -1))
```

```python
import functools

import jax
import jax.numpy as jnp
from jax import lax
from jax.experimental import pallas as pl
from jax.experimental.pallas import tpu as pltpu

EPS = 1e-6
CHUNK = 128
F32 = jnp.float32
BF16 = jnp.bfloat16

V7X_VMEM_BYTES = 64 * 1024 * 1024
VMEM_LIMIT = V7X_VMEM_BYTES * 7 // 8


def _params(*sem):
    return pltpu.CompilerParams(dimension_semantics=sem, vmem_limit_bytes=VMEM_LIMIT)


def _rms(x, g):
    r = lax.rsqrt(jnp.mean(x * x, axis=-1, keepdims=True) + EPS)
    return (x * r) * g


def _resident(shape):
    return pl.BlockSpec(shape, lambda *_: (0,) * len(shape), pipeline_mode=pl.Buffered(1))


def _bf16_tile(w_ref, wb_ref):
    if wb_ref is None:
        return w_ref[...].astype(BF16)
    wb_ref[...] = w_ref[...].astype(BF16)
    return wb_ref[...]


def _rms_matmul_kernel(x_hbm, g_ref, w_ref, *rest, emit_w, n_side):
    side_in, rest = rest[:n_side], rest[n_side:]
    o_ref, rest = rest[0], rest[1:]
    wb_ref, rest = (rest[0], rest[1:]) if emit_w else (None, rest)
    side_out, (xn_ref, xbuf, sem) = rest[:n_side], rest[n_side:]
    i, j = pl.program_id(0), pl.program_id(1)
    tm = xbuf.shape[0]

    def x_rows(tile):
        return pltpu.make_async_copy(x_hbm.at[pl.ds(tile * tm, tm), :], xbuf, sem.at[0])

    @pl.when((i == 0) & (j == 0))
    def _():
        x_rows(0).start()

    @pl.when(j == 0)
    def _():
        x_rows(i).wait()
        xn_ref[...] = _rms(xbuf[...], g_ref[...]).astype(BF16)

        @pl.when(i + 1 < pl.num_programs(0))
        def _():
            x_rows(i + 1).start()

    w = _bf16_tile(w_ref, wb_ref)
    o_ref[...] = jnp.dot(xn_ref[...], w, preferred_element_type=F32).astype(o_ref.dtype)
    for src, dst in zip(side_in, side_out):
        dst[...] = src[...].astype(BF16)


BF16_SUBLANES = 16


def _side_chunks(rows, steps):
    return max(n for n in range(1, steps + 1) if rows % n == 0 and (rows // n) % BF16_SUBLANES == 0)


def rms_matmul(x, g, w, out_dtype, *, tm, tn, name, emit_w=False, side_cast=()):
    t, d = x.shape
    n = w.shape[1]
    assert t == tm or not emit_w
    nj = n // tn
    steps = (t // tm) * nj
    w_spec = pl.BlockSpec((d, tn), lambda i, j: (0, j))
    out_specs = [pl.BlockSpec((tm, tn), lambda i, j: (i, j))]
    out_shape = [jax.ShapeDtypeStruct((t, n), out_dtype)]
    if emit_w:
        out_specs.append(w_spec)
        out_shape.append(jax.ShapeDtypeStruct(w.shape, BF16))
    side_specs = []
    for s in side_cast:
        chunks = _side_chunks(s.shape[0], steps)
        side_specs.append(pl.BlockSpec((s.shape[0] // chunks, s.shape[1]),
                                       lambda i, j, c=chunks: (jnp.minimum(i * nj + j, c - 1), 0)))
        out_shape.append(jax.ShapeDtypeStruct(s.shape, BF16))
    outs = pl.pallas_call(
        functools.partial(_rms_matmul_kernel, emit_w=emit_w, n_side=len(side_cast)),
        grid=(t // tm, nj),
        in_specs=[pl.BlockSpec(memory_space=pl.ANY), pl.BlockSpec((1, d), lambda i, j: (0, 0)), w_spec] + side_specs,
        out_specs=out_specs + side_specs, out_shape=out_shape,
        scratch_shapes=[pltpu.VMEM((tm, d), BF16), pltpu.VMEM((tm, d), F32), pltpu.SemaphoreType.DMA((1,))],
        compiler_params=_params("arbitrary", "arbitrary"),
        name=name,
    )(x, g.reshape(1, d), w, *side_cast)
    return outs if len(outs) > 1 else outs[0]


LANES = 128


def _mem_kv_kernel(x_ref, g_ref, wk_ref, wv_ref, k_ref, v_ref, kc_ref, vc_ref, xn_ref, *, heads):
    head = pl.program_id(0)

    @pl.when(head == 0)
    def _():
        xn_ref[...] = _rms(x_ref[...], g_ref[...]).astype(BF16)

    rows = x_ref.shape[0]
    slabs = wk_ref.shape[1] // LANES
    for w_ref, rows_ref, cache_ref in ((wk_ref, k_ref, kc_ref), (wv_ref, v_ref, vc_ref)):
        o = jnp.dot(xn_ref[...], w_ref[...].astype(BF16), preferred_element_type=F32)
        rows_ref[...] = o.astype(rows_ref.dtype)
        for c in range(slabs):
            cache_ref[pl.ds(c * heads + head, rows, stride=slabs * heads), :] = o[:, c * LANES:(c + 1) * LANES]


def mem_kv(mem, g, w_k, w_v, *, heads, name):
    t, d = mem.shape
    hd = w_k.shape[1] // heads
    once = dict(pipeline_mode=pl.Buffered(1))
    w_spec = pl.BlockSpec((d, hd), lambda j: (0, j))
    rows_spec = pl.BlockSpec((t, hd), lambda j: (0, j))
    cache_spec = pl.BlockSpec((t * d // LANES, LANES), lambda j: (0, 0), **once)
    return pl.pallas_call(
        functools.partial(_mem_kv_kernel, heads=heads), grid=(heads,),
        in_specs=[pl.BlockSpec((t, d), lambda j: (0, 0), **once), pl.BlockSpec((1, d), lambda j: (0, 0)),
                  w_spec, w_spec],
        out_specs=[rows_spec, rows_spec, cache_spec, cache_spec],
        out_shape=[jax.ShapeDtypeStruct((t, d), BF16)] * 2 + [jax.ShapeDtypeStruct((t * d // LANES, LANES), F32)] * 2,
        scratch_shapes=[pltpu.VMEM((t, d), BF16)],
        compiler_params=_params("arbitrary"), name=name,
    )(mem, g.reshape(1, d), w_k, w_v)


def _cache_view(rows, n, m, heads, hd):
    a = rows.reshape(n * m, hd // LANES, heads, LANES)
    return a.transpose(0, 2, 1, 3).reshape(n, m, heads, hd)


def _mix_kernel(*refs, tm, gate_len, widths, groups, prompt, tiles_per_seq):
    if prompt:
        (x_ref, proj_ref, cgp_ref, xinp_ref, lng_ref, lnb_ref, wsp_ref, bsp_ref, cw_ref,
         wa_ref, wb_ref, wmix_ref, h_ref, tail_ref, gate_ref, gbias_ref) = refs
    else:
        (x_ref, proj_ref, e1_ref, e2_ref, lng_ref, lnb_ref, wsp_ref, bsp_ref, cw_ref,
         wa_ref, wb_ref, wmix_ref, h_ref, p_ref, v_ref, gate_ref, gbias_ref) = refs
    aw, bw, dm = widths
    o_u, o_v, o_bg, o_cg, o_xin, o_ga, o_gb = (0, aw, 2 * aw, 2 * aw + bw, 2 * aw + 2 * bw,
                                               2 * aw + 3 * bw, 2 * aw + 3 * bw + dm)

    def col(o, w):
        return proj_ref[:, o:o + w].astype(F32)

    gd = aw // groups

    @pl.when(pl.program_id(0) == 0)
    def _():
        row = lax.broadcasted_iota(jnp.int32, (tm, tm), 0)
        cidx = lax.broadcasted_iota(jnp.int32, (tm, tm), 1)
        keep = ((row ^ cidx) < gate_len) & (cidx <= row)
        cs = wsp_ref.shape[1]
        pick = (lax.broadcasted_iota(jnp.int32, (tm, cs), 1)
                == (lax.broadcasted_iota(jnp.int32, (tm, cs), 0) & (gate_len - 1)))
        pick_b = jnp.where(pick, 1.0, 0.0).astype(BF16)
        for g in range(groups):
            w_rows = jnp.dot(pick_b, wsp_ref[g].astype(BF16), preferred_element_type=F32).astype(BF16)
            w_full = lax.dot_general(w_rows, pick_b, (((1,), (1,)), ((), ())), preferred_element_type=F32)
            gate_ref[g] = jnp.where(keep, w_full, 0.0).astype(BF16)
            gbias_ref[g] = jnp.sum(jnp.where(pick, bsp_ref[g:g + 1, :], 0.0), axis=1, keepdims=True)

    p = col(o_cg, bw) * col(o_xin, bw)
    trow = lax.broadcasted_iota(jnp.int32, (tm, bw), 0)
    if prompt:
        fresh = pl.program_id(0) % tiles_per_seq == 0
        pp = cgp_ref[...].astype(F32) * xinp_ref[...].astype(F32)
        pp = jnp.where(fresh, 0.0, pp)
        last1 = pp[-1:, :]
        last2 = pp[-2:-1, :]
        e1 = jnp.where(trow == 0, last1, 0.0)
        e2 = jnp.where(trow == 0, last2, jnp.where(trow == 1, last1, 0.0))
        tail_ref[...] = p[tm - 8:, :]
    else:
        trow = trow & (gate_len - 1)
        e1 = e1_ref[...]
        e2 = e2_ref[...]
        p_ref[...] = p
    s1 = jnp.where(trow >= 1, pltpu.roll(p, 1, 0), 0.0) + e1
    s2 = jnp.where(trow >= 2, pltpu.roll(p, 2, 0), 0.0) + e2
    conv = cw_ref[0:1, :] * s2 + cw_ref[1:2, :] * s1 + cw_ref[2:3, :] * p
    y_b = (col(o_bg, bw) * conv).astype(BF16)
    branch_b = jax.nn.sigmoid(col(o_gb, dm)) * jnp.dot(y_b, wb_ref[...], preferred_element_type=F32)

    v = col(o_v, aw)
    mu = jnp.mean(v, axis=-1, keepdims=True)
    vc = v - mu
    var = jnp.mean(vc * vc, axis=-1, keepdims=True)
    vn = (vc * lax.rsqrt(var + EPS)) * lng_ref[...] + lnb_ref[...]
    if not prompt:
        v_ref[...] = vn
    vb = vn.astype(BF16)
    zs = [jnp.dot(gate_ref[g], vb[:, g * gd:(g + 1) * gd], preferred_element_type=F32) + gbias_ref[g]
          for g in range(groups)]
    y_a = (col(o_u, aw) * jnp.concatenate(zs, axis=1)).astype(BF16)
    branch_a = jax.nn.sigmoid(col(o_ga, dm)) * jnp.dot(y_a, wa_ref[...], preferred_element_type=F32)

    h_ref[...] = x_ref[...] + jnp.dot((branch_a + branch_b).astype(BF16), wmix_ref[...],
                                      preferred_element_type=F32)


def mix(x, proj, prev, ln_g, ln_b, w_sp, b_sp, conv_w, wa, wb, wmix, *, tm, gate_len, seq_len, name):
    t, dm = x.shape
    aw, bw = wa.shape[0], wb.shape[0]
    groups = w_sp.shape[0]
    prompt = prev is None
    nt = t // tm
    row_spec = lambda w: pl.BlockSpec((tm, w), lambda i: (i, 0))
    in_specs = [row_spec(dm), row_spec(proj.shape[1])]
    if prompt:
        prev_rows = 16
        cg_blk = (2 * aw + bw) // bw
        in_specs += [
            pl.BlockSpec((prev_rows, bw), lambda i: (jnp.maximum(i * (tm // prev_rows) - 1, 0), cg_blk)),
            pl.BlockSpec((prev_rows, bw), lambda i: (jnp.maximum(i * (tm // prev_rows) - 1, 0), cg_blk + 1)),
        ]
        extra = (proj, proj)
    else:
        in_specs += [row_spec(bw), row_spec(bw)]
        extra = prev
    in_specs += [_resident((1, aw)), _resident((1, aw)), _resident(w_sp.shape), _resident(b_sp.shape),
                 _resident(conv_w.shape), _resident(wa.shape), _resident(wb.shape), _resident(wmix.shape)]
    out_specs = [row_spec(dm)]
    out_shape = [jax.ShapeDtypeStruct((t, dm), F32)]
    if prompt:
        out_specs.append(pl.BlockSpec((8, bw), lambda i: (i, 0)))
        out_shape.append(jax.ShapeDtypeStruct((nt * 8, bw), F32))
    else:
        out_specs += [row_spec(bw), row_spec(aw)]
        out_shape += [jax.ShapeDtypeStruct((t, bw), F32), jax.ShapeDtypeStruct((t, aw), F32)]
    kern = functools.partial(_mix_kernel, tm=tm, gate_len=gate_len, widths=(aw, bw, dm), groups=groups,
                             prompt=prompt, tiles_per_seq=max(seq_len // tm, 1))
    return pl.pallas_call(
        kern, grid=(nt,), in_specs=in_specs, out_specs=out_specs, out_shape=out_shape,
        scratch_shapes=[pltpu.VMEM((groups, tm, tm), BF16), pltpu.VMEM((groups, tm, 1), F32)],
        compiler_params=_params("arbitrary"), name=name,
    )(x, proj, *extra, ln_g.reshape(1, aw), ln_b.reshape(1, aw), w_sp, b_sp, conv_w, wa, wb, wmix)


def _attn_block_kernel(h_ref, g_ref, wq_ref, k_ref, v_ref, wxo_ref, gn_ref, out_ref, hn_ref, *, heads, head_dim):
    scale = head_dim ** -0.5
    h = h_ref[...]
    q = jnp.dot(_rms(h, g_ref[...]).astype(BF16), wq_ref[...], preferred_element_type=F32).astype(BF16)
    head_cols = [slice(hd * head_dim, (hd + 1) * head_dim) for hd in range(heads)]
    scores = [lax.dot_general(q[:, cols], k_ref[0, :, cols].astype(BF16), (((1,), (1,)), ((), ())),
                              preferred_element_type=F32) * scale for cols in head_cols]
    probs = []
    for s in scores:
        e = jnp.exp(s - jnp.max(s, axis=-1, keepdims=True))
        probs.append((e / jnp.sum(e, axis=-1, keepdims=True)).astype(BF16))
    outs = [jnp.dot(p, v_ref[0, :, cols].astype(BF16), preferred_element_type=F32).astype(BF16)
            for p, cols in zip(probs, head_cols)]
    o = jnp.concatenate(outs, axis=1)
    _residual_and_norm(h, o, wxo_ref, gn_ref, out_ref, hn_ref)


def _residual_and_norm(h, o, wxo_ref, gn_ref, out_ref, hn_ref):
    out = h + jnp.dot(o, wxo_ref[...], preferred_element_type=F32)
    out_ref[...] = out
    hn_ref[...] = _rms(out, gn_ref[...]).astype(BF16)


def attn_block(h, g, wq, k, v, wxo, g_next, *, heads, tm, seq_len, name):
    t, dm = h.shape
    m = k.shape[1]
    per_seq = seq_len // tm
    row_spec = pl.BlockSpec((tm, dm), lambda i: (i, 0))
    kv_spec = pl.BlockSpec((1, m, dm), lambda i: (i // per_seq, 0, 0))
    kern = functools.partial(_attn_block_kernel, heads=heads, head_dim=dm // heads)
    return pl.pallas_call(
        kern, grid=(t // tm,),
        in_specs=[row_spec, _resident((1, dm)), _resident(wq.shape), kv_spec, kv_spec, _resident(wxo.shape),
                  _resident((1, dm))],
        out_specs=[row_spec, row_spec],
        out_shape=[jax.ShapeDtypeStruct((t, dm), F32), jax.ShapeDtypeStruct((t, dm), BF16)],
        compiler_params=_params("parallel"), name=name,
    )(h, g.reshape(1, dm), wq, k, v, wxo, g_next.reshape(1, dm))


def _attn_rows_kernel(q_ref, k_ref, v_ref, o_ref, *, n_seq, rows, heads, head_dim, n_mem):
    scale = head_dim ** -0.5
    slabs = head_dim // LANES
    per_seq = n_mem * slabs * heads
    width = n_mem * slabs
    pairs = [(b, h) for b in range(n_seq) for h in range(heads)]
    lane_slab = lax.broadcasted_iota(jnp.int32, (rows, width), 1) & (slabs - 1)
    is_slab = [lane_slab == c for c in range(slabs)]

    parts = []
    for b, h in pairs:
        r0, c0 = b * rows, h * head_dim
        qh = jnp.concatenate(
            [q_ref[r0:r0 + rows, c0 + c * LANES:c0 + (c + 1) * LANES] for c in range(slabs)], axis=0
        ).astype(BF16)
        kh = k_ref[pl.ds(b * per_seq + h, width, stride=heads), :].astype(BF16)
        g = lax.dot_general(qh, kh, (((1,), (1,)), ((), ())), preferred_element_type=F32)
        s = g[0:rows]
        for c in range(1, slabs):
            s = jnp.where(is_slab[c], g[c * rows:(c + 1) * rows], s)
        parts.append(s)
    s = jnp.concatenate(parts, axis=0)

    lane = lax.broadcasted_iota(jnp.int32, s.shape, 1)
    step = 1
    while step < slabs:
        s = s + jnp.where((lane & step) != 0, pltpu.roll(s, step, 1), pltpu.roll(s, width - step, 1))
        step *= 2
    s = s * scale
    e = jnp.exp(s - jnp.max(s, axis=-1, keepdims=True))
    p = e / (jnp.sum(e, axis=-1, keepdims=True) * (1.0 / slabs))

    for i, (b, h) in enumerate(pairs):
        r0, c0 = b * rows, h * head_dim
        ph = p[i * rows:(i + 1) * rows]
        w = jnp.concatenate([jnp.where(is_slab[c], ph, 0.0) for c in range(slabs)], axis=0).astype(BF16)
        vh = v_ref[pl.ds(b * per_seq + h, width, stride=heads), :].astype(BF16)
        o = jnp.dot(w, vh, preferred_element_type=F32)
        for c in range(slabs):
            o_ref[r0:r0 + rows, c0 + c * LANES:c0 + (c + 1) * LANES] = o[c * rows:(c + 1) * rows]


def _rows_view(a):
    n, m, heads, hd = a.shape
    a = a.reshape(n * m, heads, hd // LANES, LANES)
    return a.transpose(0, 2, 1, 3).reshape(n * m * hd // LANES * heads, LANES)


def attn_rows(q, k, v, *, n_seq, name):
    n, n_mem, heads, hd = k.shape
    t, dm = q.shape
    rows = t // n
    slabs = hd // LANES
    assert slabs & (slabs - 1) == 0 and rows % 8 == 0
    per_seq = n_mem * slabs * heads
    kv_spec = pl.BlockSpec((n_seq * per_seq, LANES), lambda i: (i, 0))
    q_spec = pl.BlockSpec((n_seq * rows, dm), lambda i: (i, 0))
    kern = functools.partial(_attn_rows_kernel, n_seq=n_seq, rows=rows, heads=heads, head_dim=hd, n_mem=n_mem)
    return pl.pallas_call(
        kern, grid=(n // n_seq,),
        in_specs=[q_spec, kv_spec, kv_spec], out_specs=q_spec,
        out_shape=jax.ShapeDtypeStruct((t, dm), F32),
        compiler_params=_params("parallel"), name=name,
    )(q, _rows_view(k), _rows_view(v))


def _attn_out_kernel(h_ref, o_ref, wxo_ref, gn_ref, out_ref, hn_ref):
    _residual_and_norm(h_ref[...], o_ref[...].astype(BF16), wxo_ref, gn_ref, out_ref, hn_ref)


def attn_out(h, o, wxo, g_next, *, tm, name):
    t, dm = h.shape
    row_spec = pl.BlockSpec((tm, dm), lambda i: (i, 0))
    return pl.pallas_call(
        _attn_out_kernel, grid=(t // tm,),
        in_specs=[row_spec, row_spec, _resident(wxo.shape), _resident((1, dm))],
        out_specs=[row_spec, row_spec],
        out_shape=[jax.ShapeDtypeStruct((t, dm), F32), jax.ShapeDtypeStruct((t, dm), BF16)],
        compiler_params=_params("parallel"), name=name,
    )(h, o, wxo, g_next.reshape(1, dm))


def _mlp_kernel(hn_ref, h_hbm, wup_hbm, wdn_hbm, gf_ref, y_ref, hbuf, wup_buf, wdn_buf, sem, *, tm, fc):
    i, n = pl.program_id(0), pl.num_programs(0)
    chunks = wup_hbm.shape[1] // fc
    pairs = chunks // 2
    assert chunks % 2 == 0
    following = jnp.minimum(i + 1, n - 1)

    def residual_rows(tile):
        return pltpu.make_async_copy(h_hbm.at[pl.ds(tile * tm, tm), :], hbuf, sem.at[0])

    def weight_tiles(pair, slot):
        c0 = pl.multiple_of((2 * pair + slot) * fc, fc)
        return (pltpu.make_async_copy(wup_hbm.at[:, pl.ds(c0, fc)], wup_buf.at[slot], sem.at[1 + slot]),
                pltpu.make_async_copy(wdn_hbm.at[pl.ds(c0, fc), :], wdn_buf.at[slot], sem.at[3 + slot]))

    def chunk(pair, slot, seed):
        nxt_pair, nxt_slot = (pair, 1) if slot == 0 else ((pair + 1) % pairs, 0)
        for copy in weight_tiles(nxt_pair, nxt_slot):
            copy.start()
        for copy in weight_tiles(pair, slot):
            copy.wait()
        a = jnp.dot(hn_ref[...], wup_buf[slot], preferred_element_type=F32)
        a = jnp.square(jnp.maximum(a, 0.0)).astype(BF16)
        down = jnp.dot(a, wdn_buf[slot], preferred_element_type=F32)
        if seed:
            residual_rows(i).wait()
            y_ref[...] = hbuf[...] + down
            residual_rows(following).start()
        else:
            y_ref[...] += down

    @pl.when(i == 0)
    def _():
        residual_rows(0).start()
        for copy in weight_tiles(0, 0):
            copy.start()

    chunk(0, 0, seed=True)
    chunk(0, 1, seed=False)

    def pair_body(pair, carry):
        chunk(pair, 0, seed=False)
        chunk(pair, 1, seed=False)
        return carry

    lax.fori_loop(1, pairs, pair_body, 0)
    y_ref[...] = _rms(y_ref[...], gf_ref[...])

    @pl.when(i == n - 1)
    def _():
        residual_rows(following).wait()
        for copy in weight_tiles(0, 0):
            copy.wait()


def mlp(hn, h, w_up, w_down, g_final, *, tm, fc, name):
    t, dm = h.shape
    return pl.pallas_call(
        functools.partial(_mlp_kernel, tm=tm, fc=fc), grid=(t // tm,),
        in_specs=[pl.BlockSpec((tm, dm), lambda i: (i, 0)),
                  pl.BlockSpec(memory_space=pl.ANY), pl.BlockSpec(memory_space=pl.ANY),
                  pl.BlockSpec(memory_space=pl.ANY),
                  pl.BlockSpec((1, dm), lambda i: (0, 0))],
        out_specs=pl.BlockSpec((tm, dm), lambda i: (i, 0)),
        out_shape=jax.ShapeDtypeStruct((t, dm), F32),
        scratch_shapes=[pltpu.VMEM((tm, dm), F32), pltpu.VMEM((2, dm, fc), BF16), pltpu.VMEM((2, fc, dm), BF16),
                        pltpu.SemaphoreType.DMA((5,))],
        compiler_params=_params("arbitrary"), name=name,
    )(hn, h, w_up, w_down, g_final.reshape(1, dm))


MIX_TM = 256
ROW_TM = 1024
COL_TN = 1024
WIDE_TN = 1536
ATTN_TM = 512
MLP_FC = 512


def _gate_len(seq_len, w_spatial):
    gate_len = CHUNK if seq_len % CHUNK == 0 else seq_len
    assert gate_len & (gate_len - 1) == 0 and MIX_TM % gate_len == 0 and gate_len <= w_spatial.shape[1]
    return gate_len


def kernel(x_prompt, x_sample, state_conv, cache_mem_k, cache_mem_v, mem_prompt, norm_mix_g, w_in, ln_v_g, ln_v_b, w_spatial, b_spatial, conv_w, w_branch_a, w_branch_b, w_mix_out, norm_x_g, norm_mem_g, w_q, w_k, w_v, w_x_out, norm_mlp_g, w_up, w_down, norm_final_g):
    depth = w_in.shape[0]
    assert depth == 1, "the final rmsnorm is fused into the single layer's MLP kernel"
    nb, seq, dm = x_prompt.shape
    nd, dseq, _ = x_sample.shape
    n_mem, heads, hd = cache_mem_k.shape[2:]
    l = 0
    xs = x_sample.reshape(nd * dseq, dm)
    xp = x_prompt.reshape(nb * seq, dm)
    assert xs.shape[0] == ROW_TM

    proj_s, w_in_b = rms_matmul(xs, norm_mix_g[l], w_in[l], BF16, tm=ROW_TM, tn=COL_TN, emit_w=True,
                                name="in_proj_sample")
    proj_p, w_a, w_b, w_mix, w_q_b, w_xo, w_up_b, w_down_b = rms_matmul(
        xp, norm_mix_g[l], w_in_b, BF16, tm=ROW_TM, tn=WIDE_TN, name="in_proj_prompt",
        side_cast=(w_branch_a[l], w_branch_b[l], w_mix_out[l], w_q[l], w_x_out[l], w_up[l], w_down[l]))
    mix_w = (ln_v_g[l], ln_v_b[l], w_spatial[l], b_spatial[l], conv_w[l], w_a, w_b, w_mix)

    prev = state_conv[l]
    e1 = jnp.pad(prev[:, 1:2], ((0, 0), (0, dseq - 1), (0, 0))).reshape(nd * dseq, -1)
    e2 = jnp.pad(prev, ((0, 0), (0, dseq - 2), (0, 0))).reshape(nd * dseq, -1)
    h_s, p_s, vn_s = mix(xs, proj_s, (e1, e2), *mix_w, tm=MIX_TM, gate_len=_gate_len(dseq, w_spatial[l]),
                         seq_len=dseq, name="mix_sample")
    q_s = rms_matmul(h_s, norm_x_g[l], w_q_b, F32, tm=ROW_TM, tn=COL_TN, name="q_proj_sample")
    o_s = attn_rows(q_s, cache_mem_k[l], cache_mem_v[l], n_seq=4, name="attn_sample")
    h_s, hn_s = attn_out(h_s, o_s, w_xo, norm_mlp_g[l], tm=ATTN_TM, name="x_out_sample")
    y_s = mlp(hn_s, h_s, w_up_b, w_down_b, norm_final_g, tm=ROW_TM, fc=MLP_FC, name="mlp_sample")

    mem = mem_prompt.reshape(nb * n_mem, dm)
    k_b, v_b, k_rows, v_rows = mem_kv(mem, norm_mem_g[l], w_k[l], w_v[l], heads=heads, name="mem_kv")
    h_p, tail_p = mix(xp, proj_p, None, *mix_w, tm=MIX_TM, gate_len=_gate_len(seq, w_spatial[l]),
                      seq_len=seq, name="mix_prompt")
    h_p, hn_p = attn_block(h_p, norm_x_g[l], w_q_b, k_b.reshape(nb, n_mem, dm), v_b.reshape(nb, n_mem, dm), w_xo,
                           norm_mlp_g[l], heads=heads, tm=ATTN_TM, seq_len=seq, name="attn_block_prompt")
    y_p = mlp(hn_p, h_p, w_up_b, w_down_b, norm_final_g, tm=ROW_TM, fc=MLP_FC, name="mlp_prompt")
    k_p = _cache_view(k_rows, nb, n_mem, heads, hd)
    v_p = _cache_view(v_rows, nb, n_mem, heads, hd)

    bw = p_s.shape[1]
    keep = conv_w.shape[1] - 1
    conv_p = tail_p.reshape(nb, seq // MIX_TM, 8, bw)[:, -1, 8 - keep:, :]
    conv_s = p_s.reshape(nd, dseq, bw)[:, dseq - keep:, :]
    return (y_p.reshape(nb, seq, dm), y_s.reshape(nd, dseq, dm), k_p[None], v_p[None],
            conv_p[None], conv_s[None], vn_s.reshape(1, nd, dseq, -1))
```

```python
import functools

import jax
import jax.numpy as jnp
from jax import lax
from jax.experimental import pallas as pl
from jax.experimental.pallas import tpu as pltpu

EPS = 1e-6
CHUNK = 128
F32 = jnp.float32
BF16 = jnp.bfloat16

V7X_VMEM_BYTES = 64 * 1024 * 1024
VMEM_LIMIT = V7X_VMEM_BYTES * 7 // 8


def _params(*sem):
    return pltpu.CompilerParams(dimension_semantics=sem, vmem_limit_bytes=VMEM_LIMIT)


def _rms(x, g):
    r = lax.rsqrt(jnp.mean(x * x, axis=-1, keepdims=True) + EPS)
    return (x * r) * g


def _resident(shape):
    return pl.BlockSpec(shape, lambda *_: (0,) * len(shape), pipeline_mode=pl.Buffered(1))


def _bf16_tile(w_ref, wb_ref):
    if wb_ref is None:
        return w_ref[...].astype(BF16)
    wb_ref[...] = w_ref[...].astype(BF16)
    return wb_ref[...]


def _rms_matmul_kernel(x_hbm, g_ref, w_ref, *rest, emit_w, n_side):
    side_in, rest = rest[:n_side], rest[n_side:]
    o_ref, rest = rest[0], rest[1:]
    wb_ref, rest = (rest[0], rest[1:]) if emit_w else (None, rest)
    side_out, (xn_ref, xbuf, sem) = rest[:n_side], rest[n_side:]
    i, j = pl.program_id(0), pl.program_id(1)
    tm = xbuf.shape[0]

    def x_rows(tile):
        return pltpu.make_async_copy(x_hbm.at[pl.ds(tile * tm, tm), :], xbuf, sem.at[0])

    @pl.when((i == 0) & (j == 0))
    def _():
        x_rows(0).start()

    @pl.when(j == 0)
    def _():
        x_rows(i).wait()
        xn_ref[...] = _rms(xbuf[...], g_ref[...]).astype(BF16)

        @pl.when(i + 1 < pl.num_programs(0))
        def _():
            x_rows(i + 1).start()

    w = _bf16_tile(w_ref, wb_ref)
    o_ref[...] = jnp.dot(xn_ref[...], w, preferred_element_type=F32).astype(o_ref.dtype)
    for src, dst in zip(side_in, side_out):
        dst[...] = src[...].astype(BF16)


BF16_SUBLANES = 16


def _side_chunks(rows, steps):
    return max(n for n in range(1, steps + 1) if rows % n == 0 and (rows // n) % BF16_SUBLANES == 0)


def rms_matmul(x, g, w, out_dtype, *, tm, tn, name, emit_w=False, side_cast=()):
    t, d = x.shape
    n = w.shape[1]
    assert t == tm or not emit_w
    nj = n // tn
    steps = (t // tm) * nj
    w_spec = pl.BlockSpec((d, tn), lambda i, j: (0, j))
    out_specs = [pl.BlockSpec((tm, tn), lambda i, j: (i, j))]
    out_shape = [jax.ShapeDtypeStruct((t, n), out_dtype)]
    if emit_w:
        out_specs.append(w_spec)
        out_shape.append(jax.ShapeDtypeStruct(w.shape, BF16))
    side_specs = []
    for s in side_cast:
        chunks = _side_chunks(s.shape[0], steps)
        side_specs.append(pl.BlockSpec((s.shape[0] // chunks, s.shape[1]),
                                       lambda i, j, c=chunks: (jnp.minimum(i * nj + j, c - 1), 0)))
        out_shape.append(jax.ShapeDtypeStruct(s.shape, BF16))
    outs = pl.pallas_call(
        functools.partial(_rms_matmul_kernel, emit_w=emit_w, n_side=len(side_cast)),
        grid=(t // tm, nj),
        in_specs=[pl.BlockSpec(memory_space=pl.ANY), pl.BlockSpec((1, d), lambda i, j: (0, 0)), w_spec] + side_specs,
        out_specs=out_specs + side_specs, out_shape=out_shape,
        scratch_shapes=[pltpu.VMEM((tm, d), BF16), pltpu.VMEM((tm, d), F32), pltpu.SemaphoreType.DMA((1,))],
        compiler_params=_params("arbitrary", "arbitrary"),
        name=name,
    )(x, g.reshape(1, d), w, *side_cast)
    return outs if len(outs) > 1 else outs[0]


LANES = 128


def _mem_kv_kernel(x_ref, g_ref, wk_ref, wv_ref, k_ref, v_ref, kc_ref, vc_ref, xn_ref, *, heads):
    head = pl.program_id(0)

    @pl.when(head == 0)
    def _():
        xn_ref[...] = _rms(x_ref[...], g_ref[...]).astype(BF16)

    rows = x_ref.shape[0]
    slabs = wk_ref.shape[1] // LANES
    for w_ref, rows_ref, cache_ref in ((wk_ref, k_ref, kc_ref), (wv_ref, v_ref, vc_ref)):
        o = jnp.dot(xn_ref[...], w_ref[...].astype(BF16), preferred_element_type=F32)
        rows_ref[...] = o.astype(rows_ref.dtype)
        for c in range(slabs):
            cache_ref[pl.ds(c * heads + head, rows, stride=slabs * heads), :] = o[:, c * LANES:(c + 1) * LANES]


def mem_kv(mem, g, w_k, w_v, *, heads, name):
    t, d = mem.shape
    hd = w_k.shape[1] // heads
    once = dict(pipeline_mode=pl.Buffered(1))
    w_spec = pl.BlockSpec((d, hd), lambda j: (0, j))
    rows_spec = pl.BlockSpec((t, hd), lambda j: (0, j))
    cache_spec = pl.BlockSpec((t * d // LANES, LANES), lambda j: (0, 0), **once)
    return pl.pallas_call(
        functools.partial(_mem_kv_kernel, heads=heads), grid=(heads,),
        in_specs=[pl.BlockSpec((t, d), lambda j: (0, 0), **once), pl.BlockSpec((1, d), lambda j: (0, 0)),
                  w_spec, w_spec],
        out_specs=[rows_spec, rows_spec, cache_spec, cache_spec],
        out_shape=[jax.ShapeDtypeStruct((t, d), BF16)] * 2 + [jax.ShapeDtypeStruct((t * d // LANES, LANES), F32)] * 2,
        scratch_shapes=[pltpu.VMEM((t, d), BF16)],
        compiler_params=_params("arbitrary"), name=name,
    )(mem, g.reshape(1, d), w_k, w_v)


def _cache_view(rows, n, m, heads, hd):
    a = rows.reshape(n * m, hd // LANES, heads, LANES)
    return a.transpose(0, 2, 1, 3).reshape(n, m, heads, hd)


def _mix_kernel(*refs, tm, gate_len, widths, groups, prompt, tiles_per_seq):
    if prompt:
        (x_ref, proj_ref, cgp_ref, xinp_ref, lng_ref, lnb_ref, wsp_ref, bsp_ref, cw_ref,
         wa_ref, wb_ref, wmix_ref, h_ref, tail_ref, gate_ref, gbias_ref) = refs
    else:
        (x_ref, proj_ref, e1_ref, e2_ref, lng_ref, lnb_ref, wsp_ref, bsp_ref, cw_ref,
         wa_ref, wb_ref, wmix_ref, h_ref, p_ref, v_ref, gate_ref, gbias_ref) = refs
    aw, bw, dm = widths
    o_u, o_v, o_bg, o_cg, o_xin, o_ga, o_gb = (0, aw, 2 * aw, 2 * aw + bw, 2 * aw + 2 * bw,
                                               2 * aw + 3 * bw, 2 * aw + 3 * bw + dm)

    def col(o, w):
        return proj_ref[:, o:o + w].astype(F32)

    gd = aw // groups

    @pl.when(pl.program_id(0) == 0)
    def _():
        row = lax.broadcasted_iota(jnp.int32, (tm, tm), 0)
        cidx = lax.broadcasted_iota(jnp.int32, (tm, tm), 1)
        keep = ((row ^ cidx) < gate_len) & (cidx <= row)
        cs = wsp_ref.shape[1]
        pick = (lax.broadcasted_iota(jnp.int32, (tm, cs), 1)
                == (lax.broadcasted_iota(jnp.int32, (tm, cs), 0) & (gate_len - 1)))
        pick_b = jnp.where(pick, 1.0, 0.0).astype(BF16)
        for g in range(groups):
            w_rows = jnp.dot(pick_b, wsp_ref[g].astype(BF16), preferred_element_type=F32).astype(BF16)
            w_full = lax.dot_general(w_rows, pick_b, (((1,), (1,)), ((), ())), preferred_element_type=F32)
            gate_ref[g] = jnp.where(keep, w_full, 0.0).astype(BF16)
            gbias_ref[g] = jnp.sum(jnp.where(pick, bsp_ref[g:g + 1, :], 0.0), axis=1, keepdims=True)

    p = col(o_cg, bw) * col(o_xin, bw)
    trow = lax.broadcasted_iota(jnp.int32, (tm, bw), 0)
    if prompt:
        fresh = pl.program_id(0) % tiles_per_seq == 0
        pp = cgp_ref[...].astype(F32) * xinp_ref[...].astype(F32)
        pp = jnp.where(fresh, 0.0, pp)
        last1 = pp[-1:, :]
        last2 = pp[-2:-1, :]
        e1 = jnp.where(trow == 0, last1, 0.0)
        e2 = jnp.where(trow == 0, last2, jnp.where(trow == 1, last1, 0.0))
        tail_ref[...] = p[tm - 8:, :]
    else:
        trow = trow & (gate_len - 1)
        e1 = e1_ref[...]
        e2 = e2_ref[...]
        p_ref[...] = p
    s1 = jnp.where(trow >= 1, pltpu.roll(p, 1, 0), 0.0) + e1
    s2 = jnp.where(trow >= 2, pltpu.roll(p, 2, 0), 0.0) + e2
    conv = cw_ref[0:1, :] * s2 + cw_ref[1:2, :] * s1 + cw_ref[2:3, :] * p
    y_b = (col(o_bg, bw) * conv).astype(BF16)
    branch_b = jax.nn.sigmoid(col(o_gb, dm)) * jnp.dot(y_b, wb_ref[...], preferred_element_type=F32)

    v = col(o_v, aw)
    mu = jnp.mean(v, axis=-1, keepdims=True)
    vc = v - mu
    var = jnp.mean(vc * vc, axis=-1, keepdims=True)
    vn = (vc * lax.rsqrt(var + EPS)) * lng_ref[...] + lnb_ref[...]
    if not prompt:
        v_ref[...] = vn
    vb = vn.astype(BF16)
    zs = [jnp.dot(gate_ref[g], vb[:, g * gd:(g + 1) * gd], preferred_element_type=F32) + gbias_ref[g]
          for g in range(groups)]
    y_a = (col(o_u, aw) * jnp.concatenate(zs, axis=1)).astype(BF16)
    branch_a = jax.nn.sigmoid(col(o_ga, dm)) * jnp.dot(y_a, wa_ref[...], preferred_element_type=F32)

    h_ref[...] = x_ref[...] + jnp.dot((branch_a + branch_b).astype(BF16), wmix_ref[...],
                                      preferred_element_type=F32)


def mix(x, proj, prev, ln_g, ln_b, w_sp, b_sp, conv_w, wa, wb, wmix, *, tm, gate_len, seq_len, name):
    t, dm = x.shape
    aw, bw = wa.shape[0], wb.shape[0]
    groups = w_sp.shape[0]
    prompt = prev is None
    nt = t // tm
    row_spec = lambda w: pl.BlockSpec((tm, w), lambda i: (i, 0))
    in_specs = [row_spec(dm), row_spec(proj.shape[1])]
    if prompt:
        prev_rows = 16
        cg_blk = (2 * aw + bw) // bw
        in_specs += [
            pl.BlockSpec((prev_rows, bw), lambda i: (jnp.maximum(i * (tm // prev_rows) - 1, 0), cg_blk)),
            pl.BlockSpec((prev_rows, bw), lambda i: (jnp.maximum(i * (tm // prev_rows) - 1, 0), cg_blk + 1)),
        ]
        extra = (proj, proj)
    else:
        in_specs += [row_spec(bw), row_spec(bw)]
        extra = prev
    in_specs += [_resident((1, aw)), _resident((1, aw)), _resident(w_sp.shape), _resident(b_sp.shape),
                 _resident(conv_w.shape), _resident(wa.shape), _resident(wb.shape), _resident(wmix.shape)]
    out_specs = [row_spec(dm)]
    out_shape = [jax.ShapeDtypeStruct((t, dm), F32)]
    if prompt:
        out_specs.append(pl.BlockSpec((8, bw), lambda i: (i, 0)))
        out_shape.append(jax.ShapeDtypeStruct((nt * 8, bw), F32))
    else:
        out_specs += [row_spec(bw), row_spec(aw)]
        out_shape += [jax.ShapeDtypeStruct((t, bw), F32), jax.ShapeDtypeStruct((t, aw), F32)]
    kern = functools.partial(_mix_kernel, tm=tm, gate_len=gate_len, widths=(aw, bw, dm), groups=groups,
                             prompt=prompt, tiles_per_seq=max(seq_len // tm, 1))
    return pl.pallas_call(
        kern, grid=(nt,), in_specs=in_specs, out_specs=out_specs, out_shape=out_shape,
        scratch_shapes=[pltpu.VMEM((groups, tm, tm), BF16), pltpu.VMEM((groups, tm, 1), F32)],
        compiler_params=_params("arbitrary"), name=name,
    )(x, proj, *extra, ln_g.reshape(1, aw), ln_b.reshape(1, aw), w_sp, b_sp, conv_w, wa, wb, wmix)


def _attn_block_kernel(h_ref, g_ref, wq_ref, k_ref, v_ref, wxo_ref, gn_ref, out_ref, hn_ref, *, heads, head_dim):
    scale = head_dim ** -0.5
    h = h_ref[...]
    q = jnp.dot(_rms(h, g_ref[...]).astype(BF16), wq_ref[...], preferred_element_type=F32).astype(BF16)
    head_cols = [slice(hd * head_dim, (hd + 1) * head_dim) for hd in range(heads)]
    scores = [lax.dot_general(q[:, cols], k_ref[0, :, cols].astype(BF16), (((1,), (1,)), ((), ())),
                              preferred_element_type=F32) * scale for cols in head_cols]
    probs = []
    for s in scores:
        e = jnp.exp(s - jnp.max(s, axis=-1, keepdims=True))
        probs.append((e / jnp.sum(e, axis=-1, keepdims=True)).astype(BF16))
    outs = [jnp.dot(p, v_ref[0, :, cols].astype(BF16), preferred_element_type=F32).astype(BF16)
            for p, cols in zip(probs, head_cols)]
    o = jnp.concatenate(outs, axis=1)
    _residual_and_norm(h, o, wxo_ref, gn_ref, out_ref, hn_ref)


def _residual_and_norm(h, o, wxo_ref, gn_ref, out_ref, hn_ref):
    out = h + jnp.dot(o, wxo_ref[...], preferred_element_type=F32)
    out_ref[...] = out
    hn_ref[...] = _rms(out, gn_ref[...]).astype(BF16)


def attn_block(h, g, wq, k, v, wxo, g_next, *, heads, tm, seq_len, name):
    t, dm = h.shape
    m = k.shape[1]
    per_seq = seq_len // tm
    row_spec = pl.BlockSpec((tm, dm), lambda i: (i, 0))
    kv_spec = pl.BlockSpec((1, m, dm), lambda i: (i // per_seq, 0, 0))
    kern = functools.partial(_attn_block_kernel, heads=heads, head_dim=dm // heads)
    return pl.pallas_call(
        kern, grid=(t // tm,),
        in_specs=[row_spec, _resident((1, dm)), _resident(wq.shape), kv_spec, kv_spec, _resident(wxo.shape),
                  _resident((1, dm))],
        out_specs=[row_spec, row_spec],
        out_shape=[jax.ShapeDtypeStruct((t, dm), F32), jax.ShapeDtypeStruct((t, dm), BF16)],
        compiler_params=_params("parallel"), name=name,
    )(h, g.reshape(1, dm), wq, k, v, wxo, g_next.reshape(1, dm))


def _attn_rows_kernel(q_ref, k_ref, v_ref, o_ref, *, n_seq, rows, heads, head_dim, n_mem):
    scale = head_dim ** -0.5
    slabs = head_dim // LANES
    per_seq = n_mem * slabs * heads
    width = n_mem * slabs
    pairs = [(b, h) for b in range(n_seq) for h in range(heads)]
    lane_slab = lax.broadcasted_iota(jnp.int32, (rows, width), 1) & (slabs - 1)
    is_slab = [lane_slab == c for c in range(slabs)]

    parts = []
    for b, h in pairs:
        r0, c0 = b * rows, h * head_dim
        qh = jnp.concatenate(
            [q_ref[r0:r0 + rows, c0 + c * LANES:c0 + (c + 1) * LANES] for c in range(slabs)], axis=0
        ).astype(BF16)
        kh = k_ref[pl.ds(b * per_seq + h, width, stride=heads), :].astype(BF16)
        g = lax.dot_general(qh, kh, (((1,), (1,)), ((), ())), preferred_element_type=F32)
        s = g[0:rows]
        for c in range(1, slabs):
            s = jnp.where(is_slab[c], g[c * rows:(c + 1) * rows], s)
        parts.append(s)
    s = jnp.concatenate(parts, axis=0)

    lane = lax.broadcasted_iota(jnp.int32, s.shape, 1)
    step = 1
    while step < slabs:
        s = s + jnp.where((lane & step) != 0, pltpu.roll(s, step, 1), pltpu.roll(s, width - step, 1))
        step *= 2
    s = s * scale
    e = jnp.exp(s - jnp.max(s, axis=-1, keepdims=True))
    p = e / (jnp.sum(e, axis=-1, keepdims=True) * (1.0 / slabs))

    for i, (b, h) in enumerate(pairs):
        r0, c0 = b * rows, h * head_dim
        ph = p[i * rows:(i + 1) * rows]
        w = jnp.concatenate([jnp.where(is_slab[c], ph, 0.0) for c in range(slabs)], axis=0).astype(BF16)
        vh = v_ref[pl.ds(b * per_seq + h, width, stride=heads), :].astype(BF16)
        o = jnp.dot(w, vh, preferred_element_type=F32)
        for c in range(slabs):
            o_ref[r0:r0 + rows, c0 + c * LANES:c0 + (c + 1) * LANES] = o[c * rows:(c + 1) * rows]


def _rows_view(a):
    n, m, heads, hd = a.shape
    a = a.reshape(n * m, heads, hd // LANES, LANES)
    return a.transpose(0, 2, 1, 3).reshape(n * m * hd // LANES * heads, LANES)


def attn_rows(q, k, v, *, n_seq, name):
    n, n_mem, heads, hd = k.shape
    t, dm = q.shape
    rows = t // n
    slabs = hd // LANES
    assert slabs & (slabs - 1) == 0 and rows % 8 == 0
    per_seq = n_mem * slabs * heads
    kv_spec = pl.BlockSpec((n_seq * per_seq, LANES), lambda i: (i, 0))
    q_spec = pl.BlockSpec((n_seq * rows, dm), lambda i: (i, 0))
    kern = functools.partial(_attn_rows_kernel, n_seq=n_seq, rows=rows, heads=heads, head_dim=hd, n_mem=n_mem)
    return pl.pallas_call(
        kern, grid=(n // n_seq,),
        in_specs=[q_spec, kv_spec, kv_spec], out_specs=q_spec,
        out_shape=jax.ShapeDtypeStruct((t, dm), F32),
        compiler_params=_params("parallel"), name=name,
    )(q, _rows_view(k), _rows_view(v))


def _attn_out_kernel(h_ref, o_ref, wxo_ref, gn_ref, out_ref, hn_ref):
    _residual_and_norm(h_ref[...], o_ref[...].astype(BF16), wxo_ref, gn_ref, out_ref, hn_ref)


def attn_out(h, o, wxo, g_next, *, tm, name):
    t, dm = h.shape
    row_spec = pl.BlockSpec((tm, dm), lambda i: (i, 0))
    return pl.pallas_call(
        _attn_out_kernel, grid=(t // tm,),
        in_specs=[row_spec, row_spec, _resident(wxo.shape), _resident((1, dm))],
        out_specs=[row_spec, row_spec],
        out_shape=[jax.ShapeDtypeStruct((t, dm), F32), jax.ShapeDtypeStruct((t, dm), BF16)],
        compiler_params=_params("parallel"), name=name,
    )(h, o, wxo, g_next.reshape(1, dm))


def _mlp_kernel(hn_ref, h_hbm, wup_ref, wdn_ref, gf_ref, y_ref, *rest, tm, emit_w):
    wub_ref, wdb_ref, hbuf, sem = rest if emit_w else (None, None) + tuple(rest)
    i, j = pl.program_id(0), pl.program_id(1)

    def residual_rows(tile):
        return pltpu.make_async_copy(h_hbm.at[pl.ds(tile * tm, tm), :], hbuf, sem.at[0])

    def up():
        a = jnp.dot(hn_ref[...], _bf16_tile(wup_ref, wub_ref), preferred_element_type=F32)
        return jnp.square(jnp.maximum(a, 0.0)).astype(BF16)

    def down(a):
        return jnp.dot(a, _bf16_tile(wdn_ref, wdb_ref), preferred_element_type=F32)

    @pl.when((i == 0) & (j == 0))
    def _():
        residual_rows(0).start()

    @pl.when(j == 0)
    def _():
        a = up()
        residual_rows(i).wait()
        y_ref[...] = hbuf[...] + down(a)

        @pl.when(i + 1 < pl.num_programs(0))
        def _():
            residual_rows(i + 1).start()

    @pl.when(j > 0)
    def _():
        y_ref[...] += down(up())

    @pl.when(j == pl.num_programs(1) - 1)
    def _():
        y_ref[...] = _rms(y_ref[...], gf_ref[...])


def mlp(hn, h, w_up, w_down, g_final, *, tm, fc, name, emit_w=False):
    t, dm = h.shape
    dff = w_up.shape[1]
    one_tile = t == tm
    assert one_tile or not emit_w
    once = dict(pipeline_mode=pl.Buffered(1)) if one_tile else {}
    up_spec = pl.BlockSpec((dm, fc), lambda i, j: (0, j))
    dn_spec = pl.BlockSpec((fc, dm), lambda i, j: (j, 0))
    out_specs = [pl.BlockSpec((tm, dm), lambda i, j: (i, 0), **once)]
    out_shape = [jax.ShapeDtypeStruct((t, dm), F32)]
    if emit_w:
        out_specs += [up_spec, dn_spec]
        out_shape += [jax.ShapeDtypeStruct(w_up.shape, BF16), jax.ShapeDtypeStruct(w_down.shape, BF16)]
    outs = pl.pallas_call(
        functools.partial(_mlp_kernel, tm=tm, emit_w=emit_w), grid=(t // tm, dff // fc),
        in_specs=[pl.BlockSpec((tm, dm), lambda i, j: (i, 0), **once),
                  pl.BlockSpec(memory_space=pl.ANY),
                  up_spec, dn_spec,
                  pl.BlockSpec((1, dm), lambda i, j: (0, 0))],
        out_specs=out_specs, out_shape=out_shape,
        scratch_shapes=[pltpu.VMEM((tm, dm), F32), pltpu.SemaphoreType.DMA((1,))],
        compiler_params=_params("arbitrary", "arbitrary"), name=name,
    )(hn, h, w_up, w_down, g_final.reshape(1, dm))
    return outs if emit_w else outs[0]


MIX_TM = 256
ROW_TM = 1024
COL_TN = 1024
WIDE_TN = 2304
ATTN_TM = 512
MLP_FC = 1024
CAST_FC = 512


def _gate_len(seq_len, w_spatial):
    gate_len = CHUNK if seq_len % CHUNK == 0 else seq_len
    assert gate_len & (gate_len - 1) == 0 and MIX_TM % gate_len == 0 and gate_len <= w_spatial.shape[1]
    return gate_len


def kernel(x_prompt, x_sample, state_conv, cache_mem_k, cache_mem_v, mem_prompt, norm_mix_g, w_in, ln_v_g, ln_v_b, w_spatial, b_spatial, conv_w, w_branch_a, w_branch_b, w_mix_out, norm_x_g, norm_mem_g, w_q, w_k, w_v, w_x_out, norm_mlp_g, w_up, w_down, norm_final_g):
    depth = w_in.shape[0]
    assert depth == 1, "the final rmsnorm is fused into the single layer's MLP kernel"
    nb, seq, dm = x_prompt.shape
    nd, dseq, _ = x_sample.shape
    n_mem, heads, hd = cache_mem_k.shape[2:]
    l = 0
    xs = x_sample.reshape(nd * dseq, dm)
    xp = x_prompt.reshape(nb * seq, dm)
    assert xs.shape[0] == ROW_TM

    proj_s, w_in_b = rms_matmul(xs, norm_mix_g[l], w_in[l], BF16, tm=ROW_TM, tn=COL_TN, emit_w=True,
                                name="in_proj_sample")
    proj_p, w_a, w_b, w_mix, w_q_b, w_xo = rms_matmul(
        xp, norm_mix_g[l], w_in_b, BF16, tm=ROW_TM, tn=WIDE_TN, name="in_proj_prompt",
        side_cast=(w_branch_a[l], w_branch_b[l], w_mix_out[l], w_q[l], w_x_out[l]))
    mix_w = (ln_v_g[l], ln_v_b[l], w_spatial[l], b_spatial[l], conv_w[l], w_a, w_b, w_mix)

    prev = state_conv[l]
    e1 = jnp.pad(prev[:, 1:2], ((0, 0), (0, dseq - 1), (0, 0))).reshape(nd * dseq, -1)
    e2 = jnp.pad(prev, ((0, 0), (0, dseq - 2), (0, 0))).reshape(nd * dseq, -1)
    h_s, p_s, vn_s = mix(xs, proj_s, (e1, e2), *mix_w, tm=MIX_TM, gate_len=_gate_len(dseq, w_spatial[l]),
                         seq_len=dseq, name="mix_sample")
    q_s = rms_matmul(h_s, norm_x_g[l], w_q_b, F32, tm=ROW_TM, tn=COL_TN, name="q_proj_sample")
    o_s = attn_rows(q_s, cache_mem_k[l], cache_mem_v[l], n_seq=4, name="attn_sample")
    h_s, hn_s = attn_out(h_s, o_s, w_xo, norm_mlp_g[l], tm=ATTN_TM, name="x_out_sample")
    y_s, w_up_b, w_down_b = mlp(hn_s, h_s, w_up[l], w_down[l], norm_final_g, tm=ROW_TM, fc=CAST_FC,
                                name="mlp_sample", emit_w=True)

    mem = mem_prompt.reshape(nb * n_mem, dm)
    k_b, v_b, k_rows, v_rows = mem_kv(mem, norm_mem_g[l], w_k[l], w_v[l], heads=heads, name="mem_kv")
    h_p, tail_p = mix(xp, proj_p, None, *mix_w, tm=MIX_TM, gate_len=_gate_len(seq, w_spatial[l]),
                      seq_len=seq, name="mix_prompt")
    h_p, hn_p = attn_block(h_p, norm_x_g[l], w_q_b, k_b.reshape(nb, n_mem, dm), v_b.reshape(nb, n_mem, dm), w_xo,
                           norm_mlp_g[l], heads=heads, tm=ATTN_TM, seq_len=seq, name="attn_block_prompt")
    y_p = mlp(hn_p, h_p, w_up_b, w_down_b, norm_final_g, tm=ROW_TM, fc=MLP_FC, name="mlp_prompt")
    k_p = _cache_view(k_rows, nb, n_mem, heads, hd)
    v_p = _cache_view(v_rows, nb, n_mem, heads, hd)

    bw = p_s.shape[1]
    keep = conv_w.shape[1] - 1
    conv_p = tail_p.reshape(nb, seq // MIX_TM, 8, bw)[:, -1, 8 - keep:, :]
    conv_s = p_s.reshape(nd, dseq, bw)[:, dseq - keep:, :]
    return (y_p.reshape(nb, seq, dm), y_s.reshape(nd, dseq, dm), k_p[None], v_p[None],
            conv_p[None], conv_s[None], vn_s.reshape(1, nd, dseq, -1))
```

```python
import functools

import jax
import jax.numpy as jnp
from jax import lax
from jax.experimental import pallas as pl
from jax.experimental.pallas import tpu as pltpu

EPS = 1e-6
CHUNK = 128
F32 = jnp.float32
BF16 = jnp.bfloat16

V7X_VMEM_BYTES = 64 * 1024 * 1024
VMEM_LIMIT = V7X_VMEM_BYTES * 7 // 8


def _params(*sem):
    return pltpu.CompilerParams(dimension_semantics=sem, vmem_limit_bytes=VMEM_LIMIT)


def _rms(x, g):
    r = lax.rsqrt(jnp.mean(x * x, axis=-1, keepdims=True) + EPS)
    return (x * r) * g


def _resident(shape):
    return pl.BlockSpec(shape, lambda *_: (0,) * len(shape), pipeline_mode=pl.Buffered(1))


def _bf16_tile(w_ref, wb_ref):
    if wb_ref is None:
        return w_ref[...].astype(BF16)
    wb_ref[...] = w_ref[...].astype(BF16)
    return wb_ref[...]


def _rms_matmul_kernel(x_hbm, g_ref, w_ref, *rest, emit_w, n_side):
    side_in, rest = rest[:n_side], rest[n_side:]
    o_ref, rest = rest[0], rest[1:]
    wb_ref, rest = (rest[0], rest[1:]) if emit_w else (None, rest)
    side_out, (xn_ref, xbuf, sem, *ring) = rest[:n_side], rest[n_side:]
    i, j = pl.program_id(0), pl.program_id(1)
    tm = xbuf.shape[0]

    if emit_w:
        wbuf, wsem = ring
        nj, tn = pl.num_programs(1), wbuf.shape[2]

        def w_tile(t):
            slot = t % WEIGHT_RING
            return pltpu.make_async_copy(w_ref.at[:, pl.ds(pl.multiple_of(t * tn, tn), tn)], wbuf.at[slot],
                                         wsem.at[slot])

        @pl.when(j == 0)
        def _():
            w_tile(0).start()
            w_tile(1).start()

        @pl.when(j + 2 < nj)
        def _():
            w_tile(j + 2).start()

    def x_rows(tile):
        return pltpu.make_async_copy(x_hbm.at[pl.ds(tile * tm, tm), :], xbuf, sem.at[0])

    @pl.when((i == 0) & (j == 0))
    def _():
        x_rows(0).start()

    @pl.when(j == 0)
    def _():
        x_rows(i).wait()
        xn_ref[...] = _rms(xbuf[...], g_ref[...]).astype(BF16)

        @pl.when(i + 1 < pl.num_programs(0))
        def _():
            x_rows(i + 1).start()

    if emit_w:
        w_tile(j).wait()
        w = _bf16_tile(wbuf.at[j % WEIGHT_RING], wb_ref)
    else:
        w = _bf16_tile(w_ref, None)
    o_ref[...] = jnp.dot(xn_ref[...], w, preferred_element_type=F32).astype(o_ref.dtype)
    for src, dst in zip(side_in, side_out):
        dst[...] = src[...].astype(BF16)


BF16_SUBLANES = 16
WEIGHT_RING = 3


def _side_chunks(rows, steps):
    return max(n for n in range(1, steps + 1) if rows % n == 0 and (rows // n) % BF16_SUBLANES == 0)


def rms_matmul(x, g, w, out_dtype, *, tm, tn, name, emit_w=False, side_cast=()):
    t, d = x.shape
    n = w.shape[1]
    nj = n // tn
    assert not emit_w or (t == tm and nj >= 2)
    steps = (t // tm) * nj
    w_spec = pl.BlockSpec((d, tn), lambda i, j: (0, j))
    out_specs = [pl.BlockSpec((tm, tn), lambda i, j: (i, j))]
    out_shape = [jax.ShapeDtypeStruct((t, n), out_dtype)]
    scratch = [pltpu.VMEM((tm, d), BF16), pltpu.VMEM((tm, d), F32), pltpu.SemaphoreType.DMA((1,))]
    if emit_w:
        out_specs.append(w_spec)
        out_shape.append(jax.ShapeDtypeStruct(w.shape, BF16))
        scratch += [pltpu.VMEM((WEIGHT_RING, d, tn), w.dtype), pltpu.SemaphoreType.DMA((WEIGHT_RING,))]
    side_specs = []
    for s in side_cast:
        chunks = _side_chunks(s.shape[0], steps)
        side_specs.append(pl.BlockSpec((s.shape[0] // chunks, s.shape[1]),
                                       lambda i, j, c=chunks: (jnp.minimum(i * nj + j, c - 1), 0)))
        out_shape.append(jax.ShapeDtypeStruct(s.shape, BF16))
    outs = pl.pallas_call(
        functools.partial(_rms_matmul_kernel, emit_w=emit_w, n_side=len(side_cast)),
        grid=(t // tm, nj),
        in_specs=[pl.BlockSpec(memory_space=pl.ANY), pl.BlockSpec((1, d), lambda i, j: (0, 0)),
                  pl.BlockSpec(memory_space=pl.ANY) if emit_w else w_spec] + side_specs,
        out_specs=out_specs + side_specs, out_shape=out_shape,
        scratch_shapes=scratch,
        compiler_params=_params("arbitrary", "arbitrary"),
        name=name,
    )(x, g.reshape(1, d), w, *side_cast)
    return outs if len(outs) > 1 else outs[0]


LANES = 128


def _mem_kv_kernel(x_ref, g_ref, wk_ref, wv_ref, k_ref, v_ref, kc_ref, vc_ref, xn_ref, *, heads):
    head = pl.program_id(0)

    @pl.when(head == 0)
    def _():
        xn_ref[...] = _rms(x_ref[...], g_ref[...]).astype(BF16)

    rows = x_ref.shape[0]
    slabs = wk_ref.shape[1] // LANES
    for w_ref, rows_ref, cache_ref in ((wk_ref, k_ref, kc_ref), (wv_ref, v_ref, vc_ref)):
        o = jnp.dot(xn_ref[...], w_ref[...].astype(BF16), preferred_element_type=F32)
        rows_ref[...] = o.astype(rows_ref.dtype)
        for c in range(slabs):
            cache_ref[pl.ds(c * heads + head, rows, stride=slabs * heads), :] = o[:, c * LANES:(c + 1) * LANES]


def mem_kv(mem, g, w_k, w_v, *, heads, name):
    t, d = mem.shape
    hd = w_k.shape[1] // heads
    once = dict(pipeline_mode=pl.Buffered(1))
    w_spec = pl.BlockSpec((d, hd), lambda j: (0, j))
    rows_spec = pl.BlockSpec((t, hd), lambda j: (0, j))
    cache_spec = pl.BlockSpec((t * d // LANES, LANES), lambda j: (0, 0), **once)
    return pl.pallas_call(
        functools.partial(_mem_kv_kernel, heads=heads), grid=(heads,),
        in_specs=[pl.BlockSpec((t, d), lambda j: (0, 0), **once), pl.BlockSpec((1, d), lambda j: (0, 0)),
                  w_spec, w_spec],
        out_specs=[rows_spec, rows_spec, cache_spec, cache_spec],
        out_shape=[jax.ShapeDtypeStruct((t, d), BF16)] * 2 + [jax.ShapeDtypeStruct((t * d // LANES, LANES), F32)] * 2,
        scratch_shapes=[pltpu.VMEM((t, d), BF16)],
        compiler_params=_params("arbitrary"), name=name,
    )(mem, g.reshape(1, d), w_k, w_v)


def _cache_view(rows, n, m, heads, hd):
    a = rows.reshape(n * m, hd // LANES, heads, LANES)
    return a.transpose(0, 2, 1, 3).reshape(n, m, heads, hd)


def _mix_kernel(*refs, tm, gate_len, widths, groups, prompt, tiles_per_seq):
    if prompt:
        (x_ref, proj_ref, cgp_ref, xinp_ref, lng_ref, lnb_ref, wsp_ref, bsp_ref, cw_ref,
         wa_ref, wb_ref, wmix_ref, h_ref, tail_ref, gate_ref, gbias_ref) = refs
    else:
        (x_ref, proj_ref, e1_ref, e2_ref, lng_ref, lnb_ref, wsp_ref, bsp_ref, cw_ref,
         wa_ref, wb_ref, wmix_ref, h_ref, p_ref, v_ref, gate_ref, gbias_ref) = refs
    aw, bw, dm = widths
    o_u, o_v, o_bg, o_cg, o_xin, o_ga, o_gb = (0, aw, 2 * aw, 2 * aw + bw, 2 * aw + 2 * bw,
                                               2 * aw + 3 * bw, 2 * aw + 3 * bw + dm)

    def col(o, w):
        return proj_ref[:, o:o + w].astype(F32)

    gd = aw // groups

    @pl.when(pl.program_id(0) == 0)
    def _():
        row = lax.broadcasted_iota(jnp.int32, (tm, tm), 0)
        cidx = lax.broadcasted_iota(jnp.int32, (tm, tm), 1)
        keep = ((row ^ cidx) < gate_len) & (cidx <= row)
        cs = wsp_ref.shape[1]
        pick = (lax.broadcasted_iota(jnp.int32, (tm, cs), 1)
                == (lax.broadcasted_iota(jnp.int32, (tm, cs), 0) & (gate_len - 1)))
        pick_b = jnp.where(pick, 1.0, 0.0).astype(BF16)
        for g in range(groups):
            w_rows = jnp.dot(pick_b, wsp_ref[g].astype(BF16), preferred_element_type=F32).astype(BF16)
            w_full = lax.dot_general(w_rows, pick_b, (((1,), (1,)), ((), ())), preferred_element_type=F32)
            gate_ref[g] = jnp.where(keep, w_full, 0.0).astype(BF16)
            gbias_ref[g] = jnp.sum(jnp.where(pick, bsp_ref[g:g + 1, :], 0.0), axis=1, keepdims=True)

    p = col(o_cg, bw) * col(o_xin, bw)
    trow = lax.broadcasted_iota(jnp.int32, (tm, bw), 0)
    if prompt:
        fresh = pl.program_id(0) % tiles_per_seq == 0
        pp = cgp_ref[...].astype(F32) * xinp_ref[...].astype(F32)
        pp = jnp.where(fresh, 0.0, pp)
        last1 = pp[-1:, :]
        last2 = pp[-2:-1, :]
        e1 = jnp.where(trow == 0, last1, 0.0)
        e2 = jnp.where(trow == 0, last2, jnp.where(trow == 1, last1, 0.0))
        tail_ref[...] = p[tm - 8:, :]
    else:
        trow = trow & (gate_len - 1)

        def per_row(hist_ref):
            hist = hist_ref[...]
            return jnp.broadcast_to(hist[:, None, :], (tm // gate_len, gate_len, bw)).reshape(tm, bw)

        older, newer = per_row(e2_ref), per_row(e1_ref)
        e1 = jnp.where(trow == 0, newer, 0.0)
        e2 = jnp.where(trow == 0, older, jnp.where(trow == 1, newer, 0.0))
        p_ref[...] = p
    s1 = jnp.where(trow >= 1, pltpu.roll(p, 1, 0), 0.0) + e1
    s2 = jnp.where(trow >= 2, pltpu.roll(p, 2, 0), 0.0) + e2
    conv = cw_ref[0:1, :] * s2 + cw_ref[1:2, :] * s1 + cw_ref[2:3, :] * p
    y_b = (col(o_bg, bw) * conv).astype(BF16)
    branch_b = jax.nn.sigmoid(col(o_gb, dm)) * jnp.dot(y_b, wb_ref[...], preferred_element_type=F32)

    v = col(o_v, aw)
    mu = jnp.mean(v, axis=-1, keepdims=True)
    vc = v - mu
    var = jnp.mean(vc * vc, axis=-1, keepdims=True)
    vn = (vc * lax.rsqrt(var + EPS)) * lng_ref[...] + lnb_ref[...]
    if not prompt:
        v_ref[...] = vn
    vb = vn.astype(BF16)
    zs = [jnp.dot(gate_ref[g], vb[:, g * gd:(g + 1) * gd], preferred_element_type=F32) + gbias_ref[g]
          for g in range(groups)]
    y_a = (col(o_u, aw) * jnp.concatenate(zs, axis=1)).astype(BF16)
    branch_a = jax.nn.sigmoid(col(o_ga, dm)) * jnp.dot(y_a, wa_ref[...], preferred_element_type=F32)

    h_ref[...] = x_ref[...] + jnp.dot((branch_a + branch_b).astype(BF16), wmix_ref[...],
                                      preferred_element_type=F32)


def mix(x, proj, prev, ln_g, ln_b, w_sp, b_sp, conv_w, wa, wb, wmix, *, tm, gate_len, seq_len, name):
    t, dm = x.shape
    aw, bw = wa.shape[0], wb.shape[0]
    groups = w_sp.shape[0]
    prompt = prev is None
    nt = t // tm
    row_spec = lambda w: pl.BlockSpec((tm, w), lambda i: (i, 0))
    in_specs = [row_spec(dm), row_spec(proj.shape[1])]
    if prompt:
        prev_rows = 16
        cg_blk = (2 * aw + bw) // bw
        in_specs += [
            pl.BlockSpec((prev_rows, bw), lambda i: (jnp.maximum(i * (tm // prev_rows) - 1, 0), cg_blk)),
            pl.BlockSpec((prev_rows, bw), lambda i: (jnp.maximum(i * (tm // prev_rows) - 1, 0), cg_blk + 1)),
        ]
        extra = (proj, proj)
    else:
        assert seq_len == gate_len and all(h.shape == (t // seq_len, bw) for h in prev)
        hist_spec = pl.BlockSpec((tm // seq_len, bw), lambda i: (i, 0))
        in_specs += [hist_spec, hist_spec]
        extra = prev
    in_specs += [_resident((1, aw)), _resident((1, aw)), _resident(w_sp.shape), _resident(b_sp.shape),
                 _resident(conv_w.shape), _resident(wa.shape), _resident(wb.shape), _resident(wmix.shape)]
    out_specs = [row_spec(dm)]
    out_shape = [jax.ShapeDtypeStruct((t, dm), F32)]
    if prompt:
        out_specs.append(pl.BlockSpec((8, bw), lambda i: (i, 0)))
        out_shape.append(jax.ShapeDtypeStruct((nt * 8, bw), F32))
    else:
        out_specs += [row_spec(bw), row_spec(aw)]
        out_shape += [jax.ShapeDtypeStruct((t, bw), F32), jax.ShapeDtypeStruct((t, aw), F32)]
    kern = functools.partial(_mix_kernel, tm=tm, gate_len=gate_len, widths=(aw, bw, dm), groups=groups,
                             prompt=prompt, tiles_per_seq=max(seq_len // tm, 1))
    return pl.pallas_call(
        kern, grid=(nt,), in_specs=in_specs, out_specs=out_specs, out_shape=out_shape,
        scratch_shapes=[pltpu.VMEM((groups, tm, tm), BF16), pltpu.VMEM((groups, tm, 1), F32)],
        compiler_params=_params("arbitrary"), name=name,
    )(x, proj, *extra, ln_g.reshape(1, aw), ln_b.reshape(1, aw), w_sp, b_sp, conv_w, wa, wb, wmix)


def _attn_block_kernel(h_ref, g_ref, wq_ref, k_ref, v_ref, wxo_ref, gn_ref, out_ref, hn_ref, *, heads, head_dim):
    scale = head_dim ** -0.5
    h = h_ref[...]
    q = jnp.dot(_rms(h, g_ref[...]).astype(BF16), wq_ref[...], preferred_element_type=F32).astype(BF16)
    head_cols = [slice(hd * head_dim, (hd + 1) * head_dim) for hd in range(heads)]
    scores = [lax.dot_general(q[:, cols], k_ref[0, :, cols].astype(BF16), (((1,), (1,)), ((), ())),
                              preferred_element_type=F32) * scale for cols in head_cols]
    probs = []
    for s in scores:
        e = jnp.exp(s - jnp.max(s, axis=-1, keepdims=True))
        probs.append((e / jnp.sum(e, axis=-1, keepdims=True)).astype(BF16))
    outs = [jnp.dot(p, v_ref[0, :, cols].astype(BF16), preferred_element_type=F32).astype(BF16)
            for p, cols in zip(probs, head_cols)]
    o = jnp.concatenate(outs, axis=1)
    _residual_and_norm(h, o, wxo_ref, gn_ref, out_ref, hn_ref)


def _residual_and_norm(h, o, wxo_ref, gn_ref, out_ref, hn_ref):
    out = h + jnp.dot(o, wxo_ref[...], preferred_element_type=F32)
    out_ref[...] = out
    hn_ref[...] = _rms(out, gn_ref[...]).astype(BF16)


def attn_block(h, g, wq, k, v, wxo, g_next, *, heads, tm, seq_len, name):
    t, dm = h.shape
    m = k.shape[1]
    per_seq = seq_len // tm
    row_spec = pl.BlockSpec((tm, dm), lambda i: (i, 0))
    kv_spec = pl.BlockSpec((1, m, dm), lambda i: (i // per_seq, 0, 0))
    kern = functools.partial(_attn_block_kernel, heads=heads, head_dim=dm // heads)
    return pl.pallas_call(
        kern, grid=(t // tm,),
        in_specs=[row_spec, _resident((1, dm)), _resident(wq.shape), kv_spec, kv_spec, _resident(wxo.shape),
                  _resident((1, dm))],
        out_specs=[row_spec, row_spec],
        out_shape=[jax.ShapeDtypeStruct((t, dm), F32), jax.ShapeDtypeStruct((t, dm), BF16)],
        compiler_params=_params("parallel"), name=name,
    )(h, g.reshape(1, dm), wq, k, v, wxo, g_next.reshape(1, dm))


def _attn_rows_kernel(q_ref, k_ref, v_ref, o_ref, *, n_seq, rows, heads, head_dim, n_mem):
    scale = head_dim ** -0.5
    slabs = head_dim // LANES
    per_seq = n_mem * slabs * heads
    width = n_mem * slabs
    pairs = [(b, h) for b in range(n_seq) for h in range(heads)]
    lane_slab = lax.broadcasted_iota(jnp.int32, (rows, width), 1) & (slabs - 1)
    is_slab = [lane_slab == c for c in range(slabs)]

    parts = []
    for b, h in pairs:
        r0, c0 = b * rows, h * head_dim
        qh = jnp.concatenate(
            [q_ref[r0:r0 + rows, c0 + c * LANES:c0 + (c + 1) * LANES] for c in range(slabs)], axis=0
        ).astype(BF16)
        kh = k_ref[pl.ds(b * per_seq + h, width, stride=heads), :].astype(BF16)
        g = lax.dot_general(qh, kh, (((1,), (1,)), ((), ())), preferred_element_type=F32)
        s = g[0:rows]
        for c in range(1, slabs):
            s = jnp.where(is_slab[c], g[c * rows:(c + 1) * rows], s)
        parts.append(s)
    s = jnp.concatenate(parts, axis=0)

    lane = lax.broadcasted_iota(jnp.int32, s.shape, 1)
    step = 1
    while step < slabs:
        s = s + jnp.where((lane & step) != 0, pltpu.roll(s, step, 1), pltpu.roll(s, width - step, 1))
        step *= 2
    s = s * scale
    e = jnp.exp(s - jnp.max(s, axis=-1, keepdims=True))
    p = e / (jnp.sum(e, axis=-1, keepdims=True) * (1.0 / slabs))

    for i, (b, h) in enumerate(pairs):
        r0, c0 = b * rows, h * head_dim
        ph = p[i * rows:(i + 1) * rows]
        w = jnp.concatenate([jnp.where(is_slab[c], ph, 0.0) for c in range(slabs)], axis=0).astype(BF16)
        vh = v_ref[pl.ds(b * per_seq + h, width, stride=heads), :].astype(BF16)
        o = jnp.dot(w, vh, preferred_element_type=F32)
        for c in range(slabs):
            o_ref[r0:r0 + rows, c0 + c * LANES:c0 + (c + 1) * LANES] = o[c * rows:(c + 1) * rows]


def _rows_view(a):
    n, m, heads, hd = a.shape
    a = a.reshape(n * m, heads, hd // LANES, LANES)
    return a.transpose(0, 2, 1, 3).reshape(n * m * hd // LANES * heads, LANES)


def attn_rows(q, k, v, *, n_seq, name):
    n, n_mem, heads, hd = k.shape
    t, dm = q.shape
    rows = t // n
    slabs = hd // LANES
    assert slabs & (slabs - 1) == 0 and rows % 8 == 0
    per_seq = n_mem * slabs * heads
    kv_spec = pl.BlockSpec((n_seq * per_seq, LANES), lambda i: (i, 0))
    q_spec = pl.BlockSpec((n_seq * rows, dm), lambda i: (i, 0))
    kern = functools.partial(_attn_rows_kernel, n_seq=n_seq, rows=rows, heads=heads, head_dim=hd, n_mem=n_mem)
    return pl.pallas_call(
        kern, grid=(n // n_seq,),
        in_specs=[q_spec, kv_spec, kv_spec], out_specs=q_spec,
        out_shape=jax.ShapeDtypeStruct((t, dm), F32),
        compiler_params=_params("parallel"), name=name,
    )(q, _rows_view(k), _rows_view(v))


def _attn_out_kernel(h_ref, o_ref, wxo_ref, gn_ref, out_ref, hn_ref):
    _residual_and_norm(h_ref[...], o_ref[...].astype(BF16), wxo_ref, gn_ref, out_ref, hn_ref)


def attn_out(h, o, wxo, g_next, *, tm, name):
    t, dm = h.shape
    row_spec = pl.BlockSpec((tm, dm), lambda i: (i, 0))
    return pl.pallas_call(
        _attn_out_kernel, grid=(t // tm,),
        in_specs=[row_spec, row_spec, _resident(wxo.shape), _resident((1, dm))],
        out_specs=[row_spec, row_spec],
        out_shape=[jax.ShapeDtypeStruct((t, dm), F32), jax.ShapeDtypeStruct((t, dm), BF16)],
        compiler_params=_params("parallel"), name=name,
    )(h, o, wxo, g_next.reshape(1, dm))


def _mlp_kernel(hn_ref, h_hbm, wup_ref, wdn_ref, gf_ref, y_ref, *rest, tm, emit_w):
    wub_ref, wdb_ref, hbuf, sem = rest if emit_w else (None, None) + tuple(rest)
    i, j = pl.program_id(0), pl.program_id(1)

    def residual_rows(tile):
        return pltpu.make_async_copy(h_hbm.at[pl.ds(tile * tm, tm), :], hbuf, sem.at[0])

    def up():
        a = jnp.dot(hn_ref[...], _bf16_tile(wup_ref, wub_ref), preferred_element_type=F32)
        return jnp.square(jnp.maximum(a, 0.0)).astype(BF16)

    def down(a):
        return jnp.dot(a, _bf16_tile(wdn_ref, wdb_ref), preferred_element_type=F32)

    @pl.when((i == 0) & (j == 0))
    def _():
        residual_rows(0).start()

    @pl.when(j == 0)
    def _():
        a = up()
        residual_rows(i).wait()
        y_ref[...] = hbuf[...] + down(a)

        @pl.when(i + 1 < pl.num_programs(0))
        def _():
            residual_rows(i + 1).start()

    @pl.when(j > 0)
    def _():
        y_ref[...] += down(up())

    @pl.when(j == pl.num_programs(1) - 1)
    def _():
        y_ref[...] = _rms(y_ref[...], gf_ref[...])


def mlp(hn, h, w_up, w_down, g_final, *, tm, fc, name, emit_w=False):
    t, dm = h.shape
    dff = w_up.shape[1]
    one_tile = t == tm
    assert one_tile or not emit_w
    once = dict(pipeline_mode=pl.Buffered(1)) if one_tile else {}
    up_spec = pl.BlockSpec((dm, fc), lambda i, j: (0, j))
    dn_spec = pl.BlockSpec((fc, dm), lambda i, j: (j, 0))
    out_specs = [pl.BlockSpec((tm, dm), lambda i, j: (i, 0), **once)]
    out_shape = [jax.ShapeDtypeStruct((t, dm), F32)]
    if emit_w:
        out_specs += [up_spec, dn_spec]
        out_shape += [jax.ShapeDtypeStruct(w_up.shape, BF16), jax.ShapeDtypeStruct(w_down.shape, BF16)]
    outs = pl.pallas_call(
        functools.partial(_mlp_kernel, tm=tm, emit_w=emit_w), grid=(t // tm, dff // fc),
        in_specs=[pl.BlockSpec((tm, dm), lambda i, j: (i, 0), **once),
                  pl.BlockSpec(memory_space=pl.ANY),
                  up_spec, dn_spec,
                  pl.BlockSpec((1, dm), lambda i, j: (0, 0))],
        out_specs=out_specs, out_shape=out_shape,
        scratch_shapes=[pltpu.VMEM((tm, dm), F32), pltpu.SemaphoreType.DMA((1,))],
        compiler_params=_params("arbitrary", "arbitrary"), name=name,
    )(hn, h, w_up, w_down, g_final.reshape(1, dm))
    return outs if emit_w else outs[0]


MIX_TM = 256
ROW_TM = 1024
COL_TN = 1024
WIDE_TN = 2304
ATTN_TM = 512
MLP_FC = 1024
CAST_FC = 512


def _gate_len(seq_len, w_spatial):
    gate_len = CHUNK if seq_len % CHUNK == 0 else seq_len
    assert gate_len & (gate_len - 1) == 0 and MIX_TM % gate_len == 0 and gate_len <= w_spatial.shape[1]
    return gate_len


def kernel(x_prompt, x_sample, state_conv, cache_mem_k, cache_mem_v, mem_prompt, norm_mix_g, w_in, ln_v_g, ln_v_b, w_spatial, b_spatial, conv_w, w_branch_a, w_branch_b, w_mix_out, norm_x_g, norm_mem_g, w_q, w_k, w_v, w_x_out, norm_mlp_g, w_up, w_down, norm_final_g):
    depth = w_in.shape[0]
    assert depth == 1, "the final rmsnorm is fused into the single layer's MLP kernel"
    nb, seq, dm = x_prompt.shape
    nd, dseq, _ = x_sample.shape
    n_mem, heads, hd = cache_mem_k.shape[2:]
    l = 0
    xs = x_sample.reshape(nd * dseq, dm)
    xp = x_prompt.reshape(nb * seq, dm)
    assert xs.shape[0] == ROW_TM

    proj_s, w_in_b = rms_matmul(xs, norm_mix_g[l], w_in[l], BF16, tm=ROW_TM, tn=COL_TN, emit_w=True,
                                name="in_proj_sample")
    proj_p, w_a, w_b, w_mix, w_q_b, w_xo = rms_matmul(
        xp, norm_mix_g[l], w_in_b, BF16, tm=ROW_TM, tn=WIDE_TN, name="in_proj_prompt",
        side_cast=(w_branch_a[l], w_branch_b[l], w_mix_out[l], w_q[l], w_x_out[l]))
    mix_w = (ln_v_g[l], ln_v_b[l], w_spatial[l], b_spatial[l], conv_w[l], w_a, w_b, w_mix)

    prev = state_conv[l]
    h_s, p_s, vn_s = mix(xs, proj_s, (prev[:, -1], prev[:, -2]), *mix_w, tm=MIX_TM, gate_len=_gate_len(dseq, w_spatial[l]),
                         seq_len=dseq, name="mix_sample")
    q_s = rms_matmul(h_s, norm_x_g[l], w_q_b, F32, tm=ROW_TM, tn=COL_TN, name="q_proj_sample")
    o_s = attn_rows(q_s, cache_mem_k[l], cache_mem_v[l], n_seq=4, name="attn_sample")
    h_s, hn_s = attn_out(h_s, o_s, w_xo, norm_mlp_g[l], tm=ATTN_TM, name="x_out_sample")
    y_s, w_up_b, w_down_b = mlp(hn_s, h_s, w_up[l], w_down[l], norm_final_g, tm=ROW_TM, fc=CAST_FC,
                                name="mlp_sample", emit_w=True)

    mem = mem_prompt.reshape(nb * n_mem, dm)
    k_b, v_b, k_rows, v_rows = mem_kv(mem, norm_mem_g[l], w_k[l], w_v[l], heads=heads, name="mem_kv")
    h_p, tail_p = mix(xp, proj_p, None, *mix_w, tm=MIX_TM, gate_len=_gate_len(seq, w_spatial[l]),
                      seq_len=seq, name="mix_prompt")
    h_p, hn_p = attn_block(h_p, norm_x_g[l], w_q_b, k_b.reshape(nb, n_mem, dm), v_b.reshape(nb, n_mem, dm), w_xo,
                           norm_mlp_g[l], heads=heads, tm=ATTN_TM, seq_len=seq, name="attn_block_prompt")
    y_p = mlp(hn_p, h_p, w_up_b, w_down_b, norm_final_g, tm=ROW_TM, fc=MLP_FC, name="mlp_prompt")
    k_p = _cache_view(k_rows, nb, n_mem, heads, hd)
    v_p = _cache_view(v_rows, nb, n_mem, heads, hd)

    bw = p_s.shape[1]
    keep = conv_w.shape[1] - 1
    conv_p = tail_p.reshape(nb, seq // MIX_TM, 8, bw)[:, -1, 8 - keep:, :]
    conv_s = p_s.reshape(nd, dseq, bw)[:, dseq - keep:, :]
    return (y_p.reshape(nb, seq, dm), y_s.reshape(nd, dseq, dm), k_p[None], v_p[None],
            conv_p[None], conv_s[None], vn_s.reshape(1, nd, dseq, -1))
```

```python
import functools

import jax
import jax.numpy as jnp
from jax import lax
from jax.experimental import pallas as pl
from jax.experimental.pallas import tpu as pltpu

EPS = 1e-6
CHUNK = 128
F32 = jnp.float32
BF16 = jnp.bfloat16

V7X_VMEM_BYTES = 64 * 1024 * 1024
VMEM_LIMIT = V7X_VMEM_BYTES * 7 // 8


def _params(*sem):
    return pltpu.CompilerParams(dimension_semantics=sem, vmem_limit_bytes=VMEM_LIMIT)


def _rms(x, g):
    r = lax.rsqrt(jnp.mean(x * x, axis=-1, keepdims=True) + EPS)
    return (x * r) * g


def _resident(shape):
    return pl.BlockSpec(shape, lambda *_: (0,) * len(shape), pipeline_mode=pl.Buffered(1))


def _bf16_tile(w_ref, wb_ref):
    if wb_ref is None:
        return w_ref[...].astype(BF16)
    wb_ref[...] = w_ref[...].astype(BF16)
    return wb_ref[...]


def _rms_matmul_kernel(x_hbm, g_ref, w_ref, *rest, emit_w, n_side):
    side_in, rest = rest[:n_side], rest[n_side:]
    o_ref, rest = rest[0], rest[1:]
    wb_ref, rest = (rest[0], rest[1:]) if emit_w else (None, rest)
    side_out, (xn_ref, xbuf, sem, *ring) = rest[:n_side], rest[n_side:]
    i, j = pl.program_id(0), pl.program_id(1)
    tm = xbuf.shape[0]

    if emit_w:
        wbuf, wsem = ring
        nj, tn = pl.num_programs(1), wbuf.shape[2]

        def w_tile(t):
            slot = t % WEIGHT_RING
            return pltpu.make_async_copy(w_ref.at[:, pl.ds(pl.multiple_of(t * tn, tn), tn)], wbuf.at[slot],
                                         wsem.at[slot])

        @pl.when(j == 0)
        def _():
            w_tile(0).start()
            w_tile(1).start()

        @pl.when(j + 2 < nj)
        def _():
            w_tile(j + 2).start()

    def x_rows(tile):
        return pltpu.make_async_copy(x_hbm.at[pl.ds(tile * tm, tm), :], xbuf, sem.at[0])

    @pl.when((i == 0) & (j == 0))
    def _():
        x_rows(0).start()

    @pl.when(j == 0)
    def _():
        x_rows(i).wait()
        xn_ref[...] = _rms(xbuf[...], g_ref[...]).astype(BF16)

        @pl.when(i + 1 < pl.num_programs(0))
        def _():
            x_rows(i + 1).start()

    if emit_w:
        w_tile(j).wait()
        w = _bf16_tile(wbuf.at[j % WEIGHT_RING], wb_ref)
    else:
        w = _bf16_tile(w_ref, None)
    o_ref[...] = jnp.dot(xn_ref[...], w, preferred_element_type=F32).astype(o_ref.dtype)
    for src, dst in zip(side_in, side_out):
        dst[...] = src[...].astype(BF16)


BF16_SUBLANES = 16
WEIGHT_RING = 3


def _side_chunks(rows, steps):
    return max(n for n in range(1, steps + 1) if rows % n == 0 and (rows // n) % BF16_SUBLANES == 0)


def rms_matmul(x, g, w, out_dtype, *, tm, tn, name, emit_w=False, side_cast=()):
    t, d = x.shape
    n = w.shape[1]
    nj = n // tn
    assert not emit_w or (t == tm and nj >= 2)
    steps = (t // tm) * nj
    w_spec = pl.BlockSpec((d, tn), lambda i, j: (0, j))
    out_specs = [pl.BlockSpec((tm, tn), lambda i, j: (i, j))]
    out_shape = [jax.ShapeDtypeStruct((t, n), out_dtype)]
    scratch = [pltpu.VMEM((tm, d), BF16), pltpu.VMEM((tm, d), F32), pltpu.SemaphoreType.DMA((1,))]
    if emit_w:
        out_specs.append(w_spec)
        out_shape.append(jax.ShapeDtypeStruct(w.shape, BF16))
        scratch += [pltpu.VMEM((WEIGHT_RING, d, tn), w.dtype), pltpu.SemaphoreType.DMA((WEIGHT_RING,))]
    side_specs = []
    for s in side_cast:
        chunks = _side_chunks(s.shape[0], steps)
        side_specs.append(pl.BlockSpec((s.shape[0] // chunks, s.shape[1]),
                                       lambda i, j, c=chunks: (jnp.minimum(i * nj + j, c - 1), 0)))
        out_shape.append(jax.ShapeDtypeStruct(s.shape, BF16))
    outs = pl.pallas_call(
        functools.partial(_rms_matmul_kernel, emit_w=emit_w, n_side=len(side_cast)),
        grid=(t // tm, nj),
        in_specs=[pl.BlockSpec(memory_space=pl.ANY), pl.BlockSpec((1, d), lambda i, j: (0, 0)),
                  pl.BlockSpec(memory_space=pl.ANY) if emit_w else w_spec] + side_specs,
        out_specs=out_specs + side_specs, out_shape=out_shape,
        scratch_shapes=scratch,
        compiler_params=_params("arbitrary", "arbitrary"),
        name=name,
    )(x, g.reshape(1, d), w, *side_cast)
    return outs if len(outs) > 1 else outs[0]


LANES = 128


def _mem_kv_kernel(x_ref, g_ref, wk_ref, wv_ref, k_ref, v_ref, kc_ref, vc_ref, xn_ref, *, heads):
    head = pl.program_id(0)

    @pl.when(head == 0)
    def _():
        xn_ref[...] = _rms(x_ref[...], g_ref[...]).astype(BF16)

    rows = x_ref.shape[0]
    slabs = wk_ref.shape[1] // LANES
    for w_ref, rows_ref, cache_ref in ((wk_ref, k_ref, kc_ref), (wv_ref, v_ref, vc_ref)):
        o = jnp.dot(xn_ref[...], w_ref[...].astype(BF16), preferred_element_type=F32)
        rows_ref[...] = o.astype(rows_ref.dtype)
        for c in range(slabs):
            cache_ref[pl.ds(c * heads + head, rows, stride=slabs * heads), :] = o[:, c * LANES:(c + 1) * LANES]


def mem_kv(mem, g, w_k, w_v, *, heads, name):
    t, d = mem.shape
    hd = w_k.shape[1] // heads
    once = dict(pipeline_mode=pl.Buffered(1))
    w_spec = pl.BlockSpec((d, hd), lambda j: (0, j))
    rows_spec = pl.BlockSpec((t, hd), lambda j: (0, j))
    cache_spec = pl.BlockSpec((t * d // LANES, LANES), lambda j: (0, 0), **once)
    return pl.pallas_call(
        functools.partial(_mem_kv_kernel, heads=heads), grid=(heads,),
        in_specs=[pl.BlockSpec((t, d), lambda j: (0, 0), **once), pl.BlockSpec((1, d), lambda j: (0, 0)),
                  w_spec, w_spec],
        out_specs=[rows_spec, rows_spec, cache_spec, cache_spec],
        out_shape=[jax.ShapeDtypeStruct((t, d), BF16)] * 2 + [jax.ShapeDtypeStruct((t * d // LANES, LANES), F32)] * 2,
        scratch_shapes=[pltpu.VMEM((t, d), BF16)],
        compiler_params=_params("arbitrary"), name=name,
    )(mem, g.reshape(1, d), w_k, w_v)


def _cache_view(rows, n, m, heads, hd):
    a = rows.reshape(n * m, hd // LANES, heads, LANES)
    return a.transpose(0, 2, 1, 3).reshape(n, m, heads, hd)


def _mix_kernel(*refs, tm, gate_len, widths, groups, prompt, tiles_per_seq):
    if prompt:
        (x_ref, proj_ref, cgp_ref, xinp_ref, lng_ref, lnb_ref, wsp_ref, bsp_ref, cw_ref,
         wa_ref, wb_ref, wmix_ref, h_ref, tail_ref, gate_ref, gbias_ref) = refs
    else:
        (x_ref, proj_ref, e1_ref, e2_ref, lng_ref, lnb_ref, wsp_ref, bsp_ref, cw_ref,
         wa_ref, wb_ref, wmix_ref, h_ref, p_ref, v_ref, gate_ref, gbias_ref) = refs
    aw, bw, dm = widths
    o_u, o_v, o_bg, o_cg, o_xin, o_ga, o_gb = (0, aw, 2 * aw, 2 * aw + bw, 2 * aw + 2 * bw,
                                               2 * aw + 3 * bw, 2 * aw + 3 * bw + dm)

    def col(o, w):
        return proj_ref[:, o:o + w].astype(F32)

    gd = aw // groups

    @pl.when(pl.program_id(0) == 0)
    def _():
        row = lax.broadcasted_iota(jnp.int32, (tm, tm), 0)
        cidx = lax.broadcasted_iota(jnp.int32, (tm, tm), 1)
        keep = ((row ^ cidx) < gate_len) & (cidx <= row)
        cs = wsp_ref.shape[1]
        pick = (lax.broadcasted_iota(jnp.int32, (tm, cs), 1)
                == (lax.broadcasted_iota(jnp.int32, (tm, cs), 0) & (gate_len - 1)))
        pick_b = jnp.where(pick, 1.0, 0.0).astype(BF16)
        for g in range(groups):
            w_rows = jnp.dot(pick_b, wsp_ref[g].astype(BF16), preferred_element_type=F32).astype(BF16)
            w_full = lax.dot_general(w_rows, pick_b, (((1,), (1,)), ((), ())), preferred_element_type=F32)
            gate_ref[g] = jnp.where(keep, w_full, 0.0).astype(BF16)
            gbias_ref[g] = jnp.sum(jnp.where(pick, bsp_ref[g:g + 1, :], 0.0), axis=1, keepdims=True)

    p = col(o_cg, bw) * col(o_xin, bw)
    trow = lax.broadcasted_iota(jnp.int32, (tm, bw), 0)
    if prompt:
        fresh = pl.program_id(0) % tiles_per_seq == 0
        pp = cgp_ref[...].astype(F32) * xinp_ref[...].astype(F32)
        pp = jnp.where(fresh, 0.0, pp)
        last1 = pp[-1:, :]
        last2 = pp[-2:-1, :]
        e1 = jnp.where(trow == 0, last1, 0.0)
        e2 = jnp.where(trow == 0, last2, jnp.where(trow == 1, last1, 0.0))
        tail_ref[...] = p[tm - 8:, :]
    else:
        trow = trow & (gate_len - 1)

        def per_row(hist_ref):
            hist = hist_ref[...]
            return jnp.broadcast_to(hist[:, None, :], (tm // gate_len, gate_len, bw)).reshape(tm, bw)

        older, newer = per_row(e2_ref), per_row(e1_ref)
        e1 = jnp.where(trow == 0, newer, 0.0)
        e2 = jnp.where(trow == 0, older, jnp.where(trow == 1, newer, 0.0))
        p_ref[...] = p
    s1 = jnp.where(trow >= 1, pltpu.roll(p, 1, 0), 0.0) + e1
    s2 = jnp.where(trow >= 2, pltpu.roll(p, 2, 0), 0.0) + e2
    conv = cw_ref[0:1, :] * s2 + cw_ref[1:2, :] * s1 + cw_ref[2:3, :] * p
    y_b = (col(o_bg, bw) * conv).astype(BF16)
    branch_b = jax.nn.sigmoid(col(o_gb, dm)) * jnp.dot(y_b, wb_ref[...], preferred_element_type=F32)

    v = col(o_v, aw)
    mu = jnp.mean(v, axis=-1, keepdims=True)
    vc = v - mu
    var = jnp.mean(vc * vc, axis=-1, keepdims=True)
    vn = (vc * lax.rsqrt(var + EPS)) * lng_ref[...] + lnb_ref[...]
    if not prompt:
        v_ref[...] = vn
    vb = vn.astype(BF16)
    zs = [jnp.dot(gate_ref[g], vb[:, g * gd:(g + 1) * gd], preferred_element_type=F32) + gbias_ref[g]
          for g in range(groups)]
    y_a = (col(o_u, aw) * jnp.concatenate(zs, axis=1)).astype(BF16)
    branch_a = jax.nn.sigmoid(col(o_ga, dm)) * jnp.dot(y_a, wa_ref[...], preferred_element_type=F32)

    h_ref[...] = x_ref[...] + jnp.dot((branch_a + branch_b).astype(BF16), wmix_ref[...],
                                      preferred_element_type=F32)


def mix(x, proj, prev, ln_g, ln_b, w_sp, b_sp, conv_w, wa, wb, wmix, *, tm, gate_len, seq_len, name):
    t, dm = x.shape
    aw, bw = wa.shape[0], wb.shape[0]
    groups = w_sp.shape[0]
    prompt = prev is None
    nt = t // tm
    row_spec = lambda w: pl.BlockSpec((tm, w), lambda i: (i, 0))
    in_specs = [row_spec(dm), row_spec(proj.shape[1])]
    if prompt:
        prev_rows = 16
        cg_blk = (2 * aw + bw) // bw
        in_specs += [
            pl.BlockSpec((prev_rows, bw), lambda i: (jnp.maximum(i * (tm // prev_rows) - 1, 0), cg_blk)),
            pl.BlockSpec((prev_rows, bw), lambda i: (jnp.maximum(i * (tm // prev_rows) - 1, 0), cg_blk + 1)),
        ]
        extra = (proj, proj)
    else:
        assert seq_len == gate_len and all(h.shape == (t // seq_len, bw) for h in prev)
        hist_spec = pl.BlockSpec((tm // seq_len, bw), lambda i: (i, 0))
        in_specs += [hist_spec, hist_spec]
        extra = prev
    in_specs += [_resident((1, aw)), _resident((1, aw)), _resident(w_sp.shape), _resident(b_sp.shape),
                 _resident(conv_w.shape), _resident(wa.shape), _resident(wb.shape), _resident(wmix.shape)]
    out_specs = [row_spec(dm)]
    out_shape = [jax.ShapeDtypeStruct((t, dm), F32)]
    if prompt:
        out_specs.append(pl.BlockSpec((8, bw), lambda i: (i, 0)))
        out_shape.append(jax.ShapeDtypeStruct((nt * 8, bw), F32))
    else:
        out_specs += [row_spec(bw), row_spec(aw)]
        out_shape += [jax.ShapeDtypeStruct((t, bw), F32), jax.ShapeDtypeStruct((t, aw), F32)]
    kern = functools.partial(_mix_kernel, tm=tm, gate_len=gate_len, widths=(aw, bw, dm), groups=groups,
                             prompt=prompt, tiles_per_seq=max(seq_len // tm, 1))
    return pl.pallas_call(
        kern, grid=(nt,), in_specs=in_specs, out_specs=out_specs, out_shape=out_shape,
        scratch_shapes=[pltpu.VMEM((groups, tm, tm), BF16), pltpu.VMEM((groups, tm, 1), F32)],
        compiler_params=_params("arbitrary"), name=name,
    )(x, proj, *extra, ln_g.reshape(1, aw), ln_b.reshape(1, aw), w_sp, b_sp, conv_w, wa, wb, wmix)


def _attn_block_kernel(h_ref, g_ref, wq_ref, k_ref, v_ref, wxo_ref, gn_ref, out_ref, hn_ref, *, heads, head_dim):
    scale = head_dim ** -0.5
    h = h_ref[...]
    q = jnp.dot(_rms(h, g_ref[...]).astype(BF16), wq_ref[...], preferred_element_type=F32).astype(BF16)
    head_cols = [slice(hd * head_dim, (hd + 1) * head_dim) for hd in range(heads)]
    scores = [lax.dot_general(q[:, cols], k_ref[0, :, cols].astype(BF16), (((1,), (1,)), ((), ())),
                              preferred_element_type=F32) * scale for cols in head_cols]
    probs = []
    for s in scores:
        e = jnp.exp(s - jnp.max(s, axis=-1, keepdims=True))
        probs.append((e / jnp.sum(e, axis=-1, keepdims=True)).astype(BF16))
    outs = [jnp.dot(p, v_ref[0, :, cols].astype(BF16), preferred_element_type=F32).astype(BF16)
            for p, cols in zip(probs, head_cols)]
    o = jnp.concatenate(outs, axis=1)
    _residual_and_norm(h, o, wxo_ref, gn_ref, out_ref, hn_ref)


def _residual_and_norm(h, o, wxo_ref, gn_ref, out_ref, hn_ref):
    out = h + jnp.dot(o, wxo_ref[...], preferred_element_type=F32)
    out_ref[...] = out
    hn_ref[...] = _rms(out, gn_ref[...]).astype(BF16)


def attn_block(h, g, wq, k, v, wxo, g_next, *, heads, tm, seq_len, name):
    t, dm = h.shape
    m = k.shape[1]
    per_seq = seq_len // tm
    row_spec = pl.BlockSpec((tm, dm), lambda i: (i, 0))
    kv_spec = pl.BlockSpec((1, m, dm), lambda i: (i // per_seq, 0, 0))
    kern = functools.partial(_attn_block_kernel, heads=heads, head_dim=dm // heads)
    return pl.pallas_call(
        kern, grid=(t // tm,),
        in_specs=[row_spec, _resident((1, dm)), _resident(wq.shape), kv_spec, kv_spec, _resident(wxo.shape),
                  _resident((1, dm))],
        out_specs=[row_spec, row_spec],
        out_shape=[jax.ShapeDtypeStruct((t, dm), F32), jax.ShapeDtypeStruct((t, dm), BF16)],
        compiler_params=_params("parallel"), name=name,
    )(h, g.reshape(1, dm), wq, k, v, wxo, g_next.reshape(1, dm))


def _attn_rows_kernel(q_ref, k_ref, v_ref, o_ref, *, n_seq, rows, heads, head_dim, n_mem):
    scale = head_dim ** -0.5
    slabs = head_dim // LANES
    per_seq = n_mem * slabs * heads
    width = n_mem * slabs
    pairs = [(b, h) for b in range(n_seq) for h in range(heads)]
    lane_slab = lax.broadcasted_iota(jnp.int32, (rows, width), 1) & (slabs - 1)
    is_slab = [lane_slab == c for c in range(slabs)]

    parts = []
    for b, h in pairs:
        r0, c0 = b * rows, h * head_dim
        qh = jnp.concatenate(
            [q_ref[r0:r0 + rows, c0 + c * LANES:c0 + (c + 1) * LANES] for c in range(slabs)], axis=0
        ).astype(BF16)
        kh = k_ref[pl.ds(b * per_seq + h, width, stride=heads), :].astype(BF16)
        g = lax.dot_general(qh, kh, (((1,), (1,)), ((), ())), preferred_element_type=F32)
        s = g[0:rows]
        for c in range(1, slabs):
            s = jnp.where(is_slab[c], g[c * rows:(c + 1) * rows], s)
        parts.append(s)
    s = jnp.concatenate(parts, axis=0)

    lane = lax.broadcasted_iota(jnp.int32, s.shape, 1)
    step = 1
    while step < slabs:
        s = s + jnp.where((lane & step) != 0, pltpu.roll(s, step, 1), pltpu.roll(s, width - step, 1))
        step *= 2
    s = s * scale
    e = jnp.exp(s - jnp.max(s, axis=-1, keepdims=True))
    p = e / (jnp.sum(e, axis=-1, keepdims=True) * (1.0 / slabs))

    for i, (b, h) in enumerate(pairs):
        r0, c0 = b * rows, h * head_dim
        ph = p[i * rows:(i + 1) * rows]
        w = jnp.concatenate([jnp.where(is_slab[c], ph, 0.0) for c in range(slabs)], axis=0).astype(BF16)
        vh = v_ref[pl.ds(b * per_seq + h, width, stride=heads), :].astype(BF16)
        o = jnp.dot(w, vh, preferred_element_type=F32)
        for c in range(slabs):
            o_ref[r0:r0 + rows, c0 + c * LANES:c0 + (c + 1) * LANES] = o[c * rows:(c + 1) * rows]


def _rows_view(a):
    n, m, heads, hd = a.shape
    a = a.reshape(n * m, heads, hd // LANES, LANES)
    return a.transpose(0, 2, 1, 3).reshape(n * m * hd // LANES * heads, LANES)


def attn_rows(q, k, v, *, n_seq, name):
    n, n_mem, heads, hd = k.shape
    t, dm = q.shape
    rows = t // n
    slabs = hd // LANES
    assert slabs & (slabs - 1) == 0 and rows % 8 == 0
    per_seq = n_mem * slabs * heads
    blk = n_seq * per_seq
    steps = n // n_seq
    assert steps >= 2
    q_spec = pl.BlockSpec((n_seq * rows, dm), lambda i: (i, 0))
    math = functools.partial(_attn_rows_kernel, n_seq=n_seq, rows=rows, heads=heads, head_dim=hd, n_mem=n_mem)

    def kern(q_ref, k_hbm, v_hbm, o_ref, kbuf, vbuf, sem):
        s = pl.program_id(0)

        def blocks(b):
            slot = b % WEIGHT_RING
            rows_b = pl.ds(pl.multiple_of(b * blk, blk), blk)
            return (pltpu.make_async_copy(k_hbm.at[rows_b, :], kbuf.at[slot], sem.at[0, slot]),
                    pltpu.make_async_copy(v_hbm.at[rows_b, :], vbuf.at[slot], sem.at[1, slot]))

        @pl.when(s == 0)
        def _():
            for copy in blocks(0) + blocks(1):
                copy.start()

        @pl.when(s + 2 < steps)
        def _():
            for copy in blocks(s + 2):
                copy.start()

        for copy in blocks(s):
            copy.wait()
        slot = s % WEIGHT_RING
        math(q_ref, kbuf.at[slot], vbuf.at[slot], o_ref)

    ring = pltpu.VMEM((WEIGHT_RING, blk, LANES), F32)
    return pl.pallas_call(
        kern, grid=(steps,),
        in_specs=[q_spec, pl.BlockSpec(memory_space=pl.ANY), pl.BlockSpec(memory_space=pl.ANY)], out_specs=q_spec,
        out_shape=jax.ShapeDtypeStruct((t, dm), F32),
        scratch_shapes=[ring, ring, pltpu.SemaphoreType.DMA((2, WEIGHT_RING))],
        compiler_params=_params("arbitrary"), name=name,
    )(q, _rows_view(k), _rows_view(v))


def _attn_out_kernel(h_ref, o_ref, wxo_ref, gn_ref, out_ref, hn_ref):
    _residual_and_norm(h_ref[...], o_ref[...].astype(BF16), wxo_ref, gn_ref, out_ref, hn_ref)


def attn_out(h, o, wxo, g_next, *, tm, name):
    t, dm = h.shape
    row_spec = pl.BlockSpec((tm, dm), lambda i: (i, 0))
    return pl.pallas_call(
        _attn_out_kernel, grid=(t // tm,),
        in_specs=[row_spec, row_spec, _resident(wxo.shape), _resident((1, dm))],
        out_specs=[row_spec, row_spec],
        out_shape=[jax.ShapeDtypeStruct((t, dm), F32), jax.ShapeDtypeStruct((t, dm), BF16)],
        compiler_params=_params("parallel"), name=name,
    )(h, o, wxo, g_next.reshape(1, dm))


def _mlp_kernel(hn_ref, h_hbm, wup_ref, wdn_ref, gf_ref, y_ref, *rest, tm, emit_w):
    wub_ref, wdb_ref, hbuf, sem = rest if emit_w else (None, None) + tuple(rest)
    i, j = pl.program_id(0), pl.program_id(1)

    def residual_rows(tile):
        return pltpu.make_async_copy(h_hbm.at[pl.ds(tile * tm, tm), :], hbuf, sem.at[0])

    def up():
        a = jnp.dot(hn_ref[...], _bf16_tile(wup_ref, wub_ref), preferred_element_type=F32)
        return jnp.square(jnp.maximum(a, 0.0)).astype(BF16)

    def down(a):
        return jnp.dot(a, _bf16_tile(wdn_ref, wdb_ref), preferred_element_type=F32)

    @pl.when((i == 0) & (j == 0))
    def _():
        residual_rows(0).start()

    @pl.when(j == 0)
    def _():
        a = up()
        residual_rows(i).wait()
        y_ref[...] = hbuf[...] + down(a)

        @pl.when(i + 1 < pl.num_programs(0))
        def _():
            residual_rows(i + 1).start()

    @pl.when(j > 0)
    def _():
        y_ref[...] += down(up())

    @pl.when(j == pl.num_programs(1) - 1)
    def _():
        y_ref[...] = _rms(y_ref[...], gf_ref[...])


def mlp(hn, h, w_up, w_down, g_final, *, tm, fc, name, emit_w=False):
    t, dm = h.shape
    dff = w_up.shape[1]
    one_tile = t == tm
    assert one_tile or not emit_w
    once = dict(pipeline_mode=pl.Buffered(1)) if one_tile else {}
    up_spec = pl.BlockSpec((dm, fc), lambda i, j: (0, j))
    dn_spec = pl.BlockSpec((fc, dm), lambda i, j: (j, 0))
    out_specs = [pl.BlockSpec((tm, dm), lambda i, j: (i, 0), **once)]
    out_shape = [jax.ShapeDtypeStruct((t, dm), F32)]
    if emit_w:
        out_specs += [up_spec, dn_spec]
        out_shape += [jax.ShapeDtypeStruct(w_up.shape, BF16), jax.ShapeDtypeStruct(w_down.shape, BF16)]
    outs = pl.pallas_call(
        functools.partial(_mlp_kernel, tm=tm, emit_w=emit_w), grid=(t // tm, dff // fc),
        in_specs=[pl.BlockSpec((tm, dm), lambda i, j: (i, 0), **once),
                  pl.BlockSpec(memory_space=pl.ANY),
                  up_spec, dn_spec,
                  pl.BlockSpec((1, dm), lambda i, j: (0, 0))],
        out_specs=out_specs, out_shape=out_shape,
        scratch_shapes=[pltpu.VMEM((tm, dm), F32), pltpu.SemaphoreType.DMA((1,))],
        compiler_params=_params("arbitrary", "arbitrary"), name=name,
    )(hn, h, w_up, w_down, g_final.reshape(1, dm))
    return outs if emit_w else outs[0]


MIX_TM = 256
ROW_TM = 1024
COL_TN = 1024
WIDE_TN = 2304
ATTN_TM = 512
MLP_FC = 1024
CAST_FC = 512


def _gate_len(seq_len, w_spatial):
    gate_len = CHUNK if seq_len % CHUNK == 0 else seq_len
    assert gate_len & (gate_len - 1) == 0 and MIX_TM % gate_len == 0 and gate_len <= w_spatial.shape[1]
    return gate_len


def kernel(x_prompt, x_sample, state_conv, cache_mem_k, cache_mem_v, mem_prompt, norm_mix_g, w_in, ln_v_g, ln_v_b, w_spatial, b_spatial, conv_w, w_branch_a, w_branch_b, w_mix_out, norm_x_g, norm_mem_g, w_q, w_k, w_v, w_x_out, norm_mlp_g, w_up, w_down, norm_final_g):
    depth = w_in.shape[0]
    assert depth == 1, "the final rmsnorm is fused into the single layer's MLP kernel"
    nb, seq, dm = x_prompt.shape
    nd, dseq, _ = x_sample.shape
    n_mem, heads, hd = cache_mem_k.shape[2:]
    l = 0
    xs = x_sample.reshape(nd * dseq, dm)
    xp = x_prompt.reshape(nb * seq, dm)
    assert xs.shape[0] == ROW_TM

    proj_s, w_in_b = rms_matmul(xs, norm_mix_g[l], w_in[l], BF16, tm=ROW_TM, tn=COL_TN, emit_w=True,
                                name="in_proj_sample")
    proj_p, w_a, w_b, w_mix, w_q_b, w_xo = rms_matmul(
        xp, norm_mix_g[l], w_in_b, BF16, tm=ROW_TM, tn=WIDE_TN, name="in_proj_prompt",
        side_cast=(w_branch_a[l], w_branch_b[l], w_mix_out[l], w_q[l], w_x_out[l]))
    mix_w = (ln_v_g[l], ln_v_b[l], w_spatial[l], b_spatial[l], conv_w[l], w_a, w_b, w_mix)

    prev = state_conv[l]
    h_s, p_s, vn_s = mix(xs, proj_s, (prev[:, -1], prev[:, -2]), *mix_w, tm=MIX_TM, gate_len=_gate_len(dseq, w_spatial[l]),
                         seq_len=dseq, name="mix_sample")
    q_s = rms_matmul(h_s, norm_x_g[l], w_q_b, F32, tm=ROW_TM, tn=COL_TN, name="q_proj_sample")
    o_s = attn_rows(q_s, cache_mem_k[l], cache_mem_v[l], n_seq=4, name="attn_sample")
    h_s, hn_s = attn_out(h_s, o_s, w_xo, norm_mlp_g[l], tm=ATTN_TM, name="x_out_sample")
    y_s, w_up_b, w_down_b = mlp(hn_s, h_s, w_up[l], w_down[l], norm_final_g, tm=ROW_TM, fc=CAST_FC,
                                name="mlp_sample", emit_w=True)

    mem = mem_prompt.reshape(nb * n_mem, dm)
    k_b, v_b, k_rows, v_rows = mem_kv(mem, norm_mem_g[l], w_k[l], w_v[l], heads=heads, name="mem_kv")
    h_p, tail_p = mix(xp, proj_p, None, *mix_w, tm=MIX_TM, gate_len=_gate_len(seq, w_spatial[l]),
                      seq_len=seq, name="mix_prompt")
    h_p, hn_p = attn_block(h_p, norm_x_g[l], w_q_b, k_b.reshape(nb, n_mem, dm), v_b.reshape(nb, n_mem, dm), w_xo,
                           norm_mlp_g[l], heads=heads, tm=ATTN_TM, seq_len=seq, name="attn_block_prompt")
    y_p = mlp(hn_p, h_p, w_up_b, w_down_b, norm_final_g, tm=ROW_TM, fc=MLP_FC, name="mlp_prompt")
    k_p = _cache_view(k_rows, nb, n_mem, heads, hd)
    v_p = _cache_view(v_rows, nb, n_mem, heads, hd)

    bw = p_s.shape[1]
    keep = conv_w.shape[1] - 1
    conv_p = tail_p.reshape(nb, seq // MIX_TM, 8, bw)[:, -1, 8 - keep:, :]
    conv_s = p_s.reshape(nd, dseq, bw)[:, dseq - keep:, :]
    return (y_p.reshape(nb, seq, dm), y_s.reshape(nd, dseq, dm), k_p[None], v_p[None],
            conv_p[None], conv_s[None], vn_s.reshape(1, nd, dseq, -1))
```
